```python
import math
import jax, jax.numpy as jnp
from jax import lax
import numpy as np

D_MODEL = 1024
BATCH = 4
SEQ = 8192
DEPTH = 4

GRID_W = 64
CTX_LEN = 256
S5_GROUP = 16
S5_GROUPS = D_MODEL // S5_GROUP
S5_STATE = 64
S5_DT_MIN = 0.001
S5_DT_MAX = 0.1
GLA_HEADS = 4
GLA_DK = D_MODEL // (2 * GLA_HEADS)
GLA_DV = D_MODEL // GLA_HEADS
GLA_GATE_RANK = 16
GLA_GATE_NORM = 16.0
GLA_CHUNK = 64
N_EXPERTS = 32
TOP_K = 4
EXPERT_FF = D_MODEL
SWIGLU_ALPHA = 1.702
SWIGLU_LIMIT = 7.0
MOE_BLOCK = 128
DN_ALPHA = (2 * DEPTH) ** 0.25
DN_BETA = (8 * DEPTH) ** -0.25
LN_EPS = 1e-5
N_S5_LAYERS = (DEPTH + 1) // 2
N_GLA_LAYERS = DEPTH // 2

kernel_name = 'hybrid_s5_gla_moe_deepnorm_dit'

F32 = jnp.float32


def _layer_norm(x, g, b):
    xf = x.astype(F32)
    mu = jnp.mean(xf, axis=-1, keepdims=True)
    var = jnp.mean(jnp.square(xf - mu), axis=-1, keepdims=True)
    return ((xf - mu) * lax.rsqrt(var + LN_EPS) * g.astype(F32) + b.astype(F32)).astype(x.dtype)


def _modulate(x, shift, scale):
    return x * (1 + scale) + shift


def _s5_discretize(lam_re, lam_im, log_step, b_re, b_im):
    lr = lam_re.astype(F32)
    li = lam_im.astype(F32)
    dt = jnp.exp(log_step.astype(F32))[:, None]
    mag = jnp.exp(lr * dt)
    ar = mag * jnp.cos(li * dt)
    ai = mag * jnp.sin(li * dt)
    den = lr * lr + li * li
    nr = ar - 1.0
    kr = (nr * lr + ai * li) / den
    ki = (ai * lr - nr * li) / den
    br = b_re.astype(F32)
    bi = b_im.astype(F32)
    bbr = kr[..., None] * br - ki[..., None] * bi
    bbi = kr[..., None] * bi + ki[..., None] * br
    return ar, ai, bbr, bbi


def _cplx_affine_op(e1, e2):
    a1r, a1i, b1r, b1i = e1
    a2r, a2i, b2r, b2i = e2
    return (a2r * a1r - a2i * a1i, a2r * a1i + a2i * a1r,
            a2r * b1r - a2i * b1i + b2r, a2r * b1i + a2i * b1r + b2i)


def _s5_scan(u, ar, ai, bbr, bbi, cr, ci, s0, reverse, readout):
    l = u.shape[1]
    bu_re = jnp.einsum('blgh,gph->blgp', u, bbr)
    bu_im = jnp.einsum('blgh,gph->blgp', u, bbi)
    shp = (1, l) + ar.shape
    acc_re, acc_im, s_re, s_im = lax.associative_scan(
        _cplx_affine_op,
        (jnp.broadcast_to(ar, shp), jnp.broadcast_to(ai, shp), bu_re, bu_im),
        reverse=reverse, axis=1)
    if s0 is not None:
        s0r = s0[0][:, None]
        s0i = s0[1][:, None]
        s_re = s_re + acc_re * s0r - acc_im * s0i
        s_im = s_im + acc_re * s0i + acc_im * s0r
    end = 0 if reverse else l - 1
    s_end = (s_re[:, end], s_im[:, end])
    if not readout:
        return None, s_end
    y = jnp.einsum('blgp,ghp->blgh', s_re, cr) - jnp.einsum('blgp,ghp->blgh', s_im, ci)
    return y, s_end


def _s5_glu(y, w_glu):
    d = y.shape[-1]
    z = jax.nn.gelu(y) @ w_glu.astype(F32)
    return z[..., :d] * jax.nn.sigmoid(z[..., d:])


def _s5_mixer(u, uc, lam_re, lam_im, log_step, b_re, b_im, c_re, c_im, d_skip, w_glu, ctx_out):
    bn, l, d = u.shape
    lc = uc.shape[1]
    ul = u.astype(F32)
    ucf = uc.astype(F32)
    dsk = d_skip.astype(F32)
    grp = lambda t: t.reshape(t.shape[0], t.shape[1], S5_GROUPS, S5_GROUP)
    y = dsk * ul
    yc = dsk * ucf if ctx_out else None
    for dr, rev in ((0, False), (1, True)):
        ar, ai, bbr, bbi = _s5_discretize(lam_re[dr], lam_im[dr], log_step[dr], b_re[dr], b_im[dr])
        cr = c_re[dr].astype(F32)
        ci = c_im[dr].astype(F32)
        y_ctx, s_ctx = _s5_scan(grp(ucf), ar, ai, bbr, bbi, cr, ci, None, rev, ctx_out)
        y_lat, _ = _s5_scan(grp(ul), ar, ai, bbr, bbi, cr, ci, s_ctx, rev, True)
        y = y + y_lat.reshape(bn, l, d)
        if ctx_out:
            yc = yc + y_ctx.reshape(bn, lc, d)
    out = _s5_glu(y, w_glu).astype(u.dtype)
    out_c = _s5_glu(yc, w_glu).astype(uc.dtype) if ctx_out else None
    return out, out_c


def _gla_chunked(q, k, v, log_a, s0):
    bn, nh, l, dk = q.shape
    dv = v.shape[-1]
    n = l // GLA_CHUNK
    q = q.reshape(bn, nh, n, GLA_CHUNK, dk)
    k = k.reshape(bn, nh, n, GLA_CHUNK, dk)
    v = v.reshape(bn, nh, n, GLA_CHUNK, dv)
    b = jnp.cumsum(log_a.reshape(bn, nh, n, GLA_CHUNK, dk), axis=3)
    b_end = b[:, :, :, -1:, :]
    q_d = q * jnp.exp(b)
    k_d = k * jnp.exp(-b)
    k_e = k * jnp.exp(b_end - b)
    tri = jnp.tril(jnp.ones((GLA_CHUNK, GLA_CHUNK), dtype=bool))
    att = jnp.where(tri, jnp.einsum('bhnid,bhnjd->bhnij', q_d, k_d), 0.0)
    o = jnp.einsum('bhnij,bhnje->bhnie', att, v)
    u_chunk = jnp.einsum('bhnjd,bhnje->nbhde', k_e, v)
    g_chunk = jnp.moveaxis(jnp.exp(b_end[:, :, :, 0, :]), 2, 0)

    def step(s, inp):
        g, uc = inp
        return s * g[..., None] + uc, s

    s_end, s_prev = lax.scan(step, s0, (g_chunk, u_chunk))
    o = o + jnp.einsum('bhnid,nbhde->bhnie', q_d, s_prev)
    return o.reshape(bn, nh, l, dv), s_end


def _gla_mixer(u, uc, rows, w_in, w_a2, b_a2, norm_g, w_out, ctx_out):
    bn, l, d = u.shape
    u_cm = u.reshape(bn, rows, GRID_W, d).transpose(0, 2, 1, 3).reshape(bn, l, d)
    qk_w = GLA_HEADS * GLA_DK
    v_w = GLA_HEADS * GLA_DV
    splits = np.cumsum([qk_w, qk_w, v_w, v_w, GLA_GATE_RANK])

    def heads(t, hd):
        return t.reshape(t.shape[0], t.shape[1], GLA_HEADS, hd).transpose(0, 2, 1, 3)

    def project(z):
        p = (z @ w_in).astype(F32)
        q, k, v, g, a_f, a_b = jnp.split(p, splits, axis=-1)
        log_a = [heads(jax.nn.log_sigmoid(a @ w_a2[dr].astype(F32) + b_a2[dr].astype(F32)) / GLA_GATE_NORM, GLA_DK)
                 for dr, a in enumerate((a_f, a_b))]
        return heads(q, GLA_DK) * (GLA_DK ** -0.5), heads(k, GLA_DK), heads(v, GLA_DV), g, log_a

    def readout(o, g):
        o = o * lax.rsqrt(jnp.mean(o * o, axis=-1, keepdims=True) + LN_EPS) * norm_g.astype(F32)
        o = o.transpose(0, 2, 1, 3).reshape(g.shape)
        return (o * jax.nn.silu(g)) @ w_out.astype(F32)

    flip = lambda t: jnp.flip(t, axis=2)
    ql, kl, vl, gl, la_l = project(u_cm)
    qc, kc, vc, gc, la_c = project(uc)
    s_zero = jnp.zeros((bn, GLA_HEADS, GLA_DK, GLA_DV), F32)
    oc_f, sc_f = _gla_chunked(qc, kc, vc, la_c[0], s_zero)
    ol_f, _ = _gla_chunked(ql, kl, vl, la_l[0], sc_f)
    oc_b, sc_b = _gla_chunked(flip(qc), flip(kc), flip(vc), flip(la_c[1]), s_zero)
    ol_b, _ = _gla_chunked(flip(ql), flip(kl), flip(vl), flip(la_l[1]), sc_b)
    y = readout(ol_f + flip(ol_b), gl)
    y = y.reshape(bn, GRID_W, rows, d).transpose(0, 2, 1, 3).reshape(bn, l, d).astype(u.dtype)
    y_c = readout(oc_f + flip(oc_b), gc).astype(uc.dtype) if ctx_out else None
    return y, y_c


def _moe(xf, w_router, b_router, w_up, b_up, w_down, b_down):
    t, d = xf.shape
    tk = t * TOP_K
    logits = (xf @ w_router).astype(F32) + b_router.astype(F32)
    top_v, top_i = lax.top_k(logits, TOP_K)
    gate = jax.nn.softmax(top_v, axis=-1)
    e_flat = top_i.reshape(-1)
    tok_flat = jnp.arange(tk, dtype=jnp.int32) // TOP_K
    order = jnp.argsort(e_flat)
    e_s = e_flat[order]
    tok_s = tok_flat[order]
    g_s = gate.reshape(-1)[order]
    counts = jnp.bincount(e_flat, length=N_EXPERTS)
    starts = jnp.cumsum(counts) - counts
    padded = (counts + MOE_BLOCK - 1) // MOE_BLOCK * MOE_BLOCK
    pends = jnp.cumsum(padded)
    pstarts = pends - padded
    dest = pstarts[e_s] + jnp.arange(tk, dtype=jnp.int32) - starts[e_s]
    n_blocks = -(-tk // MOE_BLOCK) + N_EXPERTS
    n_rows = n_blocks * MOE_BLOCK
    x_pad = jnp.zeros((n_rows, d), xf.dtype).at[dest].set(xf[tok_s])
    tok_pad = jnp.zeros((n_rows,), jnp.int32).at[dest].set(tok_s)
    g_pad = jnp.zeros((n_rows,), F32).at[dest].set(g_s)
    blk_e = jnp.minimum(jnp.searchsorted(pends, jnp.arange(n_blocks, dtype=jnp.int32) * MOE_BLOCK, side='right'),
                        N_EXPERTS - 1)

    def expert_block(args):
        xb, e = args
        h = (xb @ w_up[e] + b_up[e]).astype(F32)
        h_glu, h_lin = jnp.split(h, 2, axis=-1)
        h_glu = jnp.minimum(h_glu, SWIGLU_LIMIT)
        h_lin = jnp.clip(h_lin, -SWIGLU_LIMIT, SWIGLU_LIMIT)
        a = h_glu * jax.nn.sigmoid(SWIGLU_ALPHA * h_glu) * (h_lin + 1.0)
        return (a.astype(xb.dtype) @ w_down[e] + b_down[e]).astype(F32)

    y = lax.map(expert_block, (x_pad.reshape(n_blocks, MOE_BLOCK, d), blk_e)).reshape(n_rows, d)
    out = jnp.zeros((t, d), F32).at[tok_pad].add(y * g_pad[:, None])
    return out.astype(xf.dtype)


def setup_inputs(seed: int = 0) -> dict:
    key = jax.random.key(seed)
    ks = iter(jax.random.split(key, 40))
    nrm = lambda shape, scale: jax.random.normal(next(ks), shape, F32) * scale
    d = D_MODEL
    na, nb = N_S5_LAYERS, N_GLA_LAYERS
    g, p, hg = S5_GROUPS, S5_STATE, S5_GROUP
    n_idx = jnp.arange(p, dtype=F32)
    inp = {}
    inp['x'] = nrm((BATCH, SEQ, d), 1.0)
    inp['c'] = nrm((BATCH, d), 1.0)
    inp['ctx'] = nrm((BATCH, CTX_LEN, d), 1.0)
    inp['c_ctx'] = nrm((d,), 1.0)
    inp['w_ada'] = nrm((DEPTH, d, 6 * d), 0.5 * d ** -0.5)
    inp['b_ada'] = nrm((DEPTH, 6 * d), 0.01)
    inp['ln1_g'] = 1.0 + nrm((DEPTH, d), 0.01)
    inp['ln1_b'] = nrm((DEPTH, d), 0.01)
    inp['ln2_g'] = 1.0 + nrm((DEPTH, d), 0.01)
    inp['ln2_b'] = nrm((DEPTH, d), 0.01)
    inp['s5_lam_re'] = -0.5 + nrm((na, 2, g, p), 0.01)
    inp['s5_lam_im'] = math.pi * n_idx + nrm((na, 2, g, p), 0.01)
    inp['s5_log_step'] = jax.random.uniform(next(ks), (na, 2, g), F32, math.log(S5_DT_MIN), math.log(S5_DT_MAX))
    inp['s5_b_re'] = nrm((na, 2, g, p, hg), (2 * hg) ** -0.5)
    inp['s5_b_im'] = nrm((na, 2, g, p, hg), (2 * hg) ** -0.5)
    inp['s5_c_re'] = nrm((na, 2, g, hg, p), p ** -0.5)
    inp['s5_c_im'] = nrm((na, 2, g, hg, p), p ** -0.5)
    inp['s5_d'] = nrm((na, d), 1.0)
    inp['s5_w_glu'] = jnp.concatenate([nrm((na, d, d), DN_BETA * d ** -0.5), nrm((na, d, d), d ** -0.5)], axis=-1)
    in_w = 2 * GLA_HEADS * GLA_DK + 2 * GLA_HEADS * GLA_DV + 2 * GLA_GATE_RANK
    inp['gla_w_in'] = nrm((nb, d, in_w), d ** -0.5)
    inp['gla_w_a2'] = nrm((nb, 2, GLA_GATE_RANK, GLA_HEADS * GLA_DK), GLA_GATE_RANK ** -0.5)
    inp['gla_b_a2'] = nrm((nb, 2, GLA_HEADS * GLA_DK), 0.1)
    inp['gla_norm_g'] = 1.0 + nrm((nb, GLA_DV), 0.01)
    inp['gla_w_out'] = nrm((nb, d, d), DN_BETA * d ** -0.5)
    inp['moe_w_router'] = nrm((DEPTH, d, N_EXPERTS), d ** -0.5)
    inp['moe_b_router'] = nrm((DEPTH, N_EXPERTS), 0.01)
    inp['moe_w_up'] = nrm((DEPTH, N_EXPERTS, d, 2 * EXPERT_FF), d ** -0.5)
    inp['moe_b_up'] = nrm((DEPTH, N_EXPERTS, 2 * EXPERT_FF), 0.01)
    inp['moe_w_down'] = nrm((DEPTH, N_EXPERTS, EXPERT_FF, d), DN_BETA * EXPERT_FF ** -0.5)
    inp['moe_b_down'] = nrm((DEPTH, N_EXPERTS, d), 0.01)
    return inp


def reference(x, c, ctx, c_ctx, w_ada, b_ada, ln1_g, ln1_b, ln2_g, ln2_b,
              s5_lam_re, s5_lam_im, s5_log_step, s5_b_re, s5_b_im, s5_c_re, s5_c_im, s5_d, s5_w_glu,
              gla_w_in, gla_w_a2, gla_b_a2, gla_norm_g, gla_w_out,
              moe_w_router, moe_b_router, moe_w_up, moe_b_up, moe_w_down, moe_b_down):
    bn, l, d = x.shape
    lc = ctx.shape[1]
    rows = l // GRID_W
    cond = jax.nn.silu(c)
    cond_ctx = jax.nn.silu(c_ctx)[None]
    h, hc = x, ctx
    for i in range(DEPTH):
        keep_ctx = i < DEPTH - 1
        sh1, sc1, g1, sh2, sc2, g2 = [m[:, None, :] for m in jnp.split(cond @ w_ada[i] + b_ada[i], 6, axis=-1)]
        sh1c, sc1c, g1c, sh2c, sc2c, g2c = [m[:, None, :] for m in jnp.split(cond_ctx @ w_ada[i] + b_ada[i], 6, axis=-1)]
        u = _modulate(h, sh1, sc1)
        uc = _modulate(hc, sh1c, sc1c)
        j = i // 2
        if i % 2 == 0:
            y, yc = _s5_mixer(u, uc, s5_lam_re[j], s5_lam_im[j], s5_log_step[j], s5_b_re[j], s5_b_im[j],
                              s5_c_re[j], s5_c_im[j], s5_d[j], s5_w_glu[j], keep_ctx)
        else:
            y, yc = _gla_mixer(u, uc, rows, gla_w_in[j], gla_w_a2[j], gla_b_a2[j], gla_norm_g[j],
                               gla_w_out[j], keep_ctx)
        h = _layer_norm(DN_ALPHA * h + g1 * y, ln1_g[i], ln1_b[i])
        u2 = _modulate(h, sh2, sc2).reshape(bn * l, d)
        if keep_ctx:
            hc = _layer_norm(DN_ALPHA * hc + g1c * yc, ln1_g[i], ln1_b[i])
            u2c = _modulate(hc, sh2c, sc2c).reshape(bn * lc, d)
            tokens = jnp.concatenate([u2, u2c], axis=0)
        else:
            tokens = u2
        f = _moe(tokens, moe_w_router[i], moe_b_router[i], moe_w_up[i], moe_b_up[i], moe_w_down[i], moe_b_down[i])
        h = _layer_norm(DN_ALPHA * h + g2 * f[:bn * l].reshape(bn, l, d), ln2_g[i], ln2_b[i])
        if keep_ctx:
            hc = _layer_norm(DN_ALPHA * hc + g2c * f[bn * l:].reshape(bn, lc, d), ln2_g[i], ln2_b[i])
    return h
```

```python
import functools
import math

import jax
import jax.numpy as jnp
from jax import lax
from jax.experimental import pallas as pl
from jax.experimental.pallas import tpu as pltpu

F32 = jnp.float32
BF16 = jnp.bfloat16
I32 = jnp.int32
HIGHEST = lax.Precision.HIGHEST

GRID_W = 64
S5_GROUP = 16
S5_STATE = 64
GLA_HEADS = 4
GLA_GATE_RANK = 16
GLA_GATE_NORM = 16.0
GLA_CHUNK = 64
N_EXPERTS = 32
TOP_K = 4
SWIGLU_ALPHA = 1.702
SWIGLU_LIMIT = 7.0
LN_EPS = 1e-5

LANES = 128
SUBLANES = 8
MXU_K = 256
VMEM_LIMIT = 56 * 1024 * 1024

TOK_TILE = 256
S5_CHUNK = 128
S5_PITCH = S5_CHUNK + SUBLANES
MOE_ROWS = 256
ROUTE_LANES = LANES


def _sigmoid(x):
    return 1.0 / (1.0 + jnp.exp(-x))


def _cparams(sem):
    return pltpu.CompilerParams(dimension_semantics=sem, vmem_limit_bytes=VMEM_LIMIT)


def _ada_body(c_ref, w_ref, b_ref, o_ref):
    c = c_ref[...]
    cond = c * _sigmoid(c)
    o_ref[...] = jnp.dot(cond, w_ref[...], precision=HIGHEST, preferred_element_type=F32) + b_ref[...]


def _ada_table(cc, w_ada, b_ada):
    depth, d, six_d = w_ada.shape
    n_tiles = six_d // d
    return pl.pallas_call(
        _ada_body,
        grid=(depth, n_tiles),
        in_specs=[
            pl.BlockSpec((SUBLANES, d), lambda i, n: (0, 0)),
            pl.BlockSpec((None, d, d), lambda i, n: (i, 0, n)),
            pl.BlockSpec((None, 1, d), lambda i, n: (i, 0, n)),
        ],
        out_specs=pl.BlockSpec((None, SUBLANES, d), lambda i, n: (i, 0, n)),
        out_shape=jax.ShapeDtypeStruct((depth, SUBLANES, six_d), F32),
        compiler_params=_cparams(("arbitrary", "arbitrary")),
        name="ada_table",
    )(cc, w_ada, b_ada.reshape(depth, 1, six_d))


def _s5_scan_body(x_ref, mod_ref, wb_ref, wc_ref, are_ref, aim_ref, y_ref,
                  sre_ref, sim_ref, st_re, st_im, *, rev, tc, pitch, nb, n_kt, cpk):
    j = pl.program_id(0)
    ncol = n_kt * cpk
    half = cpk * LANES

    @pl.when(j == 0)
    def _():
        st_re[...] = jnp.zeros_like(st_re)
        st_im[...] = jnp.zeros_like(st_im)

    for b in range(nb):
        u = (x_ref[b] * (1.0 + mod_ref[b, 1:2, :]) + mod_ref[b, 0:1, :]).astype(BF16)
        for kt in range(n_kt):
            r = jnp.dot(u[:, MXU_K * kt:MXU_K * (kt + 1)], wb_ref[kt], preferred_element_type=F32)
            for c in range(cpk):
                sre_ref[kt * cpk + c, b * pitch:b * pitch + tc, :] = r[:, LANES * c:LANES * (c + 1)]
                sim_ref[kt * cpk + c, b * pitch:b * pitch + tc, :] = r[:, half + LANES * c:half + LANES * (c + 1)]

    grp = 8
    for cg in range(ncol // grp):
        cols = list(range(cg * grp, (cg + 1) * grp))
        ar = [are_ref[c] for c in cols]
        ai = [aim_ref[c] for c in cols]
        init = tuple(st_re[c] for c in cols) + tuple(st_im[c] for c in cols)

        def step(t, carry, cols=cols, ar=ar, ai=ai):
            tt = (tc - 1 - t) if rev else t
            out_re, out_im = [], []
            for k, c in enumerate(cols):
                rows = pl.ds(tt, nb, stride=pitch)
                pr, pi = carry[k], carry[grp + k]
                nr = ar[k] * pr - ai[k] * pi + sre_ref[c, rows, :]
                ni = ar[k] * pi + ai[k] * pr + sim_ref[c, rows, :]
                sre_ref[c, rows, :] = nr
                sim_ref[c, rows, :] = ni
                out_re.append(nr)
                out_im.append(ni)
            return tuple(out_re) + tuple(out_im)

        fin = lax.fori_loop(0, tc, step, init)
        for k, c in enumerate(cols):
            st_re[c] = fin[k]
            st_im[c] = fin[grp + k]

    for b in range(nb):
        for kt in range(n_kt):
            s_re = jnp.concatenate(
                [sre_ref[kt * cpk + c, b * pitch:b * pitch + tc, :] for c in range(cpk)], axis=-1).astype(BF16)
            s_im = jnp.concatenate(
                [sim_ref[kt * cpk + c, b * pitch:b * pitch + tc, :] for c in range(cpk)], axis=-1).astype(BF16)
            y = (jnp.dot(s_re, wc_ref[kt, :half, :], preferred_element_type=F32)
                 + jnp.dot(s_im, wc_ref[kt, half:, :], preferred_element_type=F32))
            y_ref[b, :, MXU_K * kt:MXU_K * (kt + 1)] = y


def _s5_scan(h3, modtab, wb, wc, a_re, a_im, *, rev, n_lat):
    nb, ltot, d = h3.shape
    tc = S5_CHUNK
    n_chunks = ltot // tc
    lat_chunks = n_lat // tc
    n_kt, _, two_half = wb.shape
    cpk = two_half // (2 * LANES)
    ncol = n_kt * cpk

    if rev:
        chunk = lambda j: n_chunks - 1 - j
    else:
        chunk = lambda j: (j + lat_chunks) % n_chunks
    is_ctx = lambda j: (chunk(j) >= lat_chunks).astype(I32)

    body = functools.partial(_s5_scan_body, rev=rev, tc=tc, pitch=S5_PITCH, nb=nb, n_kt=n_kt, cpk=cpk)
    return pl.pallas_call(
        body,
        grid=(n_chunks,),
        in_specs=[
            pl.BlockSpec((nb, tc, d), lambda j: (0, chunk(j), 0)),
            pl.BlockSpec((nb, None, 6, d), lambda j: (0, is_ctx(j), 0, 0)),
            pl.BlockSpec(wb.shape, lambda j: (0, 0, 0)),
            pl.BlockSpec(wc.shape, lambda j: (0, 0, 0)),
            pl.BlockSpec(a_re.shape, lambda j: (0, 0, 0)),
            pl.BlockSpec(a_im.shape, lambda j: (0, 0, 0)),
        ],
        out_specs=pl.BlockSpec((nb, tc, d), lambda j: (0, chunk(j), 0)),
        out_shape=jax.ShapeDtypeStruct((nb, ltot, d), F32),
        scratch_shapes=[
            pltpu.VMEM((ncol, nb * S5_PITCH, LANES), F32),
            pltpu.VMEM((ncol, nb * S5_PITCH, LANES), F32),
            pltpu.VMEM((ncol, nb, LANES), F32),
            pltpu.VMEM((ncol, nb, LANES), F32),
        ],
        compiler_params=_cparams(("arbitrary",)),
        name="s5_scan_bwd" if rev else "s5_scan_fwd",
    )(h3, modtab, wb, wc, a_re, a_im)


def _s5_prepare(lam_re, lam_im, log_step, b_re, b_im, c_re, c_im, nb):
    g, p = lam_re.shape
    hg = b_re.shape[-1]
    lr = lam_re.astype(F32)
    li = lam_im.astype(F32)
    dt = jnp.exp(log_step.astype(F32))[:, None]
    mag = jnp.exp(lr * dt)
    ar = mag * jnp.cos(li * dt)
    ai = mag * jnp.sin(li * dt)
    den = lr * lr + li * li
    nr = ar - 1.0
    kr = (nr * lr + ai * li) / den
    ki = (ai * lr - nr * li) / den
    br = b_re.astype(F32)
    bi = b_im.astype(F32)
    bbr = kr[..., None] * br - ki[..., None] * bi
    bbi = kr[..., None] * bi + ki[..., None] * br
    gpk = MXU_K // hg
    n_kt = g // gpk
    eye = jnp.eye(gpk, dtype=F32)

    def in_blocks(bb):
        t = bb.reshape(n_kt, gpk, p, hg)
        return jnp.einsum('kgph,gj->kghjp', t, eye).reshape(n_kt, gpk * hg, gpk * p)

    def out_blocks(cc):
        t = cc.reshape(n_kt, gpk, hg, p)
        return jnp.einsum('kghp,gj->kgpjh', t, eye).reshape(n_kt, gpk * p, gpk * hg)

    wb = jnp.concatenate([in_blocks(bbr), in_blocks(bbi)], axis=-1).astype(BF16)
    wc = jnp.concatenate([out_blocks(c_re.astype(F32)), -out_blocks(c_im.astype(F32))], axis=1).astype(BF16)
    ncol = g * p // LANES
    a_re = jnp.broadcast_to(ar.reshape(ncol, 1, LANES), (ncol, nb, LANES))
    a_im = jnp.broadcast_to(ai.reshape(ncol, 1, LANES), (ncol, nb, LANES))
    return wb, wc, a_re, a_im


def _gla_proj_body(x_ref, mod_ref, w_ref, wa_ref, wa2_ref, ba2_ref,
                   q_ref, k_ref, v_ref, g_ref, laf_ref, lab_ref, *, qk_w, v_w, dk):
    u = (x_ref[...] * (1.0 + mod_ref[1:2, :]) + mod_ref[0:1, :]).astype(BF16)
    p = jnp.dot(u, w_ref[...], preferred_element_type=F32)
    q_ref[...] = p[:, :qk_w] * (dk ** -0.5)
    k_ref[...] = p[:, qk_w:2 * qk_w]
    v_ref[...] = p[:, 2 * qk_w:2 * qk_w + v_w]
    g_ref[...] = p[:, 2 * qk_w + v_w:]
    for dr, out in enumerate((laf_ref, lab_ref)):
        a_d = jnp.dot(u, wa_ref[dr], preferred_element_type=F32).astype(BF16)
        z = jnp.dot(a_d, wa2_ref[dr], preferred_element_type=F32) + ba2_ref[dr]
        out[...] = (jnp.minimum(z, 0.0) - jnp.log1p(jnp.exp(-jnp.abs(z)))) / GLA_GATE_NORM


def _gla_proj(h2d, modtab, w_main, w_a, w_a2, b_a2, *, tiles_per_batch, ctx_tile0):
    t, d = h2d.shape
    tt = TOK_TILE
    qk_w = w_a2.shape[-1]
    v_w = (w_main.shape[1] - 2 * qk_w) // 2
    dk = qk_w // GLA_HEADS
    mod_map = lambda i: (i // tiles_per_batch, ((i % tiles_per_batch) >= ctx_tile0).astype(I32), 0, 0)
    row = lambda w: pl.BlockSpec((tt, w), lambda i: (i, 0))
    full = lambda a: pl.BlockSpec(a.shape, lambda i: (0,) * a.ndim)
    body = functools.partial(_gla_proj_body, qk_w=qk_w, v_w=v_w, dk=dk)
    b_a2r = b_a2.reshape(2, 1, qk_w).astype(F32)
    return pl.pallas_call(
        body,
        grid=(t // tt,),
        in_specs=[row(d), pl.BlockSpec((None, None, 6, d), mod_map),
                  full(w_main), full(w_a), full(w_a2), full(b_a2r)],
        out_specs=[row(qk_w), row(qk_w), row(v_w), row(v_w), row(qk_w), row(qk_w)],
        out_shape=[jax.ShapeDtypeStruct((t, w), F32) for w in (qk_w, qk_w, v_w, v_w, qk_w, qk_w)],
        compiler_params=_cparams(("parallel",)),
        name="gla_proj",
    )(h2d, modtab, w_main, w_a, w_a2, b_a2r)


def _gla_rec_body(q_ref, k_ref, v_ref, la_ref, o_ref, s_ref, *, rev, ch, dk, dv, nh):
    j = pl.program_id(1)

    @pl.when(j == 0)
    def _():
        s_ref[...] = jnp.zeros_like(s_ref)

    r_i = lax.broadcasted_iota(I32, (ch, ch), 0)
    c_i = lax.broadcasted_iota(I32, (ch, ch), 1)
    seen = (c_i >= r_i) if rev else (c_i <= r_i)
    seen_f = seen.astype(F32)
    end = 0 if rev else ch - 1
    for hd in range(nh):
        q = q_ref[:, hd * dk:(hd + 1) * dk]
        k = k_ref[:, hd * dk:(hd + 1) * dk]
        la = la_ref[:, hd * dk:(hd + 1) * dk]
        v = v_ref[:, hd * dv:(hd + 1) * dv].astype(BF16)
        b = jnp.dot(seen_f, la, precision=HIGHEST, preferred_element_type=F32)
        b_end = b[end:end + 1, :]
        q_d = (q * jnp.exp(b)).astype(BF16)
        k_d = (k * jnp.exp(-b)).astype(BF16)
        k_e = (k * jnp.exp(b_end - b)).astype(BF16)
        att = lax.dot_general(q_d, k_d, (((1,), (1,)), ((), ())), preferred_element_type=F32)
        att = jnp.where(seen, att, 0.0).astype(BF16)
        s_prev = s_ref[hd]
        o = (jnp.dot(att, v, preferred_element_type=F32)
             + jnp.dot(q_d, s_prev.astype(BF16), preferred_element_type=F32))
        o_ref[:, hd * dv:(hd + 1) * dv] = o
        upd = lax.dot_general(k_e, v, (((0,), (0,)), ((), ())), preferred_element_type=F32)
        g_col = jnp.transpose(jnp.broadcast_to(jnp.exp(b_end), (SUBLANES, dk)))[:, 0:1]
        s_ref[hd] = s_prev * g_col + upd


def _gla_rec(q, k, v, la, *, rev, n_lat):
    nb, ltot, qk_w = q.shape
    v_w = v.shape[-1]
    ch = GLA_CHUNK
    n_chunks = ltot // ch
    lat_chunks = n_lat // ch
    if rev:
        chunk = lambda j: n_chunks - 1 - j
    else:
        chunk = lambda j: (j + lat_chunks) % n_chunks
    spec = lambda w: pl.BlockSpec((None, ch, w), lambda b, j: (b, chunk(j), 0))
    dk, dv = qk_w // GLA_HEADS, v_w // GLA_HEADS
    body = functools.partial(_gla_rec_body, rev=rev, ch=ch, dk=dk, dv=dv, nh=GLA_HEADS)
    return pl.pallas_call(
        body,
        grid=(nb, n_chunks),
        in_specs=[spec(qk_w), spec(qk_w), spec(v_w), spec(qk_w)],
        out_specs=spec(v_w),
        out_shape=jax.ShapeDtypeStruct((nb, ltot, v_w), F32),
        scratch_shapes=[pltpu.VMEM((GLA_HEADS, dk, dv), F32)],
        compiler_params=_cparams(("parallel", "arbitrary")),
        name="gla_rec_bwd" if rev else "gla_rec_fwd",
    )(q, k, v, la)


def _layer_norm_rows(v, g, b):
    mu = jnp.mean(v, axis=-1, keepdims=True)
    c = v - mu
    var = jnp.mean(c * c, axis=-1, keepdims=True)
    return c * lax.rsqrt(var + LN_EPS) * g + b


def _post_mixer_body(*refs, kind, alpha, tt, dv):
    if kind == "s5":
        (h_ref, mod_ref, ya_ref, yb_ref, dsk_ref, w_ref, lng_ref, lnb_ref, wr_ref, br_ref,
         h1_ref, xr_ref, ti_ref, gt_ref, rk_ref, cnt_ref, base_ref) = refs
    else:
        (h_ref, mod_ref, ya_ref, yb_ref, gate_ref, ng_ref, w_ref, lng_ref, lnb_ref, wr_ref, br_ref,
         h1_ref, xr_ref, ti_ref, gt_ref, rk_ref, cnt_ref, base_ref) = refs
    i = pl.program_id(0)

    @pl.when(i == 0)
    def _():
        base_ref[...] = jnp.zeros_like(base_ref)

    h = h_ref[...]
    d = h.shape[-1]
    if kind == "s5":
        u = h * (1.0 + mod_ref[1:2, :]) + mod_ref[0:1, :]
        y = dsk_ref[...] * u + ya_ref[...] + yb_ref[...]
        ge = 0.5 * y * (1.0 + jnp.tanh(math.sqrt(2.0 / math.pi) * (y + 0.044715 * (y * y * y))))
        z = jnp.dot(ge.astype(BF16), w_ref[...], preferred_element_type=F32)
        mix = z[:, :d] * _sigmoid(z[:, d:])
    else:
        o = ya_ref[...] + yb_ref[...]
        parts = []
        for hd in range(d // dv):
            oh = o[:, hd * dv:(hd + 1) * dv]
            ms = jnp.mean(oh * oh, axis=-1, keepdims=True)
            parts.append(oh * lax.rsqrt(ms + LN_EPS))
        on = jnp.concatenate(parts, axis=-1) * ng_ref[...]
        gv = gate_ref[...]
        a = on * (gv * _sigmoid(gv))
        mix = jnp.dot(a.astype(BF16), w_ref[...], preferred_element_type=F32)

    h1 = _layer_norm_rows(alpha * h + mod_ref[2:3, :] * mix, lng_ref[...], lnb_ref[...])
    h1_ref[...] = h1
    u2 = h1 * (1.0 + mod_ref[4:5, :]) + mod_ref[3:4, :]
    for c in range(d // LANES):
        xr_ref[:, c, :] = u2[:, c * LANES:(c + 1) * LANES]

    logits = jnp.dot(u2, wr_ref[...], precision=HIGHEST, preferred_element_type=F32) + br_ref[...]
    lane = lax.broadcasted_iota(I32, logits.shape, 1)
    work = logits
    vals, idxs = [], []
    for _ in range(TOP_K):
        m = jnp.max(work, axis=-1, keepdims=True)
        idx = jnp.min(jnp.where(work == m, lane, ROUTE_LANES), axis=-1, keepdims=True)
        vals.append(m)
        idxs.append(idx)
        work = jnp.where(lane == idx, -jnp.inf, work)
    exps = [jnp.exp(v - vals[0]) for v in vals]
    den = exps[0]
    for e in exps[1:]:
        den = den + e

    multi = jnp.zeros(logits.shape, F32)
    for idx in idxs:
        multi = multi + (lane == idx).astype(F32)
    r_i = lax.broadcasted_iota(I32, (tt, tt), 0)
    c_i = lax.broadcasted_iota(I32, (tt, tt), 1)
    before = (c_i < r_i).astype(BF16)
    pos = jnp.dot(before, multi.astype(BF16), preferred_element_type=F32) + base_ref[0:1, :]
    ti = jnp.zeros(logits.shape, I32)
    gt = jnp.zeros(logits.shape, F32)
    rk = jnp.zeros(logits.shape, F32)
    for kk in range(TOP_K):
        ti = jnp.where(lane == kk, idxs[kk], ti)
        gt = jnp.where(lane == kk, exps[kk] / den, gt)
        rank_k = jnp.sum(jnp.where(lane == idxs[kk], pos, 0.0), axis=-1, keepdims=True)
        rk = jnp.where(lane == kk, rank_k, rk)
    ti_ref[...] = ti
    gt_ref[...] = gt
    rk_ref[...] = rk.astype(I32)
    new_base = base_ref[0:1, :] + jnp.sum(multi, axis=0, keepdims=True)
    base_ref[...] = jnp.broadcast_to(new_base, base_ref.shape)
    cnt_ref[...] = jnp.broadcast_to(new_base, cnt_ref.shape)


def _post_mixer(kind, h2d, modtab, ya, yb, extra, w_mix, ln_g, ln_b, w_router, b_router, *,
                alpha, tiles_per_batch, ctx_tile0):
    t, d = h2d.shape
    tt = TOK_TILE
    n_e = w_router.shape[-1]
    wr = jnp.zeros((d, ROUTE_LANES), F32).at[:, :n_e].set(w_router.astype(F32))
    br = jnp.full((1, ROUTE_LANES), jnp.finfo(F32).min, F32).at[0, :n_e].set(b_router.astype(F32))
    mod_map = lambda i: (i // tiles_per_batch, ((i % tiles_per_batch) >= ctx_tile0).astype(I32), 0, 0)
    row = lambda w: pl.BlockSpec((tt, w), lambda i: (i, 0))
    full = lambda a: pl.BlockSpec(a.shape, lambda i: (0,) * a.ndim)
    vec = lambda a: a.reshape(1, -1).astype(F32)
    if kind == "s5":
        (d_skip,) = extra
        ins = [h2d, modtab, ya, yb, vec(d_skip), w_mix, vec(ln_g), vec(ln_b), wr, br]
        in_specs = [row(d), pl.BlockSpec((None, None, 6, d), mod_map), row(d), row(d)]
        in_specs += [full(a) for a in ins[4:]]
        dv = d
    else:
        gate, norm_g = extra
        dv = norm_g.shape[-1]
        ng = jnp.tile(norm_g.astype(F32), d // dv).reshape(1, d)
        ins = [h2d, modtab, ya, yb, gate, ng, w_mix, vec(ln_g), vec(ln_b), wr, br]
        in_specs = [row(d), pl.BlockSpec((None, None, 6, d), mod_map), row(d), row(d), row(d)]
        in_specs += [full(a) for a in ins[5:]]
    body = functools.partial(_post_mixer_body, kind=kind, alpha=alpha, tt=tt, dv=dv)
    return pl.pallas_call(
        body,
        grid=(t // tt,),
        in_specs=in_specs,
        out_specs=[
            row(d),
            pl.BlockSpec((tt, d // LANES, LANES), lambda i: (i, 0, 0)),
            row(ROUTE_LANES), row(ROUTE_LANES), row(ROUTE_LANES),
            pl.BlockSpec((SUBLANES, ROUTE_LANES), lambda i: (0, 0)),
        ],
        out_shape=[
            jax.ShapeDtypeStruct((t, d), F32),
            jax.ShapeDtypeStruct((t, d // LANES, LANES), F32),
            jax.ShapeDtypeStruct((t, ROUTE_LANES), I32),
            jax.ShapeDtypeStruct((t, ROUTE_LANES), F32),
            jax.ShapeDtypeStruct((t, ROUTE_LANES), I32),
            jax.ShapeDtypeStruct((SUBLANES, ROUTE_LANES), F32),
        ],
        scratch_shapes=[pltpu.VMEM((SUBLANES, ROUTE_LANES), F32)],
        compiler_params=_cparams(("arbitrary",)),
        name="post_mixer_" + kind,
    )(*ins)


def _row_copy_wait(src_hbm, dst, sem, n):
    pltpu.make_async_copy(src_hbm.at[pl.ds(0, n)], dst.at[pl.ds(0, n)], sem).wait()


def _dispatch_body(ti_ref, rk_ref, ps_ref, xr_hbm, init_hbm, xp_hbm, sem, *, tt):
    del init_hbm
    i = pl.program_id(0)

    def issue(s, c):
        dst = ps_ref[ti_ref[s]] + rk_ref[s]
        tok = i * tt + s // TOP_K
        pltpu.make_async_copy(xr_hbm.at[tok], xp_hbm.at[dst], sem).start()
        return c

    lax.fori_loop(0, tt * TOP_K, issue, 0)
    _row_copy_wait(xr_hbm, xp_hbm, sem, tt * TOP_K)


def _dispatch(ti_flat, rk_flat, pstart, xr, n_rows):
    t = xr.shape[0]
    tt = TOK_TILE
    smem_blk = pl.BlockSpec((tt * TOP_K,), lambda i: (i,), memory_space=pltpu.SMEM)
    zeros = jnp.zeros((n_rows,) + xr.shape[1:], xr.dtype)
    return pl.pallas_call(
        functools.partial(_dispatch_body, tt=tt),
        grid=(t // tt,),
        in_specs=[smem_blk, smem_blk,
                  pl.BlockSpec(memory_space=pltpu.SMEM),
                  pl.BlockSpec(memory_space=pl.ANY),
                  pl.BlockSpec(memory_space=pl.ANY)],
        out_specs=pl.BlockSpec(memory_space=pl.ANY),
        out_shape=jax.ShapeDtypeStruct(zeros.shape, zeros.dtype),
        scratch_shapes=[pltpu.SemaphoreType.DMA],
        input_output_aliases={4: 0},
        compiler_params=_cparams(("arbitrary",)),
        name="moe_dispatch",
    )(ti_flat, rk_flat, pstart, xr, zeros)


def _expert_body(be_ref, nu_ref, x_ref, wu_ref, bu_ref, wd_ref, bd_ref, y_ref, wub, wdb):
    i = pl.program_id(0)
    e = be_ref[i]
    prev = be_ref[jnp.maximum(i - 1, 0)]

    @pl.when((i == 0) | (e != prev))
    def _():
        wub[...] = wu_ref[...].astype(BF16)
        wdb[...] = wd_ref[...].astype(BF16)

    nc = x_ref.shape[1]

    @pl.when(i < nu_ref[0])
    def _():
        x = jnp.concatenate([x_ref[:, c, :] for c in range(nc)], axis=-1).astype(BF16)
        h = jnp.dot(x, wub[...], preferred_element_type=F32) + bu_ref[...]
        ff = h.shape[-1] // 2
        h_glu = jnp.minimum(h[:, :ff], SWIGLU_LIMIT)
        h_lin = jnp.clip(h[:, ff:], -SWIGLU_LIMIT, SWIGLU_LIMIT)
        a = h_glu * _sigmoid(SWIGLU_ALPHA * h_glu) * (h_lin + 1.0)
        y = jnp.dot(a.astype(BF16), wdb[...], preferred_element_type=F32) + bd_ref[...]
        for c in range(nc):
            y_ref[:, c, :] = y[:, c * LANES:(c + 1) * LANES]

    @pl.when(i >= nu_ref[0])
    def _():
        y_ref[...] = jnp.zeros_like(y_ref)


def _experts(blk_e, n_used, xp, w_up, b_up, w_down, b_down):
    n_rows, nc, _ = xp.shape
    r = MOE_ROWS
    n_e, d, two_ff = w_up.shape
    ff = two_ff // 2
    grid_spec = pltpu.PrefetchScalarGridSpec(
        num_scalar_prefetch=2,
        grid=(n_rows // r,),
        in_specs=[
            pl.BlockSpec((r, nc, LANES), lambda i, be, nu: (i, 0, 0)),
            pl.BlockSpec((None, d, two_ff), lambda i, be, nu: (be[i], 0, 0)),
            pl.BlockSpec((None, 1, two_ff), lambda i, be, nu: (be[i], 0, 0)),
            pl.BlockSpec((None, ff, d), lambda i, be, nu: (be[i], 0, 0)),
            pl.BlockSpec((None, 1, d), lambda i, be, nu: (be[i], 0, 0)),
        ],
        out_specs=pl.BlockSpec((r, nc, LANES), lambda i, be, nu: (i, 0, 0)),
        scratch_shapes=[pltpu.VMEM((d, two_ff), BF16), pltpu.VMEM((ff, d), BF16)],
    )
    return pl.pallas_call(
        _expert_body,
        grid_spec=grid_spec,
        out_shape=jax.ShapeDtypeStruct(xp.shape, F32),
        compiler_params=_cparams(("arbitrary",)),
        name="moe_experts",
    )(blk_e, n_used, xp, w_up, b_up.reshape(n_e, 1, two_ff), w_down, b_down.reshape(n_e, 1, d))


def _combine_body(ti_ref, rk_ref, ps_ref, gt_ref, h_ref, mod_ref, lng_ref, lnb_ref, yp_hbm,
                  o_ref, buf, sem, *, tt, alpha):
    def issue(t, c):
        for kk in range(TOP_K):
            s = t * TOP_K + kk
            src = ps_ref[ti_ref[s]] + rk_ref[s]
            pltpu.make_async_copy(yp_hbm.at[src], buf.at[kk, t], sem).start()
        return c

    lax.fori_loop(0, tt, issue, 0)
    for kk in range(TOP_K):
        _row_copy_wait(yp_hbm, buf.at[kk], sem, tt)

    gt = gt_ref[...]
    nc = buf.shape[2]
    cols = []
    for c in range(nc):
        acc = gt[:, 0:1] * buf[0, :, c, :]
        for kk in range(1, TOP_K):
            acc = acc + gt[:, kk:kk + 1] * buf[kk, :, c, :]
        cols.append(acc)
    f = jnp.concatenate(cols, axis=-1)
    o_ref[...] = _layer_norm_rows(alpha * h_ref[...] + mod_ref[5:6, :] * f, lng_ref[...], lnb_ref[...])


def _combine(ti_flat, rk_flat, pstart, gates, h1, modtab, ln_g, ln_b, yp, *, alpha, tiles_per_batch, ctx_tile0):
    t, d = h1.shape
    tt = TOK_TILE
    nc = yp.shape[1]
    smem_blk = pl.BlockSpec((tt * TOP_K,), lambda i: (i,), memory_space=pltpu.SMEM)
    mod_map = lambda i: (i // tiles_per_batch, ((i % tiles_per_batch) >= ctx_tile0).astype(I32), 0, 0)
    vec = lambda a: a.reshape(1, -1).astype(F32)
    return pl.pallas_call(
        functools.partial(_combine_body, tt=tt, alpha=alpha),
        grid=(t // tt,),
        in_specs=[smem_blk, smem_blk,
                  pl.BlockSpec(memory_space=pltpu.SMEM),
                  pl.BlockSpec((tt, ROUTE_LANES), lambda i: (i, 0)),
                  pl.BlockSpec((tt, d), lambda i: (i, 0)),
                  pl.BlockSpec((None, None, 6, d), mod_map),
                  pl.BlockSpec((1, d), lambda i: (0, 0)),
                  pl.BlockSpec((1, d), lambda i: (0, 0)),
                  pl.BlockSpec(memory_space=pl.ANY)],
        out_specs=pl.BlockSpec((tt, d), lambda i: (i, 0)),
        out_shape=jax.ShapeDtypeStruct((t, d), F32),
        scratch_shapes=[pltpu.VMEM((TOP_K, tt, nc, LANES), F32), pltpu.SemaphoreType.DMA],
        compiler_params=_cparams(("arbitrary",)),
        name="moe_combine",
    )(ti_flat, rk_flat, pstart, gates, h1, modtab, vec(ln_g), vec(ln_b), yp)


def _moe_layer(h1, xr, ti, gt, rk, counts, modtab, ln_g, ln_b, w_up, b_up, w_down, b_down, *,
               alpha, tiles_per_batch, ctx_tile0):
    t = h1.shape[0]
    n_e = w_up.shape[0]
    r = MOE_ROWS
    n_blocks = -(-(t * TOP_K) // r) + n_e
    cnt = counts[0, :n_e].astype(I32)
    padded = (cnt + r - 1) // r * r
    pends = jnp.cumsum(padded)
    pstart = jnp.zeros((ROUTE_LANES,), I32).at[:n_e].set(pends - padded)
    blk_e = jnp.minimum(
        jnp.searchsorted(pends, jnp.arange(n_blocks, dtype=I32) * r, side='right'), n_e - 1).astype(I32)
    n_used = (pends[-1] // r).astype(I32).reshape(1)
    ti_flat = ti[:, :TOP_K].reshape(-1)
    rk_flat = rk[:, :TOP_K].reshape(-1)
    xp = _dispatch(ti_flat, rk_flat, pstart, xr, n_blocks * r)
    yp = _experts(blk_e, n_used, xp, w_up, b_up, w_down, b_down)
    return _combine(ti_flat, rk_flat, pstart, gt, h1, modtab, ln_g, ln_b, yp,
                    alpha=alpha, tiles_per_batch=tiles_per_batch, ctx_tile0=ctx_tile0)


def kernel(x, c, ctx, c_ctx, w_ada, b_ada, ln1_g, ln1_b, ln2_g, ln2_b, s5_lam_re, s5_lam_im, s5_log_step, s5_b_re, s5_b_im, s5_c_re, s5_c_im, s5_d, s5_w_glu, gla_w_in, gla_w_a2, gla_b_a2, gla_norm_g, gla_w_out, moe_w_router, moe_b_router, moe_w_up, moe_b_up, moe_w_down, moe_b_down):
    bn, l, d = x.shape
    lc = ctx.shape[1]
    depth = w_ada.shape[0]
    ltot = l + lc
    rows = l // GRID_W
    alpha = (2 * depth) ** 0.25
    assert l % TOK_TILE == 0 and lc % TOK_TILE == 0 and l % S5_CHUNK == 0 and lc % S5_CHUNK == 0
    assert bn < SUBLANES
    tiles_per_batch = ltot // TOK_TILE
    ctx_tile0 = l // TOK_TILE
    tile_kw = dict(tiles_per_batch=tiles_per_batch, ctx_tile0=ctx_tile0)

    cc = jnp.zeros((SUBLANES, d), F32).at[:bn].set(c.astype(F32)).at[bn].set(c_ctx.astype(F32))
    mod = _ada_table(cc, w_ada.astype(F32), b_ada.astype(F32))
    mod_lat = mod[:, :bn].reshape(depth, bn, 1, 6, d)
    mod_ctx = jnp.broadcast_to(mod[:, bn].reshape(depth, 1, 1, 6, d), (depth, bn, 1, 6, d))
    modtab = jnp.concatenate([mod_lat, mod_ctx], axis=2)

    def to_cm(h3):
        lat = h3[:, :l].reshape(bn, rows, GRID_W, d).transpose(0, 2, 1, 3).reshape(bn, l, d)
        return jnp.concatenate([lat, h3[:, l:]], axis=1)

    def to_rm(h3):
        lat = h3[:, :l].reshape(bn, GRID_W, rows, d).transpose(0, 2, 1, 3).reshape(bn, l, d)
        return jnp.concatenate([lat, h3[:, l:]], axis=1)

    h3 = jnp.concatenate([x.astype(F32), ctx.astype(F32)], axis=1)
    for i in range(depth):
        j = i // 2
        mt = modtab[i]
        if i % 2 == 0:
            ys = []
            for dr, rev in ((0, False), (1, True)):
                wb, wc, a_re, a_im = _s5_prepare(s5_lam_re[j, dr], s5_lam_im[j, dr], s5_log_step[j, dr],
                                                 s5_b_re[j, dr], s5_b_im[j, dr], s5_c_re[j, dr], s5_c_im[j, dr], bn)
                ys.append(_s5_scan(h3, mt, wb, wc, a_re, a_im, rev=rev, n_lat=l).reshape(bn * ltot, d))
            h2d = h3.reshape(bn * ltot, d)
            outs = _post_mixer("s5", h2d, mt, ys[0], ys[1], (s5_d[j],), s5_w_glu[j].astype(BF16),
                               ln1_g[i], ln1_b[i], moe_w_router[i], moe_b_router[i], alpha=alpha, **tile_kw)
        else:
            h3 = to_cm(h3)
            h2d = h3.reshape(bn * ltot, d)
            w_in = gla_w_in[j]
            n_main = w_in.shape[1] - 2 * GLA_GATE_RANK
            w_a = w_in[:, n_main:].reshape(d, 2, GLA_GATE_RANK).transpose(1, 0, 2).astype(BF16)
            q, k, v, g, la_f, la_b = _gla_proj(h2d, mt, w_in[:, :n_main].astype(BF16), w_a,
                                               gla_w_a2[j].astype(BF16), gla_b_a2[j], **tile_kw)
            r3 = lambda a: a.reshape(bn, ltot, a.shape[-1])
            o_f = _gla_rec(r3(q), r3(k), r3(v), r3(la_f), rev=False, n_lat=l).reshape(bn * ltot, d)
            o_b = _gla_rec(r3(q), r3(k), r3(v), r3(la_b), rev=True, n_lat=l).reshape(bn * ltot, d)
            outs = _post_mixer("gla", h2d, mt, o_f, o_b, (g, gla_norm_g[j]), gla_w_out[j].astype(BF16),
                               ln1_g[i], ln1_b[i], moe_w_router[i], moe_b_router[i], alpha=alpha, **tile_kw)
        h1, xr, ti, gt, rk, counts = outs
        h2 = _moe_layer(h1, xr, ti, gt, rk, counts, mt, ln2_g[i], ln2_b[i],
                        moe_w_up[i], moe_b_up[i], moe_w_down[i], moe_b_down[i], alpha=alpha, **tile_kw)
        h3 = h2.reshape(bn, ltot, d)
        if i % 2 == 1:
            h3 = to_rm(h3)
    return h3[:, :l].astype(x.dtype)
```

```python
import functools
import math

import jax
import jax.numpy as jnp
from jax import lax
from jax.experimental import pallas as pl
from jax.experimental.pallas import tpu as pltpu

F32 = jnp.float32
BF16 = jnp.bfloat16
I32 = jnp.int32
HIGHEST = lax.Precision.HIGHEST

GRID_W = 64
S5_GROUP = 16
S5_STATE = 64
GLA_HEADS = 4
GLA_GATE_RANK = 16
GLA_GATE_NORM = 16.0
GLA_CHUNK = 64
N_EXPERTS = 32
TOP_K = 4
SWIGLU_ALPHA = 1.702
SWIGLU_LIMIT = 7.0
LN_EPS = 1e-5

LANES = 128
SUBLANES = 8
MXU_K = 256
VMEM_LIMIT = 56 * 1024 * 1024

TOK_TILE = 256
S5_CHUNK = 128
S5_PITCH = S5_CHUNK + SUBLANES
MOE_ROWS = 512
ROUTE_LANES = LANES


def _sigmoid(x):
    return 1.0 / (1.0 + jnp.exp(-x))


def _cparams(sem):
    return pltpu.CompilerParams(dimension_semantics=sem, vmem_limit_bytes=VMEM_LIMIT)


def _ada_body(c_ref, w_ref, b_ref, o_ref):
    c = c_ref[...]
    cond = c * _sigmoid(c)
    o_ref[...] = jnp.dot(cond, w_ref[...], precision=HIGHEST, preferred_element_type=F32) + b_ref[...]


def _ada_table(cc, w_ada, b_ada):
    depth, d, six_d = w_ada.shape
    n_tiles = six_d // d
    return pl.pallas_call(
        _ada_body,
        grid=(depth, n_tiles),
        in_specs=[
            pl.BlockSpec((SUBLANES, d), lambda i, n: (0, 0)),
            pl.BlockSpec((None, d, d), lambda i, n: (i, 0, n)),
            pl.BlockSpec((None, 1, d), lambda i, n: (i, 0, n)),
        ],
        out_specs=pl.BlockSpec((None, SUBLANES, d), lambda i, n: (i, 0, n)),
        out_shape=jax.ShapeDtypeStruct((depth, SUBLANES, six_d), F32),
        compiler_params=_cparams(("arbitrary", "arbitrary")),
        name="ada_table",
    )(cc, w_ada, b_ada.reshape(depth, 1, six_d))


def _s5_scan_body(x_ref, mod_ref, wb_ref, wc_ref, are_ref, aim_ref, y_ref,
                  sre_ref, sim_ref, st_re, st_im, *, rev, tc, pitch, nb, n_kt, cpk):
    j = pl.program_id(0)
    ncol = n_kt * cpk
    half = cpk * LANES

    @pl.when(j == 0)
    def _():
        st_re[...] = jnp.zeros_like(st_re)
        st_im[...] = jnp.zeros_like(st_im)

    u = jnp.concatenate(
        [(x_ref[b] * (1.0 + mod_ref[b, 1:2, :]) + mod_ref[b, 0:1, :]).astype(BF16) for b in range(nb)], axis=0)
    for kt in range(n_kt):
        r = jnp.dot(u[:, MXU_K * kt:MXU_K * (kt + 1)], wb_ref[kt], preferred_element_type=F32)
        for b in range(nb):
            rb = r[b * tc:(b + 1) * tc]
            for c in range(cpk):
                sre_ref[kt * cpk + c, b * pitch:b * pitch + tc, :] = rb[:, LANES * c:LANES * (c + 1)]
                sim_ref[kt * cpk + c, b * pitch:b * pitch + tc, :] = rb[:, half + LANES * c:half + LANES * (c + 1)]

    grp = 8
    for cg in range(ncol // grp):
        cols = list(range(cg * grp, (cg + 1) * grp))
        ar = [are_ref[c] for c in cols]
        ai = [aim_ref[c] for c in cols]
        init = tuple(st_re[c] for c in cols) + tuple(st_im[c] for c in cols)

        def step(t, carry, cols=cols, ar=ar, ai=ai):
            tt = (tc - 1 - t) if rev else t
            out_re, out_im = [], []
            for k, c in enumerate(cols):
                rows = pl.ds(tt, nb, stride=pitch)
                pr, pi = carry[k], carry[grp + k]
                nr = ar[k] * pr - ai[k] * pi + sre_ref[c, rows, :]
                ni = ar[k] * pi + ai[k] * pr + sim_ref[c, rows, :]
                sre_ref[c, rows, :] = nr
                sim_ref[c, rows, :] = ni
                out_re.append(nr)
                out_im.append(ni)
            return tuple(out_re) + tuple(out_im)

        fin = lax.fori_loop(0, tc, step, init)
        for k, c in enumerate(cols):
            st_re[c] = fin[k]
            st_im[c] = fin[grp + k]

    def stacked(ref, kt):
        return jnp.concatenate(
            [jnp.concatenate([ref[kt * cpk + c, b * pitch:b * pitch + tc, :] for c in range(cpk)], axis=-1)
             for b in range(nb)], axis=0).astype(BF16)

    for kt in range(n_kt):
        y = (jnp.dot(stacked(sre_ref, kt), wc_ref[kt, :half, :], preferred_element_type=F32)
             + jnp.dot(stacked(sim_ref, kt), wc_ref[kt, half:, :], preferred_element_type=F32))
        for b in range(nb):
            y_ref[b, :, MXU_K * kt:MXU_K * (kt + 1)] = y[b * tc:(b + 1) * tc]


def _s5_scan(h3, modtab, wb, wc, a_re, a_im, *, rev, n_lat):
    nb, ltot, d = h3.shape
    tc = S5_CHUNK
    n_chunks = ltot // tc
    lat_chunks = n_lat // tc
    n_kt, _, two_half = wb.shape
    cpk = two_half // (2 * LANES)
    ncol = n_kt * cpk

    if rev:
        chunk = lambda j: n_chunks - 1 - j
    else:
        chunk = lambda j: (j + lat_chunks) % n_chunks
    is_ctx = lambda j: (chunk(j) >= lat_chunks).astype(I32)

    body = functools.partial(_s5_scan_body, rev=rev, tc=tc, pitch=S5_PITCH, nb=nb, n_kt=n_kt, cpk=cpk)
    return pl.pallas_call(
        body,
        grid=(n_chunks,),
        in_specs=[
            pl.BlockSpec((nb, tc, d), lambda j: (0, chunk(j), 0)),
            pl.BlockSpec((nb, None, 6, d), lambda j: (0, is_ctx(j), 0, 0)),
            pl.BlockSpec(wb.shape, lambda j: (0, 0, 0)),
            pl.BlockSpec(wc.shape, lambda j: (0, 0, 0)),
            pl.BlockSpec(a_re.shape, lambda j: (0, 0, 0)),
            pl.BlockSpec(a_im.shape, lambda j: (0, 0, 0)),
        ],
        out_specs=pl.BlockSpec((nb, tc, d), lambda j: (0, chunk(j), 0)),
        out_shape=jax.ShapeDtypeStruct((nb, ltot, d), F32),
        scratch_shapes=[
            pltpu.VMEM((ncol, nb * S5_PITCH, LANES), F32),
            pltpu.VMEM((ncol, nb * S5_PITCH, LANES), F32),
            pltpu.VMEM((ncol, nb, LANES), F32),
            pltpu.VMEM((ncol, nb, LANES), F32),
        ],
        compiler_params=_cparams(("arbitrary",)),
        name="s5_scan_bwd" if rev else "s5_scan_fwd",
    )(h3, modtab, wb, wc, a_re, a_im)


def _s5_prepare(lam_re, lam_im, log_step, b_re, b_im, c_re, c_im, nb):
    g, p = lam_re.shape
    hg = b_re.shape[-1]
    lr = lam_re.astype(F32)
    li = lam_im.astype(F32)
    dt = jnp.exp(log_step.astype(F32))[:, None]
    mag = jnp.exp(lr * dt)
    ar = mag * jnp.cos(li * dt)
    ai = mag * jnp.sin(li * dt)
    den = lr * lr + li * li
    nr = ar - 1.0
    kr = (nr * lr + ai * li) / den
    ki = (ai * lr - nr * li) / den
    br = b_re.astype(F32)
    bi = b_im.astype(F32)
    bbr = kr[..., None] * br - ki[..., None] * bi
    bbi = kr[..., None] * bi + ki[..., None] * br
    gpk = MXU_K // hg
    n_kt = g // gpk
    eye = jnp.eye(gpk, dtype=F32)

    def block_diag(t):
        full = t[:, :, :, None, :] * eye[None, :, None, :, None]
        return full.reshape(n_kt, gpk * t.shape[2], gpk * t.shape[3])

    def in_blocks(bb):
        return block_diag(bb.reshape(n_kt, gpk, p, hg).transpose(0, 1, 3, 2))

    def out_blocks(cc):
        return block_diag(cc.reshape(n_kt, gpk, hg, p).transpose(0, 1, 3, 2))

    wb = jnp.concatenate([in_blocks(bbr), in_blocks(bbi)], axis=-1).astype(BF16)
    wc = jnp.concatenate([out_blocks(c_re.astype(F32)), -out_blocks(c_im.astype(F32))], axis=1).astype(BF16)
    ncol = g * p // LANES
    a_re = jnp.broadcast_to(ar.reshape(ncol, 1, LANES), (ncol, nb, LANES))
    a_im = jnp.broadcast_to(ai.reshape(ncol, 1, LANES), (ncol, nb, LANES))
    return wb, wc, a_re, a_im


def _gla_proj_body(x_ref, mod_ref, w_ref, wa_ref, wa2_ref, ba2_ref,
                   q_ref, k_ref, v_ref, g_ref, laf_ref, lab_ref, *, qk_w, v_w, dk):
    u = (x_ref[...] * (1.0 + mod_ref[1:2, :]) + mod_ref[0:1, :]).astype(BF16)
    p = jnp.dot(u, w_ref[...], preferred_element_type=F32)
    q_ref[...] = p[:, :qk_w] * (dk ** -0.5)
    k_ref[...] = p[:, qk_w:2 * qk_w]
    v_ref[...] = p[:, 2 * qk_w:2 * qk_w + v_w]
    g_ref[...] = p[:, 2 * qk_w + v_w:]
    for dr, out in enumerate((laf_ref, lab_ref)):
        a_d = jnp.dot(u, wa_ref[dr], preferred_element_type=F32).astype(BF16)
        z = jnp.dot(a_d, wa2_ref[dr], preferred_element_type=F32) + ba2_ref[dr]
        out[...] = (jnp.minimum(z, 0.0) - jnp.log1p(jnp.exp(-jnp.abs(z)))) / GLA_GATE_NORM


def _gla_proj(h2d, modtab, w_main, w_a, w_a2, b_a2, *, tiles_per_batch, ctx_tile0):
    t, d = h2d.shape
    tt = TOK_TILE
    qk_w = w_a2.shape[-1]
    v_w = (w_main.shape[1] - 2 * qk_w) // 2
    dk = qk_w // GLA_HEADS
    mod_map = lambda i: (i // tiles_per_batch, ((i % tiles_per_batch) >= ctx_tile0).astype(I32), 0, 0)
    row = lambda w: pl.BlockSpec((tt, w), lambda i: (i, 0))
    full = lambda a: pl.BlockSpec(a.shape, lambda i: (0,) * a.ndim)
    body = functools.partial(_gla_proj_body, qk_w=qk_w, v_w=v_w, dk=dk)
    b_a2r = b_a2.reshape(2, 1, qk_w).astype(F32)
    return pl.pallas_call(
        body,
        grid=(t // tt,),
        in_specs=[row(d), pl.BlockSpec((None, None, 6, d), mod_map),
                  full(w_main), full(w_a), full(w_a2), full(b_a2r)],
        out_specs=[row(qk_w), row(qk_w), row(v_w), row(v_w), row(qk_w), row(qk_w)],
        out_shape=[jax.ShapeDtypeStruct((t, w), F32) for w in (qk_w, qk_w, v_w, v_w, qk_w, qk_w)],
        compiler_params=_cparams(("parallel",)),
        name="gla_proj",
    )(h2d, modtab, w_main, w_a, w_a2, b_a2r)


def _gla_rec_body(q_ref, k_ref, v_ref, la_ref, o_ref, s_ref, *, rev, ch, dk, dv, nh, nb):
    j = pl.program_id(0)

    @pl.when(j == 0)
    def _():
        s_ref[...] = jnp.zeros_like(s_ref)

    r_i = lax.broadcasted_iota(I32, (ch, ch), 0)
    c_i = lax.broadcasted_iota(I32, (ch, ch), 1)
    seen = (c_i >= r_i) if rev else (c_i <= r_i)
    seen_f = seen.astype(F32)
    end = 0 if rev else ch - 1
    for bi in range(nb):
        b = jnp.dot(seen_f, la_ref[bi], precision=HIGHEST, preferred_element_type=F32)
        b_end = b[end:end + 1, :]
        k_all = k_ref[bi]
        q_d_all = (q_ref[bi] * jnp.exp(b)).astype(BF16)
        k_d_all = (k_all * jnp.exp(-b)).astype(BF16)
        k_e_all = (k_all * jnp.exp(b_end - b)).astype(BF16)
        g_cols = jnp.transpose(jnp.broadcast_to(jnp.exp(b_end), (SUBLANES, nh * dk)))[:, 0:1]
        for hd in range(nh):
            qs = slice(hd * dk, (hd + 1) * dk)
            vs = slice(hd * dv, (hd + 1) * dv)
            q_d, k_d, k_e = q_d_all[:, qs], k_d_all[:, qs], k_e_all[:, qs]
            v = v_ref[bi, :, vs].astype(BF16)
            att = lax.dot_general(q_d, k_d, (((1,), (1,)), ((), ())), preferred_element_type=F32)
            att = jnp.where(seen, att, 0.0).astype(BF16)
            s_prev = s_ref[bi * nh + hd]
            o = (jnp.dot(att, v, preferred_element_type=F32)
                 + jnp.dot(q_d, s_prev.astype(BF16), preferred_element_type=F32))
            o_ref[bi, :, vs] = o
            upd = lax.dot_general(k_e, v, (((0,), (0,)), ((), ())), preferred_element_type=F32)
            s_ref[bi * nh + hd] = s_prev * g_cols[qs, :] + upd


def _gla_rec(q, k, v, la, *, rev, n_lat):
    nb, ltot, qk_w = q.shape
    v_w = v.shape[-1]
    ch = GLA_CHUNK
    n_chunks = ltot // ch
    lat_chunks = n_lat // ch
    if rev:
        chunk = lambda j: n_chunks - 1 - j
    else:
        chunk = lambda j: (j + lat_chunks) % n_chunks
    spec = lambda w: pl.BlockSpec((nb, ch, w), lambda j: (0, chunk(j), 0))
    dk, dv = qk_w // GLA_HEADS, v_w // GLA_HEADS
    body = functools.partial(_gla_rec_body, rev=rev, ch=ch, dk=dk, dv=dv, nh=GLA_HEADS, nb=nb)
    return pl.pallas_call(
        body,
        grid=(n_chunks,),
        in_specs=[spec(qk_w), spec(qk_w), spec(v_w), spec(qk_w)],
        out_specs=spec(v_w),
        out_shape=jax.ShapeDtypeStruct((nb, ltot, v_w), F32),
        scratch_shapes=[pltpu.VMEM((nb * GLA_HEADS, dk, dv), F32)],
        compiler_params=_cparams(("arbitrary",)),
        name="gla_rec_bwd" if rev else "gla_rec_fwd",
    )(q, k, v, la)


def _layer_norm_rows(v, g, b):
    mu = jnp.mean(v, axis=-1, keepdims=True)
    c = v - mu
    var = jnp.mean(c * c, axis=-1, keepdims=True)
    return c * lax.rsqrt(var + LN_EPS) * g + b


def _post_mixer_body(*refs, kind, alpha, tt, dv):
    if kind == "s5":
        (h_ref, mod_ref, ya_ref, yb_ref, dsk_ref, w_ref, lng_ref, lnb_ref, wr_ref, br_ref,
         h1_ref, xr_ref, ti_ref, gt_ref, rk_ref, cnt_ref, base_ref) = refs
    else:
        (h_ref, mod_ref, ya_ref, yb_ref, gate_ref, ng_ref, w_ref, lng_ref, lnb_ref, wr_ref, br_ref,
         h1_ref, xr_ref, ti_ref, gt_ref, rk_ref, cnt_ref, base_ref) = refs
    i = pl.program_id(0)

    @pl.when(i == 0)
    def _():
        base_ref[...] = jnp.zeros_like(base_ref)

    h = h_ref[...]
    d = h.shape[-1]
    if kind == "s5":
        u = h * (1.0 + mod_ref[1:2, :]) + mod_ref[0:1, :]
        y = dsk_ref[...] * u + ya_ref[...] + yb_ref[...]
        ge = 0.5 * y * (1.0 + jnp.tanh(math.sqrt(2.0 / math.pi) * (y + 0.044715 * (y * y * y))))
        z = jnp.dot(ge.astype(BF16), w_ref[...], preferred_element_type=F32)
        mix = z[:, :d] * _sigmoid(z[:, d:])
    else:
        o = ya_ref[...] + yb_ref[...]
        parts = []
        for hd in range(d // dv):
            oh = o[:, hd * dv:(hd + 1) * dv]
            ms = jnp.mean(oh * oh, axis=-1, keepdims=True)
            parts.append(oh * lax.rsqrt(ms + LN_EPS))
        on = jnp.concatenate(parts, axis=-1) * ng_ref[...]
        gv = gate_ref[...]
        a = on * (gv * _sigmoid(gv))
        mix = jnp.dot(a.astype(BF16), w_ref[...], preferred_element_type=F32)

    h1 = _layer_norm_rows(alpha * h + mod_ref[2:3, :] * mix, lng_ref[...], lnb_ref[...])
    h1_ref[...] = h1
    u2 = h1 * (1.0 + mod_ref[4:5, :]) + mod_ref[3:4, :]
    for c in range(d // LANES):
        xr_ref[:, c, :] = u2[:, c * LANES:(c + 1) * LANES]

    logits = jnp.dot(u2, wr_ref[...], precision=HIGHEST, preferred_element_type=F32) + br_ref[...]
    lane = lax.broadcasted_iota(I32, logits.shape, 1)
    work = logits
    vals, idxs = [], []
    for _ in range(TOP_K):
        m = jnp.max(work, axis=-1, keepdims=True)
        idx = jnp.min(jnp.where(work == m, lane, ROUTE_LANES), axis=-1, keepdims=True)
        vals.append(m)
        idxs.append(idx)
        work = jnp.where(lane == idx, -jnp.inf, work)
    exps = [jnp.exp(v - vals[0]) for v in vals]
    den = exps[0]
    for e in exps[1:]:
        den = den + e

    multi = jnp.zeros(logits.shape, F32)
    for idx in idxs:
        multi = multi + (lane == idx).astype(F32)
    r_i = lax.broadcasted_iota(I32, (tt, tt), 0)
    c_i = lax.broadcasted_iota(I32, (tt, tt), 1)
    before = (c_i < r_i).astype(BF16)
    pos = jnp.dot(before, multi.astype(BF16), preferred_element_type=F32) + base_ref[0:1, :]
    ti = jnp.zeros(logits.shape, I32)
    gt = jnp.zeros(logits.shape, F32)
    rk = jnp.zeros(logits.shape, F32)
    for kk in range(TOP_K):
        ti = jnp.where(lane == kk, idxs[kk], ti)
        gt = jnp.where(lane == kk, exps[kk] / den, gt)
        rank_k = jnp.sum(jnp.where(lane == idxs[kk], pos, 0.0), axis=-1, keepdims=True)
        rk = jnp.where(lane == kk, rank_k, rk)
    ti_ref[...] = ti
    gt_ref[...] = gt
    rk_ref[...] = rk.astype(I32)
    new_base = base_ref[0:1, :] + jnp.sum(multi, axis=0, keepdims=True)
    base_ref[...] = jnp.broadcast_to(new_base, base_ref.shape)
    cnt_ref[...] = jnp.broadcast_to(new_base, cnt_ref.shape)


def _post_mixer(kind, h2d, modtab, ya, yb, extra, w_mix, ln_g, ln_b, w_router, b_router, *,
                alpha, tiles_per_batch, ctx_tile0):
    t, d = h2d.shape
    tt = TOK_TILE
    n_e = w_router.shape[-1]
    wr = jnp.zeros((d, ROUTE_LANES), F32).at[:, :n_e].set(w_router.astype(F32))
    br = jnp.full((1, ROUTE_LANES), jnp.finfo(F32).min, F32).at[0, :n_e].set(b_router.astype(F32))
    mod_map = lambda i: (i // tiles_per_batch, ((i % tiles_per_batch) >= ctx_tile0).astype(I32), 0, 0)
    row = lambda w: pl.BlockSpec((tt, w), lambda i: (i, 0))
    full = lambda a: pl.BlockSpec(a.shape, lambda i: (0,) * a.ndim)
    vec = lambda a: a.reshape(1, -1).astype(F32)
    if kind == "s5":
        (d_skip,) = extra
        ins = [h2d, modtab, ya, yb, vec(d_skip), w_mix, vec(ln_g), vec(ln_b), wr, br]
        in_specs = [row(d), pl.BlockSpec((None, None, 6, d), mod_map), row(d), row(d)]
        in_specs += [full(a) for a in ins[4:]]
        dv = d
    else:
        gate, norm_g = extra
        dv = norm_g.shape[-1]
        ng = jnp.tile(norm_g.astype(F32), d // dv).reshape(1, d)
        ins = [h2d, modtab, ya, yb, gate, ng, w_mix, vec(ln_g), vec(ln_b), wr, br]
        in_specs = [row(d), pl.BlockSpec((None, None, 6, d), mod_map), row(d), row(d), row(d)]
        in_specs += [full(a) for a in ins[5:]]
    body = functools.partial(_post_mixer_body, kind=kind, alpha=alpha, tt=tt, dv=dv)
    return pl.pallas_call(
        body,
        grid=(t // tt,),
        in_specs=in_specs,
        out_specs=[
            row(d),
            pl.BlockSpec((tt, d // LANES, LANES), lambda i: (i, 0, 0)),
            row(ROUTE_LANES), row(ROUTE_LANES), row(ROUTE_LANES),
            pl.BlockSpec((SUBLANES, ROUTE_LANES), lambda i: (0, 0)),
        ],
        out_shape=[
            jax.ShapeDtypeStruct((t, d), F32),
            jax.ShapeDtypeStruct((t, d // LANES, LANES), F32),
            jax.ShapeDtypeStruct((t, ROUTE_LANES), I32),
            jax.ShapeDtypeStruct((t, ROUTE_LANES), F32),
            jax.ShapeDtypeStruct((t, ROUTE_LANES), I32),
            jax.ShapeDtypeStruct((SUBLANES, ROUTE_LANES), F32),
        ],
        scratch_shapes=[pltpu.VMEM((SUBLANES, ROUTE_LANES), F32)],
        compiler_params=_cparams(("arbitrary",)),
        name="post_mixer_" + kind,
    )(*ins)


def _row_copy_wait(src_hbm, dst, sem, n):
    pltpu.make_async_copy(src_hbm.at[pl.ds(0, n)], dst.at[pl.ds(0, n)], sem).wait()


def _dispatch_body(ti_ref, rk_ref, ps_ref, zs_ref, x_ref, xp_hbm, zbuf, sem, zsem, *, tt, n_e):
    r = zbuf.shape[0]

    @pl.when(pl.program_id(0) == 0)
    def _():
        zbuf[...] = jnp.zeros_like(zbuf)
        for e in range(2 * n_e):
            @pl.when(zs_ref[e] >= 0)
            def _():
                pltpu.make_async_copy(zbuf, xp_hbm.at[pl.ds(zs_ref[e], r)], zsem).start()
        for e in range(2 * n_e):
            @pl.when(zs_ref[e] >= 0)
            def _():
                pltpu.make_async_copy(zbuf, xp_hbm.at[pl.ds(zs_ref[e], r)], zsem).wait()

    def issue(t, c):
        for kk in range(TOP_K):
            s = t * TOP_K + kk
            dst = ps_ref[ti_ref[s]] + rk_ref[s]
            pltpu.make_async_copy(x_ref.at[t], xp_hbm.at[dst], sem).start()
        return c

    lax.fori_loop(0, tt, issue, 0, unroll=4)
    _row_copy_wait(xp_hbm, xp_hbm, sem, tt * TOP_K)


def _dispatch(ti_flat, rk_flat, pstart, zstart, xr, n_rows):
    t = xr.shape[0]
    tt = TOK_TILE
    smem_blk = pl.BlockSpec((tt * TOP_K,), lambda i: (i,), memory_space=pltpu.SMEM)
    return pl.pallas_call(
        functools.partial(_dispatch_body, tt=tt, n_e=N_EXPERTS),
        grid=(t // tt,),
        in_specs=[smem_blk, smem_blk,
                  pl.BlockSpec(memory_space=pltpu.SMEM),
                  pl.BlockSpec(memory_space=pltpu.SMEM),
                  pl.BlockSpec((tt,) + xr.shape[1:], lambda i: (i, 0, 0))],
        out_specs=pl.BlockSpec(memory_space=pl.ANY),
        out_shape=jax.ShapeDtypeStruct((n_rows,) + xr.shape[1:], xr.dtype),
        scratch_shapes=[pltpu.VMEM((MOE_ROWS,) + xr.shape[1:], xr.dtype),
                        pltpu.SemaphoreType.DMA, pltpu.SemaphoreType.DMA],
        compiler_params=_cparams(("arbitrary",)),
        name="moe_dispatch",
    )(ti_flat, rk_flat, pstart, zstart, xr)


def _expert_body(be_ref, nu_ref, x_ref, wu_ref, bu_ref, wd_ref, bd_ref, y_ref, wub, wdb):
    i = pl.program_id(0)
    e = be_ref[i]
    prev = be_ref[jnp.maximum(i - 1, 0)]

    @pl.when((i == 0) | (e != prev))
    def _():
        wub[...] = wu_ref[...].astype(BF16)
        wdb[...] = wd_ref[...].astype(BF16)

    nc = wu_ref.shape[0] // LANES
    r = x_ref.shape[0] // nc

    @pl.when(i < nu_ref[0])
    def _():
        x = jnp.concatenate([x_ref[pl.ds(c, r, stride=nc), :] for c in range(nc)], axis=-1).astype(BF16)
        h = jnp.dot(x, wub[...], preferred_element_type=F32) + bu_ref[...]
        ff = h.shape[-1] // 2
        h_glu = jnp.minimum(h[:, :ff], SWIGLU_LIMIT)
        h_lin = jnp.clip(h[:, ff:], -SWIGLU_LIMIT, SWIGLU_LIMIT)
        a = h_glu * _sigmoid(SWIGLU_ALPHA * h_glu) * (h_lin + 1.0)
        y = jnp.dot(a.astype(BF16), wdb[...], preferred_element_type=F32) + bd_ref[...]
        for c in range(nc):
            y_ref[pl.ds(c, r, stride=nc), :] = y[:, c * LANES:(c + 1) * LANES]

    @pl.when(i >= nu_ref[0])
    def _():
        y_ref[...] = jnp.zeros_like(y_ref)


def _experts(blk_e, n_used, xp, w_up, b_up, w_down, b_down):
    n_rows, nc, _ = xp.shape
    r = MOE_ROWS
    n_e, d, two_ff = w_up.shape
    ff = two_ff // 2
    grid_spec = pltpu.PrefetchScalarGridSpec(
        num_scalar_prefetch=2,
        grid=(n_rows // r,),
        in_specs=[
            pl.BlockSpec((r * nc, LANES), lambda i, be, nu: (jnp.minimum(i, nu[0] - 1), 0)),
            pl.BlockSpec((None, d, two_ff), lambda i, be, nu: (be[i], 0, 0)),
            pl.BlockSpec((None, 1, two_ff), lambda i, be, nu: (be[i], 0, 0)),
            pl.BlockSpec((None, ff, d), lambda i, be, nu: (be[i], 0, 0)),
            pl.BlockSpec((None, 1, d), lambda i, be, nu: (be[i], 0, 0)),
        ],
        out_specs=pl.BlockSpec((r * nc, LANES), lambda i, be, nu: (i, 0)),
        scratch_shapes=[pltpu.VMEM((d, two_ff), BF16), pltpu.VMEM((ff, d), BF16)],
    )
    yp = pl.pallas_call(
        _expert_body,
        grid_spec=grid_spec,
        out_shape=jax.ShapeDtypeStruct((n_rows * nc, LANES), F32),
        compiler_params=_cparams(("arbitrary",)),
        name="moe_experts",
    )(blk_e, n_used, xp.reshape(n_rows * nc, LANES), w_up, b_up.reshape(n_e, 1, two_ff),
      w_down, b_down.reshape(n_e, 1, d))
    return yp.reshape(n_rows, nc, LANES)


def _combine_body(ti_ref, rk_ref, ps_ref, gt_ref, h_ref, mod_ref, lng_ref, lnb_ref, yp_hbm,
                  o_ref, buf, sem, *, tt, alpha):
    def issue(t, c):
        for kk in range(TOP_K):
            s = t * TOP_K + kk
            src = ps_ref[ti_ref[s]] + rk_ref[s]
            pltpu.make_async_copy(yp_hbm.at[src], buf.at[kk, t], sem).start()
        return c

    lax.fori_loop(0, tt, issue, 0, unroll=4)
    for kk in range(TOP_K):
        _row_copy_wait(yp_hbm, buf.at[kk], sem, tt)

    gt = gt_ref[...]
    nc = buf.shape[2]
    cols = []
    for c in range(nc):
        acc = gt[:, 0:1] * buf[0, :, c, :]
        for kk in range(1, TOP_K):
            acc = acc + gt[:, kk:kk + 1] * buf[kk, :, c, :]
        cols.append(acc)
    f = jnp.concatenate(cols, axis=-1)
    o_ref[...] = _layer_norm_rows(alpha * h_ref[...] + mod_ref[5:6, :] * f, lng_ref[...], lnb_ref[...])


def _combine(ti_flat, rk_flat, pstart, gates, h1, modtab, ln_g, ln_b, yp, *, alpha, tiles_per_batch, ctx_tile0):
    t, d = h1.shape
    tt = TOK_TILE
    nc = yp.shape[1]
    smem_blk = pl.BlockSpec((tt * TOP_K,), lambda i: (i,), memory_space=pltpu.SMEM)
    mod_map = lambda i: (i // tiles_per_batch, ((i % tiles_per_batch) >= ctx_tile0).astype(I32), 0, 0)
    vec = lambda a: a.reshape(1, -1).astype(F32)
    return pl.pallas_call(
        functools.partial(_combine_body, tt=tt, alpha=alpha),
        grid=(t // tt,),
        in_specs=[smem_blk, smem_blk,
                  pl.BlockSpec(memory_space=pltpu.SMEM),
                  pl.BlockSpec((tt, ROUTE_LANES), lambda i: (i, 0)),
                  pl.BlockSpec((tt, d), lambda i: (i, 0)),
                  pl.BlockSpec((None, None, 6, d), mod_map),
                  pl.BlockSpec((1, d), lambda i: (0, 0)),
                  pl.BlockSpec((1, d), lambda i: (0, 0)),
                  pl.BlockSpec(memory_space=pl.ANY)],
        out_specs=pl.BlockSpec((tt, d), lambda i: (i, 0)),
        out_shape=jax.ShapeDtypeStruct((t, d), F32),
        scratch_shapes=[pltpu.VMEM((TOP_K, tt, nc, LANES), F32), pltpu.SemaphoreType.DMA],
        compiler_params=_cparams(("arbitrary",)),
        name="moe_combine",
    )(ti_flat, rk_flat, pstart, gates, h1, modtab, vec(ln_g), vec(ln_b), yp)


def _moe_layer(h1, xr, ti, gt, rk, counts, modtab, ln_g, ln_b, w_up, b_up, w_down, b_down, *,
               alpha, tiles_per_batch, ctx_tile0):
    t = h1.shape[0]
    n_e = w_up.shape[0]
    r = MOE_ROWS
    n_blocks = -(-(t * TOP_K) // r) + n_e
    cnt = counts[0, :n_e].astype(I32)
    padded = (cnt + r - 1) // r * r
    pends = jnp.cumsum(padded)
    pstart = jnp.zeros((ROUTE_LANES,), I32).at[:n_e].set(pends - padded)
    tail = pends[-1] + jnp.arange(n_e, dtype=I32) * r
    zstart = jnp.concatenate([jnp.where(cnt > 0, pends - r, -1),
                              jnp.where(tail < n_blocks * r, tail, -1)]).astype(I32)
    blk_row0 = jnp.arange(n_blocks, dtype=I32) * r
    blk_e = jnp.minimum(jnp.sum((pends[None, :] <= blk_row0[:, None]).astype(I32), axis=1), n_e - 1)
    n_used = (pends[-1] // r).astype(I32).reshape(1)
    ti_flat = ti[:, :TOP_K].reshape(-1)
    rk_flat = rk[:, :TOP_K].reshape(-1)
    xp = _dispatch(ti_flat, rk_flat, pstart, zstart, xr, n_blocks * r)
    yp = _experts(blk_e, n_used, xp, w_up, b_up, w_down, b_down)
    return _combine(ti_flat, rk_flat, pstart, gt, h1, modtab, ln_g, ln_b, yp,
                    alpha=alpha, tiles_per_batch=tiles_per_batch, ctx_tile0=ctx_tile0)


def kernel(x, c, ctx, c_ctx, w_ada, b_ada, ln1_g, ln1_b, ln2_g, ln2_b, s5_lam_re, s5_lam_im, s5_log_step, s5_b_re, s5_b_im, s5_c_re, s5_c_im, s5_d, s5_w_glu, gla_w_in, gla_w_a2, gla_b_a2, gla_norm_g, gla_w_out, moe_w_router, moe_b_router, moe_w_up, moe_b_up, moe_w_down, moe_b_down):
    bn, l, d = x.shape
    lc = ctx.shape[1]
    depth = w_ada.shape[0]
    ltot = l + lc
    rows = l // GRID_W
    alpha = (2 * depth) ** 0.25
    assert l % TOK_TILE == 0 and lc % TOK_TILE == 0 and l % S5_CHUNK == 0 and lc % S5_CHUNK == 0
    assert bn < SUBLANES
    tiles_per_batch = ltot // TOK_TILE
    ctx_tile0 = l // TOK_TILE
    tile_kw = dict(tiles_per_batch=tiles_per_batch, ctx_tile0=ctx_tile0)

    cc = jnp.zeros((SUBLANES, d), F32).at[:bn].set(c.astype(F32)).at[bn].set(c_ctx.astype(F32))
    mod = _ada_table(cc, w_ada.astype(F32), b_ada.astype(F32))
    mod_lat = mod[:, :bn].reshape(depth, bn, 1, 6, d)
    mod_ctx = jnp.broadcast_to(mod[:, bn].reshape(depth, 1, 1, 6, d), (depth, bn, 1, 6, d))
    modtab = jnp.concatenate([mod_lat, mod_ctx], axis=2)

    def to_cm(h3):
        lat = h3[:, :l].reshape(bn, rows, GRID_W, d).transpose(0, 2, 1, 3).reshape(bn, l, d)
        return jnp.concatenate([lat, h3[:, l:]], axis=1)

    def to_rm(h3):
        lat = h3[:, :l].reshape(bn, GRID_W, rows, d).transpose(0, 2, 1, 3).reshape(bn, l, d)
        return jnp.concatenate([lat, h3[:, l:]], axis=1)

    h3 = jnp.concatenate([x.astype(F32), ctx.astype(F32)], axis=1)
    for i in range(depth):
        j = i // 2
        mt = modtab[i]
        if i % 2 == 0:
            ys = []
            for dr, rev in ((0, False), (1, True)):
                wb, wc, a_re, a_im = _s5_prepare(s5_lam_re[j, dr], s5_lam_im[j, dr], s5_log_step[j, dr],
                                                 s5_b_re[j, dr], s5_b_im[j, dr], s5_c_re[j, dr], s5_c_im[j, dr], bn)
                ys.append(_s5_scan(h3, mt, wb, wc, a_re, a_im, rev=rev, n_lat=l).reshape(bn * ltot, d))
            h2d = h3.reshape(bn * ltot, d)
            outs = _post_mixer("s5", h2d, mt, ys[0], ys[1], (s5_d[j],), s5_w_glu[j].astype(BF16),
                               ln1_g[i], ln1_b[i], moe_w_router[i], moe_b_router[i], alpha=alpha, **tile_kw)
        else:
            h3 = to_cm(h3)
            h2d = h3.reshape(bn * ltot, d)
            w_in = gla_w_in[j]
            n_main = w_in.shape[1] - 2 * GLA_GATE_RANK
            w_a = w_in[:, n_main:].reshape(d, 2, GLA_GATE_RANK).transpose(1, 0, 2).astype(BF16)
            q, k, v, g, la_f, la_b = _gla_proj(h2d, mt, w_in[:, :n_main].astype(BF16), w_a,
                                               gla_w_a2[j].astype(BF16), gla_b_a2[j], **tile_kw)
            r3 = lambda a: a.reshape(bn, ltot, a.shape[-1])
            o_f = _gla_rec(r3(q), r3(k), r3(v), r3(la_f), rev=False, n_lat=l).reshape(bn * ltot, d)
            o_b = _gla_rec(r3(q), r3(k), r3(v), r3(la_b), rev=True, n_lat=l).reshape(bn * ltot, d)
            outs = _post_mixer("gla", h2d, mt, o_f, o_b, (g, gla_norm_g[j]), gla_w_out[j].astype(BF16),
                               ln1_g[i], ln1_b[i], moe_w_router[i], moe_b_router[i], alpha=alpha, **tile_kw)
        h1, xr, ti, gt, rk, counts = outs
        h2 = _moe_layer(h1, xr, ti, gt, rk, counts, mt, ln2_g[i], ln2_b[i],
                        moe_w_up[i], moe_b_up[i], moe_w_down[i], moe_b_down[i], alpha=alpha, **tile_kw)
        h3 = h2.reshape(bn, ltot, d)
        if i % 2 == 1:
            h3 = to_rm(h3)
    return h3[:, :l].astype(x.dtype)
```

```python
import functools
import math

import jax
import jax.numpy as jnp
from jax import lax
from jax.experimental import pallas as pl
from jax.experimental.pallas import tpu as pltpu

F32 = jnp.float32
BF16 = jnp.bfloat16
I32 = jnp.int32
HIGHEST = lax.Precision.HIGHEST

GRID_W = 64
S5_GROUP = 16
S5_STATE = 64
GLA_HEADS = 4
GLA_GATE_RANK = 16
GLA_GATE_NORM = 16.0
GLA_CHUNK = 64
N_EXPERTS = 32
TOP_K = 4
SWIGLU_ALPHA = 1.702
SWIGLU_LIMIT = 7.0
LN_EPS = 1e-5

LANES = 128
SUBLANES = 8
MXU_K = 256
VMEM_LIMIT = 56 * 1024 * 1024

TOK_TILE = 256
S5_CHUNK = 128
S5_PITCH = S5_CHUNK + SUBLANES
MOE_ROWS = 512
FF_SLICE = 1024
ROUTE_LANES = LANES


def _sigmoid(x):
    return 1.0 / (1.0 + jnp.exp(-x))


def _cparams(sem):
    return pltpu.CompilerParams(dimension_semantics=sem, vmem_limit_bytes=VMEM_LIMIT)


def _ada_body(c_ref, w_ref, b_ref, o_ref):
    c = c_ref[...]
    cond = c * _sigmoid(c)
    o_ref[...] = jnp.dot(cond, w_ref[...], precision=HIGHEST, preferred_element_type=F32) + b_ref[...]


def _ada_table(cc, w_ada, b_ada):
    depth, d, six_d = w_ada.shape
    n_tiles = six_d // d
    return pl.pallas_call(
        _ada_body,
        grid=(depth, n_tiles),
        in_specs=[
            pl.BlockSpec((SUBLANES, d), lambda i, n: (0, 0)),
            pl.BlockSpec((None, d, d), lambda i, n: (i, 0, n)),
            pl.BlockSpec((None, 1, d), lambda i, n: (i, 0, n)),
        ],
        out_specs=pl.BlockSpec((None, SUBLANES, d), lambda i, n: (i, 0, n)),
        out_shape=jax.ShapeDtypeStruct((depth, SUBLANES, six_d), F32),
        compiler_params=_cparams(("arbitrary", "arbitrary")),
        name="ada_table",
    )(cc, w_ada, b_ada.reshape(depth, 1, six_d))


def _s5_scan_body(x_ref, mod_ref, wb_ref, wc_ref, are_ref, aim_ref, y_ref,
                  sre_ref, sim_ref, st_re, st_im, *, rev, tc, pitch, nb, n_kt, cpk):
    j = pl.program_id(0)
    ncol = n_kt * cpk
    half = cpk * LANES

    @pl.when(j == 0)
    def _():
        st_re[...] = jnp.zeros_like(st_re)
        st_im[...] = jnp.zeros_like(st_im)

    u = jnp.concatenate(
        [(x_ref[b] * (1.0 + mod_ref[b, 1:2, :]) + mod_ref[b, 0:1, :]).astype(BF16) for b in range(nb)], axis=0)
    for kt in range(n_kt):
        r = jnp.dot(u[:, MXU_K * kt:MXU_K * (kt + 1)], wb_ref[kt], preferred_element_type=F32)
        for b in range(nb):
            rb = r[b * tc:(b + 1) * tc]
            for c in range(cpk):
                sre_ref[kt * cpk + c, b * pitch:b * pitch + tc, :] = rb[:, LANES * c:LANES * (c + 1)]
                sim_ref[kt * cpk + c, b * pitch:b * pitch + tc, :] = rb[:, half + LANES * c:half + LANES * (c + 1)]

    grp = 8
    for cg in range(ncol // grp):
        cols = list(range(cg * grp, (cg + 1) * grp))
        ar = [are_ref[c] for c in cols]
        ai = [aim_ref[c] for c in cols]
        init = tuple(st_re[c] for c in cols) + tuple(st_im[c] for c in cols)

        def step(t, carry, cols=cols, ar=ar, ai=ai):
            tt = (tc - 1 - t) if rev else t
            out_re, out_im = [], []
            for k, c in enumerate(cols):
                rows = pl.ds(tt, nb, stride=pitch)
                pr, pi = carry[k], carry[grp + k]
                nr = ar[k] * pr - ai[k] * pi + sre_ref[c, rows, :]
                ni = ar[k] * pi + ai[k] * pr + sim_ref[c, rows, :]
                sre_ref[c, rows, :] = nr
                sim_ref[c, rows, :] = ni
                out_re.append(nr)
                out_im.append(ni)
            return tuple(out_re) + tuple(out_im)

        fin = lax.fori_loop(0, tc, step, init)
        for k, c in enumerate(cols):
            st_re[c] = fin[k]
            st_im[c] = fin[grp + k]

    def stacked(ref, kt):
        return jnp.concatenate(
            [jnp.concatenate([ref[kt * cpk + c, b * pitch:b * pitch + tc, :] for c in range(cpk)], axis=-1)
             for b in range(nb)], axis=0).astype(BF16)

    for kt in range(n_kt):
        y = (jnp.dot(stacked(sre_ref, kt), wc_ref[kt, :half, :], preferred_element_type=F32)
             + jnp.dot(stacked(sim_ref, kt), wc_ref[kt, half:, :], preferred_element_type=F32))
        for b in range(nb):
            y_ref[b, :, MXU_K * kt:MXU_K * (kt + 1)] = y[b * tc:(b + 1) * tc]


def _s5_scan(h3, modtab, wb, wc, a_re, a_im, *, rev, n_lat):
    nb, ltot, d = h3.shape
    tc = S5_CHUNK
    n_chunks = ltot // tc
    lat_chunks = n_lat // tc
    n_kt, _, two_half = wb.shape
    cpk = two_half // (2 * LANES)
    ncol = n_kt * cpk

    if rev:
        chunk = lambda j: n_chunks - 1 - j
    else:
        chunk = lambda j: (j + lat_chunks) % n_chunks
    is_ctx = lambda j: (chunk(j) >= lat_chunks).astype(I32)

    body = functools.partial(_s5_scan_body, rev=rev, tc=tc, pitch=S5_PITCH, nb=nb, n_kt=n_kt, cpk=cpk)
    return pl.pallas_call(
        body,
        grid=(n_chunks,),
        in_specs=[
            pl.BlockSpec((nb, tc, d), lambda j: (0, chunk(j), 0)),
            pl.BlockSpec((nb, None, 6, d), lambda j: (0, is_ctx(j), 0, 0)),
            pl.BlockSpec(wb.shape, lambda j: (0, 0, 0)),
            pl.BlockSpec(wc.shape, lambda j: (0, 0, 0)),
            pl.BlockSpec(a_re.shape, lambda j: (0, 0, 0)),
            pl.BlockSpec(a_im.shape, lambda j: (0, 0, 0)),
        ],
        out_specs=pl.BlockSpec((nb, tc, d), lambda j: (0, chunk(j), 0)),
        out_shape=jax.ShapeDtypeStruct((nb, ltot, d), F32),
        scratch_shapes=[
            pltpu.VMEM((ncol, nb * S5_PITCH, LANES), F32),
            pltpu.VMEM((ncol, nb * S5_PITCH, LANES), F32),
            pltpu.VMEM((ncol, nb, LANES), F32),
            pltpu.VMEM((ncol, nb, LANES), F32),
        ],
        compiler_params=_cparams(("arbitrary",)),
        name="s5_scan_bwd" if rev else "s5_scan_fwd",
    )(h3, modtab, wb, wc, a_re, a_im)


def _s5_prepare(lam_re, lam_im, log_step, b_re, b_im, c_re, c_im, nb):
    g, p = lam_re.shape
    hg = b_re.shape[-1]
    lr = lam_re.astype(F32)
    li = lam_im.astype(F32)
    dt = jnp.exp(log_step.astype(F32))[:, None]
    mag = jnp.exp(lr * dt)
    ar = mag * jnp.cos(li * dt)
    ai = mag * jnp.sin(li * dt)
    den = lr * lr + li * li
    nr = ar - 1.0
    kr = (nr * lr + ai * li) / den
    ki = (ai * lr - nr * li) / den
    br = b_re.astype(F32)
    bi = b_im.astype(F32)
    bbr = kr[..., None] * br - ki[..., None] * bi
    bbi = kr[..., None] * bi + ki[..., None] * br
    gpk = MXU_K // hg
    n_kt = g // gpk
    eye = jnp.eye(gpk, dtype=F32)

    def block_diag(t):
        full = t[:, :, :, None, :] * eye[None, :, None, :, None]
        return full.reshape(n_kt, gpk * t.shape[2], gpk * t.shape[3])

    def in_blocks(bb):
        return block_diag(bb.reshape(n_kt, gpk, p, hg).transpose(0, 1, 3, 2))

    def out_blocks(cc):
        return block_diag(cc.reshape(n_kt, gpk, hg, p).transpose(0, 1, 3, 2))

    wb = jnp.concatenate([in_blocks(bbr), in_blocks(bbi)], axis=-1).astype(BF16)
    wc = jnp.concatenate([out_blocks(c_re.astype(F32)), -out_blocks(c_im.astype(F32))], axis=1).astype(BF16)
    ncol = g * p // LANES
    a_re = jnp.broadcast_to(ar.reshape(ncol, 1, LANES), (ncol, nb, LANES))
    a_im = jnp.broadcast_to(ai.reshape(ncol, 1, LANES), (ncol, nb, LANES))
    return wb, wc, a_re, a_im


def _gla_proj_body(x_ref, mod_ref, w_ref, wa_ref, wa2_ref, ba2_ref,
                   q_ref, k_ref, v_ref, g_ref, laf_ref, lab_ref, *, qk_w, v_w, dk):
    u = (x_ref[...] * (1.0 + mod_ref[1:2, :]) + mod_ref[0:1, :]).astype(BF16)
    p = jnp.dot(u, w_ref[...], preferred_element_type=F32)
    q_ref[...] = p[:, :qk_w] * (dk ** -0.5)
    k_ref[...] = p[:, qk_w:2 * qk_w]
    v_ref[...] = p[:, 2 * qk_w:2 * qk_w + v_w]
    g_ref[...] = p[:, 2 * qk_w + v_w:]
    for dr, out in enumerate((laf_ref, lab_ref)):
        a_d = jnp.dot(u, wa_ref[dr], preferred_element_type=F32).astype(BF16)
        z = jnp.dot(a_d, wa2_ref[dr], preferred_element_type=F32) + ba2_ref[dr]
        out[...] = (jnp.minimum(z, 0.0) - jnp.log1p(jnp.exp(-jnp.abs(z)))) / GLA_GATE_NORM


def _gla_proj(h2d, modtab, w_main, w_a, w_a2, b_a2, *, tiles_per_batch, ctx_tile0):
    t, d = h2d.shape
    tt = TOK_TILE
    qk_w = w_a2.shape[-1]
    v_w = (w_main.shape[1] - 2 * qk_w) // 2
    dk = qk_w // GLA_HEADS
    mod_map = lambda i: (i // tiles_per_batch, ((i % tiles_per_batch) >= ctx_tile0).astype(I32), 0, 0)
    row = lambda w: pl.BlockSpec((tt, w), lambda i: (i, 0))
    full = lambda a: pl.BlockSpec(a.shape, lambda i: (0,) * a.ndim)
    body = functools.partial(_gla_proj_body, qk_w=qk_w, v_w=v_w, dk=dk)
    b_a2r = b_a2.reshape(2, 1, qk_w).astype(F32)
    return pl.pallas_call(
        body,
        grid=(t // tt,),
        in_specs=[row(d), pl.BlockSpec((None, None, 6, d), mod_map),
                  full(w_main), full(w_a), full(w_a2), full(b_a2r)],
        out_specs=[row(qk_w), row(qk_w), row(v_w), row(v_w), row(qk_w), row(qk_w)],
        out_shape=[jax.ShapeDtypeStruct((t, w), F32) for w in (qk_w, qk_w, v_w, v_w, qk_w, qk_w)],
        compiler_params=_cparams(("parallel",)),
        name="gla_proj",
    )(h2d, modtab, w_main, w_a, w_a2, b_a2r)


def _gla_rec_body(qf_ref, kf_ref, vf_ref, laf_ref, qb_ref, kb_ref, vb_ref, lab_ref, of_ref, ob_ref, s_ref,
                  *, ch, dk, dv, nh, nb):
    @pl.when(pl.program_id(0) == 0)
    def _():
        s_ref[...] = jnp.zeros_like(s_ref)

    _gla_chunk(qf_ref, kf_ref, vf_ref, laf_ref, of_ref, s_ref, 0, rev=False, ch=ch, dk=dk, dv=dv, nh=nh, nb=nb)
    _gla_chunk(qb_ref, kb_ref, vb_ref, lab_ref, ob_ref, s_ref, nb * nh, rev=True, ch=ch, dk=dk, dv=dv, nh=nh, nb=nb)


def _gla_chunk(q_ref, k_ref, v_ref, la_ref, o_ref, s_ref, s0, *, rev, ch, dk, dv, nh, nb):
    r_i = lax.broadcasted_iota(I32, (ch, ch), 0)
    c_i = lax.broadcasted_iota(I32, (ch, ch), 1)
    seen = (c_i >= r_i) if rev else (c_i <= r_i)
    seen_f = seen.astype(F32)
    end = 0 if rev else ch - 1
    for bi in range(nb):
        b = jnp.dot(seen_f, la_ref[bi], precision=HIGHEST, preferred_element_type=F32)
        b_end = b[end:end + 1, :]
        k_all = k_ref[bi]
        q_d_all = (q_ref[bi] * jnp.exp(b)).astype(BF16)
        k_d_all = (k_all * jnp.exp(-b)).astype(BF16)
        k_e_all = (k_all * jnp.exp(b_end - b)).astype(BF16)
        g_cols = jnp.transpose(jnp.broadcast_to(jnp.exp(b_end), (SUBLANES, nh * dk)))[:, 0:1]
        for hd in range(nh):
            qs = slice(hd * dk, (hd + 1) * dk)
            vs = slice(hd * dv, (hd + 1) * dv)
            q_d, k_d, k_e = q_d_all[:, qs], k_d_all[:, qs], k_e_all[:, qs]
            v = v_ref[bi, :, vs].astype(BF16)
            att = lax.dot_general(q_d, k_d, (((1,), (1,)), ((), ())), preferred_element_type=F32)
            att = jnp.where(seen, att, 0.0).astype(BF16)
            s_prev = s_ref[s0 + bi * nh + hd]
            o = (jnp.dot(att, v, preferred_element_type=F32)
                 + jnp.dot(q_d, s_prev.astype(BF16), preferred_element_type=F32))
            o_ref[bi, :, vs] = o
            upd = lax.dot_general(k_e, v, (((0,), (0,)), ((), ())), preferred_element_type=F32)
            s_ref[s0 + bi * nh + hd] = s_prev * g_cols[qs, :] + upd


def _gla_rec(q, k, v, la_f, la_b, *, n_lat):
    nb, ltot, qk_w = q.shape
    v_w = v.shape[-1]
    ch = GLA_CHUNK
    n_chunks = ltot // ch
    lat_chunks = n_lat // ch
    fwd = lambda j: (j + lat_chunks) % n_chunks
    bwd = lambda j: n_chunks - 1 - j
    spec = lambda w, chunk: pl.BlockSpec((nb, ch, w), lambda j: (0, chunk(j), 0))
    dk, dv = qk_w // GLA_HEADS, v_w // GLA_HEADS
    body = functools.partial(_gla_rec_body, ch=ch, dk=dk, dv=dv, nh=GLA_HEADS, nb=nb)
    out = jax.ShapeDtypeStruct((nb, ltot, v_w), F32)
    return pl.pallas_call(
        body,
        grid=(n_chunks,),
        in_specs=[spec(qk_w, fwd), spec(qk_w, fwd), spec(v_w, fwd), spec(qk_w, fwd),
                  spec(qk_w, bwd), spec(qk_w, bwd), spec(v_w, bwd), spec(qk_w, bwd)],
        out_specs=[spec(v_w, fwd), spec(v_w, bwd)],
        out_shape=[out, out],
        scratch_shapes=[pltpu.VMEM((2 * nb * GLA_HEADS, dk, dv), F32)],
        compiler_params=_cparams(("arbitrary",)),
        name="gla_rec",
    )(q, k, v, la_f, q, k, v, la_b)


def _layer_norm_rows(v, g, b):
    mu = jnp.mean(v, axis=-1, keepdims=True)
    c = v - mu
    var = jnp.mean(c * c, axis=-1, keepdims=True)
    return c * lax.rsqrt(var + LN_EPS) * g + b


def _post_mixer_body(*refs, kind, alpha, tt, dv):
    if kind == "s5":
        (h_ref, mod_ref, ya_ref, yb_ref, dsk_ref, w_ref, lng_ref, lnb_ref, wr_ref, br_ref,
         h1_ref, xr_ref, ti_ref, gt_ref, rk_ref, cnt_ref, base_ref) = refs
    else:
        (h_ref, mod_ref, ya_ref, yb_ref, gate_ref, ng_ref, w_ref, lng_ref, lnb_ref, wr_ref, br_ref,
         h1_ref, xr_ref, ti_ref, gt_ref, rk_ref, cnt_ref, base_ref) = refs
    i = pl.program_id(0)

    @pl.when(i == 0)
    def _():
        base_ref[...] = jnp.zeros_like(base_ref)

    h = h_ref[...]
    d = h.shape[-1]
    if kind == "s5":
        u = h * (1.0 + mod_ref[1:2, :]) + mod_ref[0:1, :]
        y = dsk_ref[...] * u + ya_ref[...] + yb_ref[...]
        ge = 0.5 * y * (1.0 + jnp.tanh(math.sqrt(2.0 / math.pi) * (y + 0.044715 * (y * y * y))))
        z = jnp.dot(ge.astype(BF16), w_ref[...], preferred_element_type=F32)
        mix = z[:, :d] * _sigmoid(z[:, d:])
    else:
        o = ya_ref[...] + yb_ref[...]
        parts = []
        for hd in range(d // dv):
            oh = o[:, hd * dv:(hd + 1) * dv]
            ms = jnp.mean(oh * oh, axis=-1, keepdims=True)
            parts.append(oh * lax.rsqrt(ms + LN_EPS))
        on = jnp.concatenate(parts, axis=-1) * ng_ref[...]
        gv = gate_ref[...]
        a = on * (gv * _sigmoid(gv))
        mix = jnp.dot(a.astype(BF16), w_ref[...], preferred_element_type=F32)

    h1 = _layer_norm_rows(alpha * h + mod_ref[2:3, :] * mix, lng_ref[...], lnb_ref[...])
    h1_ref[...] = h1
    u2 = h1 * (1.0 + mod_ref[4:5, :]) + mod_ref[3:4, :]
    nc = d // LANES
    for c in range(nc):
        xr_ref[pl.ds(c, tt, stride=nc), :] = u2[:, c * LANES:(c + 1) * LANES]

    logits = jnp.dot(u2, wr_ref[...], precision=HIGHEST, preferred_element_type=F32) + br_ref[...]
    lane = lax.broadcasted_iota(I32, logits.shape, 1)
    work = logits
    vals, idxs = [], []
    for _ in range(TOP_K):
        m = jnp.max(work, axis=-1, keepdims=True)
        idx = jnp.min(jnp.where(work == m, lane, ROUTE_LANES), axis=-1, keepdims=True)
        vals.append(m)
        idxs.append(idx)
        work = jnp.where(lane == idx, -jnp.inf, work)
    exps = [jnp.exp(v - vals[0]) for v in vals]
    den = exps[0]
    for e in exps[1:]:
        den = den + e

    multi = jnp.zeros(logits.shape, F32)
    for idx in idxs:
        multi = multi + (lane == idx).astype(F32)
    r_i = lax.broadcasted_iota(I32, (tt, tt), 0)
    c_i = lax.broadcasted_iota(I32, (tt, tt), 1)
    before = (c_i < r_i).astype(BF16)
    pos = jnp.dot(before, multi.astype(BF16), preferred_element_type=F32) + base_ref[0:1, :]
    ti = jnp.zeros(logits.shape, I32)
    gt = jnp.zeros(logits.shape, F32)
    rk = jnp.zeros(logits.shape, F32)
    for kk in range(TOP_K):
        ti = jnp.where(lane == kk, idxs[kk], ti)
        gt = jnp.where(lane == kk, exps[kk] / den, gt)
        rank_k = jnp.sum(jnp.where(lane == idxs[kk], pos, 0.0), axis=-1, keepdims=True)
        rk = jnp.where(lane == kk, rank_k, rk)
    ti_ref[...] = ti
    gt_ref[...] = gt
    rk_ref[...] = rk.astype(I32)
    new_base = base_ref[0:1, :] + jnp.sum(multi, axis=0, keepdims=True)
    base_ref[...] = jnp.broadcast_to(new_base, base_ref.shape)
    cnt_ref[...] = jnp.broadcast_to(new_base, cnt_ref.shape)


def _post_mixer(kind, h2d, modtab, ya, yb, extra, w_mix, ln_g, ln_b, w_router, b_router, *,
                alpha, tiles_per_batch, ctx_tile0):
    t, d = h2d.shape
    tt = TOK_TILE
    n_e = w_router.shape[-1]
    wr = jnp.zeros((d, ROUTE_LANES), F32).at[:, :n_e].set(w_router.astype(F32))
    br = jnp.full((1, ROUTE_LANES), jnp.finfo(F32).min, F32).at[0, :n_e].set(b_router.astype(F32))
    mod_map = lambda i: (i // tiles_per_batch, ((i % tiles_per_batch) >= ctx_tile0).astype(I32), 0, 0)
    row = lambda w: pl.BlockSpec((tt, w), lambda i: (i, 0))
    full = lambda a: pl.BlockSpec(a.shape, lambda i: (0,) * a.ndim)
    vec = lambda a: a.reshape(1, -1).astype(F32)
    if kind == "s5":
        (d_skip,) = extra
        ins = [h2d, modtab, ya, yb, vec(d_skip), w_mix, vec(ln_g), vec(ln_b), wr, br]
        in_specs = [row(d), pl.BlockSpec((None, None, 6, d), mod_map), row(d), row(d)]
        in_specs += [full(a) for a in ins[4:]]
        dv = d
    else:
        gate, norm_g = extra
        dv = norm_g.shape[-1]
        ng = jnp.tile(norm_g.astype(F32), d // dv).reshape(1, d)
        ins = [h2d, modtab, ya, yb, gate, ng, w_mix, vec(ln_g), vec(ln_b), wr, br]
        in_specs = [row(d), pl.BlockSpec((None, None, 6, d), mod_map), row(d), row(d), row(d)]
        in_specs += [full(a) for a in ins[5:]]
    body = functools.partial(_post_mixer_body, kind=kind, alpha=alpha, tt=tt, dv=dv)
    return pl.pallas_call(
        body,
        grid=(t // tt,),
        in_specs=in_specs,
        out_specs=[
            row(d),
            pl.BlockSpec((tt * (d // LANES), LANES), lambda i: (i, 0)),
            row(ROUTE_LANES), row(ROUTE_LANES), row(ROUTE_LANES),
            pl.BlockSpec((SUBLANES, ROUTE_LANES), lambda i: (0, 0)),
        ],
        out_shape=[
            jax.ShapeDtypeStruct((t, d), F32),
            jax.ShapeDtypeStruct((t * (d // LANES), LANES), F32),
            jax.ShapeDtypeStruct((t, ROUTE_LANES), I32),
            jax.ShapeDtypeStruct((t, ROUTE_LANES), F32),
            jax.ShapeDtypeStruct((t, ROUTE_LANES), I32),
            jax.ShapeDtypeStruct((SUBLANES, ROUTE_LANES), F32),
        ],
        scratch_shapes=[pltpu.VMEM((SUBLANES, ROUTE_LANES), F32)],
        compiler_params=_cparams(("arbitrary",)),
        name="post_mixer_" + kind,
    )(*ins)


def _row_copy_wait(src_hbm, dst, sem, n_rows):
    pltpu.make_async_copy(src_hbm.at[pl.ds(0, n_rows), :], dst.at[pl.ds(0, n_rows), :], sem).wait()


def _tok_rows(t, nc):
    return pl.ds(pl.multiple_of(t * nc, nc), nc)


def _dispatch_body(ti_ref, rk_ref, ps_ref, zs_ref, x_ref, xp_hbm, zbuf, sem, zsem, *, tt, nc, n_e):
    zr = zbuf.shape[0]

    @pl.when(pl.program_id(0) == 0)
    def _():
        zbuf[...] = jnp.zeros_like(zbuf)
        for e in range(2 * n_e):
            @pl.when(zs_ref[e] >= 0)
            def _():
                pltpu.make_async_copy(zbuf, xp_hbm.at[pl.ds(pl.multiple_of(zs_ref[e] * nc, nc), zr), :], zsem).start()
        for e in range(2 * n_e):
            @pl.when(zs_ref[e] >= 0)
            def _():
                pltpu.make_async_copy(zbuf, xp_hbm.at[pl.ds(pl.multiple_of(zs_ref[e] * nc, nc), zr), :], zsem).wait()

    def issue(t, c):
        for kk in range(TOP_K):
            s = t * TOP_K + kk
            dst = ps_ref[ti_ref[s]] + rk_ref[s]
            pltpu.make_async_copy(x_ref.at[_tok_rows(t, nc), :], xp_hbm.at[_tok_rows(dst, nc), :],
                                  sem).start(priority=kk % 2)
        return c

    lax.fori_loop(0, tt, issue, 0, unroll=4)
    _row_copy_wait(xp_hbm, xp_hbm, sem, tt * TOP_K * nc)


def _dispatch(ti_flat, rk_flat, pstart, zstart, xr, n_rows, nc):
    t = xr.shape[0] // nc
    tt = TOK_TILE
    smem_blk = pl.BlockSpec((tt * TOP_K,), lambda i: (i,), memory_space=pltpu.SMEM)
    return pl.pallas_call(
        functools.partial(_dispatch_body, tt=tt, nc=nc, n_e=N_EXPERTS),
        grid=(t // tt,),
        in_specs=[smem_blk, smem_blk,
                  pl.BlockSpec(memory_space=pltpu.SMEM),
                  pl.BlockSpec(memory_space=pltpu.SMEM),
                  pl.BlockSpec((tt * nc, LANES), lambda i: (i, 0))],
        out_specs=pl.BlockSpec(memory_space=pl.ANY),
        out_shape=jax.ShapeDtypeStruct((n_rows * nc, LANES), xr.dtype),
        scratch_shapes=[pltpu.VMEM((MOE_ROWS * nc, LANES), xr.dtype),
                        pltpu.SemaphoreType.DMA, pltpu.SemaphoreType.DMA],
        compiler_params=_cparams(("arbitrary",)),
        name="moe_dispatch",
    )(ti_flat, rk_flat, pstart, zstart, xr)


def _expert_body(be_ref, nu_ref, x_ref, wu_ref, bu_ref, wd_ref, bd_ref, y_ref, wub, wdb):
    i = pl.program_id(0)
    e = be_ref[i]
    prev = be_ref[jnp.maximum(i - 1, 0)]

    @pl.when((i == 0) | (e != prev))
    def _():
        wub[...] = wu_ref[...].astype(BF16)
        wdb[...] = wd_ref[...].astype(BF16)

    nc = wu_ref.shape[0] // LANES
    r = x_ref.shape[0] // nc

    @pl.when(i < nu_ref[0])
    def _():
        x = jnp.concatenate([x_ref[pl.ds(c, r, stride=nc), :] for c in range(nc)], axis=-1).astype(BF16)
        ff = wdb.shape[0]
        y = bd_ref[...]
        for j in range(ff // FF_SLICE):
            lo, hi = j * FF_SLICE, (j + 1) * FF_SLICE
            h_glu = jnp.dot(x, wub[:, lo:hi], preferred_element_type=F32) + bu_ref[:, lo:hi]
            h_lin = jnp.dot(x, wub[:, ff + lo:ff + hi], preferred_element_type=F32) + bu_ref[:, ff + lo:ff + hi]
            h_glu = jnp.minimum(h_glu, SWIGLU_LIMIT)
            h_lin = jnp.clip(h_lin, -SWIGLU_LIMIT, SWIGLU_LIMIT)
            a = h_glu * _sigmoid(SWIGLU_ALPHA * h_glu) * (h_lin + 1.0)
            y = y + jnp.dot(a.astype(BF16), wdb[lo:hi, :], preferred_element_type=F32)
        for c in range(nc):
            y_ref[pl.ds(c, r, stride=nc), :] = y[:, c * LANES:(c + 1) * LANES]

    @pl.when(i >= nu_ref[0])
    def _():
        y_ref[...] = jnp.zeros_like(y_ref)


def _experts(blk_e, n_used, xp, w_up, b_up, w_down, b_down):
    n_e, d, two_ff = w_up.shape
    nc = d // LANES
    n_rows = xp.shape[0] // nc
    r = MOE_ROWS
    ff = two_ff // 2
    grid_spec = pltpu.PrefetchScalarGridSpec(
        num_scalar_prefetch=2,
        grid=(n_rows // r,),
        in_specs=[
            pl.BlockSpec((r * nc, LANES), lambda i, be, nu: (jnp.minimum(i, nu[0] - 1), 0)),
            pl.BlockSpec((None, d, two_ff), lambda i, be, nu: (be[i], 0, 0)),
            pl.BlockSpec((None, 1, two_ff), lambda i, be, nu: (be[i], 0, 0)),
            pl.BlockSpec((None, ff, d), lambda i, be, nu: (be[i], 0, 0)),
            pl.BlockSpec((None, 1, d), lambda i, be, nu: (be[i], 0, 0)),
        ],
        out_specs=pl.BlockSpec((r * nc, LANES), lambda i, be, nu: (i, 0)),
        scratch_shapes=[pltpu.VMEM((d, two_ff), BF16), pltpu.VMEM((ff, d), BF16)],
    )
    return pl.pallas_call(
        _expert_body,
        grid_spec=grid_spec,
        out_shape=jax.ShapeDtypeStruct(xp.shape, F32),
        compiler_params=_cparams(("arbitrary",)),
        name="moe_experts",
    )(blk_e, n_used, xp, w_up, b_up.reshape(n_e, 1, two_ff), w_down, b_down.reshape(n_e, 1, d))


def _combine_body(ti_ref, rk_ref, ps_ref, gt_ref, h_ref, mod_ref, lng_ref, lnb_ref, yp_hbm,
                  o_ref, buf, sems, *, tt, nc, n_tiles, alpha):
    s = pl.program_id(0)
    slot = s % 2

    @pl.when(s < n_tiles)
    def _():
        def issue(t, c):
            for kk in range(TOP_K):
                q = t * TOP_K + kk
                src = ps_ref[ti_ref[q]] + rk_ref[q]
                pltpu.make_async_copy(yp_hbm.at[_tok_rows(src, nc), :], buf.at[slot, kk, _tok_rows(t, nc), :],
                                      sems.at[slot]).start(priority=kk % 2)
            return c

        lax.fori_loop(0, tt, issue, 0, unroll=4)

    @pl.when(s > 0)
    def _():
        prev = 1 - slot
        for kk in range(TOP_K):
            _row_copy_wait(yp_hbm, buf.at[prev, kk], sems.at[prev], tt * nc)
        gt = gt_ref[...]
        cols = []
        for c in range(nc):
            acc = gt[:, 0:1] * buf[prev, 0, pl.ds(c, tt, stride=nc), :]
            for kk in range(1, TOP_K):
                acc = acc + gt[:, kk:kk + 1] * buf[prev, kk, pl.ds(c, tt, stride=nc), :]
            cols.append(acc)
        f = jnp.concatenate(cols, axis=-1)
        o_ref[...] = _layer_norm_rows(alpha * h_ref[...] + mod_ref[5:6, :] * f, lng_ref[...], lnb_ref[...])


def _combine(ti_flat, rk_flat, pstart, gates, h1, modtab, ln_g, ln_b, yp, *, alpha, tiles_per_batch, ctx_tile0):
    t, d = h1.shape
    tt = TOK_TILE
    nc = d // LANES
    n_tiles = t // tt
    nxt = lambda s: jnp.minimum(s, n_tiles - 1)
    cur = lambda s: jnp.maximum(s - 1, 0)
    smem_blk = pl.BlockSpec((tt * TOP_K,), lambda s: (nxt(s),), memory_space=pltpu.SMEM)
    mod_map = lambda s: (cur(s) // tiles_per_batch, ((cur(s) % tiles_per_batch) >= ctx_tile0).astype(I32), 0, 0)
    vec = lambda a: a.reshape(1, -1).astype(F32)
    return pl.pallas_call(
        functools.partial(_combine_body, tt=tt, nc=nc, n_tiles=n_tiles, alpha=alpha),
        grid=(n_tiles + 1,),
        in_specs=[smem_blk, smem_blk,
                  pl.BlockSpec(memory_space=pltpu.SMEM),
                  pl.BlockSpec((tt, ROUTE_LANES), lambda s: (cur(s), 0)),
                  pl.BlockSpec((tt, d), lambda s: (cur(s), 0)),
                  pl.BlockSpec((None, None, 6, d), mod_map),
                  pl.BlockSpec((1, d), lambda s: (0, 0)),
                  pl.BlockSpec((1, d), lambda s: (0, 0)),
                  pl.BlockSpec(memory_space=pl.ANY)],
        out_specs=pl.BlockSpec((tt, d), lambda s: (cur(s), 0)),
        out_shape=jax.ShapeDtypeStruct((t, d), F32),
        scratch_shapes=[pltpu.VMEM((2, TOP_K, tt * nc, LANES), F32), pltpu.SemaphoreType.DMA((2,))],
        compiler_params=_cparams(("arbitrary",)),
        name="moe_combine",
    )(ti_flat, rk_flat, pstart, gates, h1, modtab, vec(ln_g), vec(ln_b), yp)


def _moe_layer(h1, xr, ti, gt, rk, counts, modtab, ln_g, ln_b, w_up, b_up, w_down, b_down, *,
               alpha, tiles_per_batch, ctx_tile0):
    t, d = h1.shape
    n_e = w_up.shape[0]
    r = MOE_ROWS
    n_blocks = -(-(t * TOP_K) // r) + n_e
    cnt = counts[0, :n_e].astype(I32)
    padded = (cnt + r - 1) // r * r
    pends = jnp.cumsum(padded)
    pstart = jnp.zeros((ROUTE_LANES,), I32).at[:n_e].set(pends - padded)
    tail = pends[-1] + jnp.arange(n_e, dtype=I32) * r
    zstart = jnp.concatenate([jnp.where(cnt > 0, pends - r, -1),
                              jnp.where(tail < n_blocks * r, tail, -1)]).astype(I32)
    blk_row0 = jnp.arange(n_blocks, dtype=I32) * r
    blk_e = jnp.minimum(jnp.sum((pends[None, :] <= blk_row0[:, None]).astype(I32), axis=1), n_e - 1)
    n_used = (pends[-1] // r).astype(I32).reshape(1)
    ti_flat = ti[:, :TOP_K].reshape(-1)
    rk_flat = rk[:, :TOP_K].reshape(-1)
    xp = _dispatch(ti_flat, rk_flat, pstart, zstart, xr, n_blocks * r, d // LANES)
    yp = _experts(blk_e, n_used, xp, w_up, b_up, w_down, b_down)
    return _combine(ti_flat, rk_flat, pstart, gt, h1, modtab, ln_g, ln_b, yp,
                    alpha=alpha, tiles_per_batch=tiles_per_batch, ctx_tile0=ctx_tile0)


def _to_cm_body(lat_ref, ctx_ref, o_ref, *, n_cols):
    w = pl.program_id(1)

    @pl.when(w < n_cols)
    def _():
        o_ref[...] = lat_ref[...]

    @pl.when(w >= n_cols)
    def _():
        o_ref[...] = ctx_ref[...]


def _to_col_major(h3, n_lat):
    bn, ltot, d = h3.shape
    rows = n_lat // GRID_W
    assert (ltot - n_lat) % rows == 0
    n_ctx_blk = (ltot - n_lat) // rows
    return pl.pallas_call(
        functools.partial(_to_cm_body, n_cols=GRID_W),
        grid=(bn, GRID_W + n_ctx_blk),
        in_specs=[pl.BlockSpec((None, rows, d), lambda b, w: (b, 0, jnp.minimum(w, GRID_W - 1))),
                  pl.BlockSpec((None, rows, d), lambda b, w: (b, jnp.maximum(w, GRID_W), 0))],
        out_specs=pl.BlockSpec((None, rows, d), lambda b, w: (b, w, 0)),
        out_shape=jax.ShapeDtypeStruct(h3.shape, h3.dtype),
        compiler_params=_cparams(("arbitrary", "arbitrary")),
        name="to_col_major",
    )(h3.reshape(bn, ltot // GRID_W, GRID_W * d), h3)


def _to_rm_body(lat_ref, ctx_ref, o_ref, *, rows, n_ctx_rows):
    w = pl.program_id(1)
    o_ref[0:rows, :] = lat_ref[...]
    for m in range(n_ctx_rows):
        o_ref[rows + m:rows + m + 1, :] = ctx_ref[pl.ds(w + GRID_W * m, 1), :]


def _to_row_major(h3, n_lat):
    bn, ltot, d = h3.shape
    lc = ltot - n_lat
    rows = n_lat // GRID_W
    assert lc % GRID_W == 0 and n_lat % lc == 0
    out = pl.pallas_call(
        functools.partial(_to_rm_body, rows=rows, n_ctx_rows=lc // GRID_W),
        grid=(bn, GRID_W),
        in_specs=[pl.BlockSpec((None, rows, d), lambda b, w: (b, w, 0)),
                  pl.BlockSpec((None, lc, d), lambda b, w: (b, n_lat // lc, 0))],
        out_specs=pl.BlockSpec((None, ltot // GRID_W, d), lambda b, w: (b, 0, w)),
        out_shape=jax.ShapeDtypeStruct((bn, ltot // GRID_W, GRID_W * d), h3.dtype),
        compiler_params=_cparams(("arbitrary", "arbitrary")),
        name="to_row_major",
    )(h3, h3)
    return out.reshape(bn, ltot, d)


def kernel(x, c, ctx, c_ctx, w_ada, b_ada, ln1_g, ln1_b, ln2_g, ln2_b, s5_lam_re, s5_lam_im, s5_log_step, s5_b_re, s5_b_im, s5_c_re, s5_c_im, s5_d, s5_w_glu, gla_w_in, gla_w_a2, gla_b_a2, gla_norm_g, gla_w_out, moe_w_router, moe_b_router, moe_w_up, moe_b_up, moe_w_down, moe_b_down):
    bn, l, d = x.shape
    lc = ctx.shape[1]
    depth = w_ada.shape[0]
    ltot = l + lc
    rows = l // GRID_W
    alpha = (2 * depth) ** 0.25
    assert l % TOK_TILE == 0 and lc % TOK_TILE == 0 and l % S5_CHUNK == 0 and lc % S5_CHUNK == 0
    assert bn < SUBLANES
    tiles_per_batch = ltot // TOK_TILE
    ctx_tile0 = l // TOK_TILE
    tile_kw = dict(tiles_per_batch=tiles_per_batch, ctx_tile0=ctx_tile0)

    cc = jnp.zeros((SUBLANES, d), F32).at[:bn].set(c.astype(F32)).at[bn].set(c_ctx.astype(F32))
    mod = _ada_table(cc, w_ada.astype(F32), b_ada.astype(F32))
    mod_lat = mod[:, :bn].reshape(depth, bn, 1, 6, d)
    mod_ctx = jnp.broadcast_to(mod[:, bn].reshape(depth, 1, 1, 6, d), (depth, bn, 1, 6, d))
    modtab = jnp.concatenate([mod_lat, mod_ctx], axis=2)

    to_cm = functools.partial(_to_col_major, n_lat=l)
    to_rm = functools.partial(_to_row_major, n_lat=l)

    h3 = jnp.concatenate([x.astype(F32), ctx.astype(F32)], axis=1)
    for i in range(depth):
        j = i // 2
        mt = modtab[i]
        if i % 2 == 0:
            ys = []
            for dr, rev in ((0, False), (1, True)):
                wb, wc, a_re, a_im = _s5_prepare(s5_lam_re[j, dr], s5_lam_im[j, dr], s5_log_step[j, dr],
                                                 s5_b_re[j, dr], s5_b_im[j, dr], s5_c_re[j, dr], s5_c_im[j, dr], bn)
                ys.append(_s5_scan(h3, mt, wb, wc, a_re, a_im, rev=rev, n_lat=l).reshape(bn * ltot, d))
            h2d = h3.reshape(bn * ltot, d)
            outs = _post_mixer("s5", h2d, mt, ys[0], ys[1], (s5_d[j],), s5_w_glu[j].astype(BF16),
                               ln1_g[i], ln1_b[i], moe_w_router[i], moe_b_router[i], alpha=alpha, **tile_kw)
        else:
            h3 = to_cm(h3)
            h2d = h3.reshape(bn * ltot, d)
            w_in = gla_w_in[j]
            n_main = w_in.shape[1] - 2 * GLA_GATE_RANK
            w_a = w_in[:, n_main:].reshape(d, 2, GLA_GATE_RANK).transpose(1, 0, 2).astype(BF16)
            q, k, v, g, la_f, la_b = _gla_proj(h2d, mt, w_in[:, :n_main].astype(BF16), w_a,
                                               gla_w_a2[j].astype(BF16), gla_b_a2[j], **tile_kw)
            r3 = lambda a: a.reshape(bn, ltot, a.shape[-1])
            o_f, o_b = [o.reshape(bn * ltot, d) for o in _gla_rec(r3(q), r3(k), r3(v), r3(la_f), r3(la_b), n_lat=l)]
            outs = _post_mixer("gla", h2d, mt, o_f, o_b, (g, gla_norm_g[j]), gla_w_out[j].astype(BF16),
                               ln1_g[i], ln1_b[i], moe_w_router[i], moe_b_router[i], alpha=alpha, **tile_kw)
        h1, xr, ti, gt, rk, counts = outs
        h2 = _moe_layer(h1, xr, ti, gt, rk, counts, mt, ln2_g[i], ln2_b[i],
                        moe_w_up[i], moe_b_up[i], moe_w_down[i], moe_b_down[i], alpha=alpha, **tile_kw)
        h3 = h2.reshape(bn, ltot, d)
        if i % 2 == 1:
            h3 = to_rm(h3)
    return h3[:, :l].astype(x.dtype)
```

```python
import functools
import math

import jax
import jax.numpy as jnp
from jax import lax
from jax.experimental import pallas as pl
from jax.experimental.pallas import tpu as pltpu

F32 = jnp.float32
BF16 = jnp.bfloat16
I32 = jnp.int32
HIGHEST = lax.Precision.HIGHEST

GRID_W = 64
S5_GROUP = 16
S5_STATE = 64
GLA_HEADS = 4
GLA_GATE_RANK = 16
GLA_GATE_NORM = 16.0
GLA_CHUNK = 64
N_EXPERTS = 32
TOP_K = 4
SWIGLU_ALPHA = 1.702
SWIGLU_LIMIT = 7.0
LN_EPS = 1e-5

LANES = 128
SUBLANES = 8
MXU_K = 256
VMEM_LIMIT = 56 * 1024 * 1024

TOK_TILE = 256
S5_CHUNK = 128
S5_PITCH = S5_CHUNK + SUBLANES
MOE_ROWS = 512
FF_SLICE = 1024
ROUTE_LANES = LANES


def _sigmoid(x):
    return 1.0 / (1.0 + jnp.exp(-x))


def _cparams(sem):
    return pltpu.CompilerParams(dimension_semantics=sem, vmem_limit_bytes=VMEM_LIMIT)


def _ada_body(c_ref, w_ref, b_ref, o_ref):
    c = c_ref[...]
    cond = c * _sigmoid(c)
    o_ref[...] = jnp.dot(cond, w_ref[...], precision=HIGHEST, preferred_element_type=F32) + b_ref[...]


def _ada_table(cc, w_ada, b_ada):
    depth, d, six_d = w_ada.shape
    n_tiles = six_d // d
    return pl.pallas_call(
        _ada_body,
        grid=(depth, n_tiles),
        in_specs=[
            pl.BlockSpec((SUBLANES, d), lambda i, n: (0, 0)),
            pl.BlockSpec((None, d, d), lambda i, n: (i, 0, n)),
            pl.BlockSpec((None, 1, d), lambda i, n: (i, 0, n)),
        ],
        out_specs=pl.BlockSpec((None, SUBLANES, d), lambda i, n: (i, 0, n)),
        out_shape=jax.ShapeDtypeStruct((depth, SUBLANES, six_d), F32),
        compiler_params=_cparams(("arbitrary", "arbitrary")),
        name="ada_table",
    )(cc, w_ada, b_ada.reshape(depth, 1, six_d))


def _s5_scan_body(x_ref, mod_ref, wb_ref, wc_ref, are_ref, aim_ref, y_ref,
                  sre_ref, sim_ref, st_re, st_im, *, rev, tc, pitch, nb, n_kt, cpk, pk):
    j = pl.program_id(0)
    ncol = n_kt * cpk
    npc = ncol // pk
    half = cpk * LANES

    def seg_rows(c, b):
        r0 = ((c // npc) * nb + b) * pitch
        return c % npc, slice(r0, r0 + tc)

    @pl.when(j == 0)
    def _():
        st_re[...] = jnp.zeros_like(st_re)
        st_im[...] = jnp.zeros_like(st_im)

    u = jnp.concatenate(
        [(x_ref[b] * (1.0 + mod_ref[b, 1:2, :]) + mod_ref[b, 0:1, :]).astype(BF16) for b in range(nb)], axis=0)
    for kt in range(n_kt):
        r = jnp.dot(u[:, MXU_K * kt:MXU_K * (kt + 1)], wb_ref[kt], preferred_element_type=F32)
        for b in range(nb):
            rb = r[b * tc:(b + 1) * tc]
            for c in range(cpk):
                pc, rws = seg_rows(kt * cpk + c, b)
                sre_ref[pc, rws, :] = rb[:, LANES * c:LANES * (c + 1)]
                sim_ref[pc, rws, :] = rb[:, half + LANES * c:half + LANES * (c + 1)]

    grp = 8
    for cg in range(npc // grp):
        cols = list(range(cg * grp, (cg + 1) * grp))
        ar = [are_ref[c] for c in cols]
        ai = [aim_ref[c] for c in cols]
        init = tuple(st_re[c] for c in cols) + tuple(st_im[c] for c in cols)

        def step(t, carry, cols=cols, ar=ar, ai=ai):
            tt = (tc - 1 - t) if rev else t
            out_re, out_im = [], []
            for k, c in enumerate(cols):
                rows = pl.ds(tt, pk * nb, stride=pitch)
                pr, pi = carry[k], carry[grp + k]
                nr = ar[k] * pr - ai[k] * pi + sre_ref[c, rows, :]
                ni = ar[k] * pi + ai[k] * pr + sim_ref[c, rows, :]
                sre_ref[c, rows, :] = nr
                sim_ref[c, rows, :] = ni
                out_re.append(nr)
                out_im.append(ni)
            return tuple(out_re) + tuple(out_im)

        fin = lax.fori_loop(0, tc, step, init)
        for k, c in enumerate(cols):
            st_re[c] = fin[k]
            st_im[c] = fin[grp + k]

    def stacked(ref, kt):
        def piece(c, b):
            pc, rws = seg_rows(kt * cpk + c, b)
            return ref[pc, rws, :]

        return jnp.concatenate(
            [jnp.concatenate([piece(c, b) for c in range(cpk)], axis=-1) for b in range(nb)], axis=0).astype(BF16)

    for kt in range(n_kt):
        y = (jnp.dot(stacked(sre_ref, kt), wc_ref[kt, :half, :], preferred_element_type=F32)
             + jnp.dot(stacked(sim_ref, kt), wc_ref[kt, half:, :], preferred_element_type=F32))
        for b in range(nb):
            y_ref[b, :, MXU_K * kt:MXU_K * (kt + 1)] = y[b * tc:(b + 1) * tc]


def _s5_scan(h3, modtab, wb, wc, a_re, a_im, *, rev, n_lat):
    nb, ltot, d = h3.shape
    tc = S5_CHUNK
    n_chunks = ltot // tc
    lat_chunks = n_lat // tc
    n_kt, _, two_half = wb.shape
    cpk = two_half // (2 * LANES)
    ncol = n_kt * cpk
    pk = SUBLANES // nb
    npc = ncol // pk

    if rev:
        chunk = lambda j: n_chunks - 1 - j
    else:
        chunk = lambda j: (j + lat_chunks) % n_chunks
    is_ctx = lambda j: (chunk(j) >= lat_chunks).astype(I32)

    body = functools.partial(_s5_scan_body, rev=rev, tc=tc, pitch=S5_PITCH, nb=nb, n_kt=n_kt, cpk=cpk, pk=pk)
    return pl.pallas_call(
        body,
        grid=(n_chunks,),
        in_specs=[
            pl.BlockSpec((nb, tc, d), lambda j: (0, chunk(j), 0)),
            pl.BlockSpec((nb, None, 6, d), lambda j: (0, is_ctx(j), 0, 0)),
            pl.BlockSpec(wb.shape, lambda j: (0, 0, 0)),
            pl.BlockSpec(wc.shape, lambda j: (0, 0, 0)),
            pl.BlockSpec(a_re.shape, lambda j: (0, 0, 0)),
            pl.BlockSpec(a_im.shape, lambda j: (0, 0, 0)),
        ],
        out_specs=pl.BlockSpec((nb, tc, d), lambda j: (0, chunk(j), 0)),
        out_shape=jax.ShapeDtypeStruct((nb, ltot, d), F32),
        scratch_shapes=[
            pltpu.VMEM((npc, pk * nb * S5_PITCH, LANES), F32),
            pltpu.VMEM((npc, pk * nb * S5_PITCH, LANES), F32),
            pltpu.VMEM((npc, pk * nb, LANES), F32),
            pltpu.VMEM((npc, pk * nb, LANES), F32),
        ],
        compiler_params=_cparams(("arbitrary",)),
        name="s5_scan_bwd" if rev else "s5_scan_fwd",
    )(h3, modtab, wb, wc, a_re, a_im)


def _s5_prepare(lam_re, lam_im, log_step, b_re, b_im, c_re, c_im, nb):
    g, p = lam_re.shape
    hg = b_re.shape[-1]
    lr = lam_re.astype(F32)
    li = lam_im.astype(F32)
    dt = jnp.exp(log_step.astype(F32))[:, None]
    mag = jnp.exp(lr * dt)
    ar = mag * jnp.cos(li * dt)
    ai = mag * jnp.sin(li * dt)
    den = lr * lr + li * li
    nr = ar - 1.0
    kr = (nr * lr + ai * li) / den
    ki = (ai * lr - nr * li) / den
    br = b_re.astype(F32)
    bi = b_im.astype(F32)
    bbr = kr[..., None] * br - ki[..., None] * bi
    bbi = kr[..., None] * bi + ki[..., None] * br
    gpk = MXU_K // hg
    n_kt = g // gpk
    eye = jnp.eye(gpk, dtype=F32)

    def block_diag(t):
        full = t[:, :, :, None, :] * eye[None, :, None, :, None]
        return full.reshape(n_kt, gpk * t.shape[2], gpk * t.shape[3])

    def in_blocks(bb):
        return block_diag(bb.reshape(n_kt, gpk, p, hg).transpose(0, 1, 3, 2))

    def out_blocks(cc):
        return block_diag(cc.reshape(n_kt, gpk, hg, p).transpose(0, 1, 3, 2))

    wb = jnp.concatenate([in_blocks(bbr), in_blocks(bbi)], axis=-1).astype(BF16)
    wc = jnp.concatenate([out_blocks(c_re.astype(F32)), -out_blocks(c_im.astype(F32))], axis=1).astype(BF16)
    ncol = g * p // LANES
    pk = SUBLANES // nb
    npc = ncol // pk

    def packed_rows(a):
        t = a.reshape(pk, npc, 1, LANES).transpose(1, 0, 2, 3)
        return jnp.broadcast_to(t, (npc, pk, nb, LANES)).reshape(npc, pk * nb, LANES)

    return wb, wc, packed_rows(ar), packed_rows(ai)


def _gla_proj_body(x_ref, mod_ref, w_ref, wa_ref, wa2_ref, ba2_ref,
                   q_ref, k_ref, v_ref, g_ref, laf_ref, lab_ref, *, qk_w, v_w, dk):
    u = (x_ref[...] * (1.0 + mod_ref[1:2, :]) + mod_ref[0:1, :]).astype(BF16)
    p = jnp.dot(u, w_ref[...], preferred_element_type=F32)
    q_ref[...] = p[:, :qk_w] * (dk ** -0.5)
    k_ref[...] = p[:, qk_w:2 * qk_w]
    v_ref[...] = p[:, 2 * qk_w:2 * qk_w + v_w]
    g_ref[...] = p[:, 2 * qk_w + v_w:]
    for dr, out in enumerate((laf_ref, lab_ref)):
        a_d = jnp.dot(u, wa_ref[dr], preferred_element_type=F32).astype(BF16)
        z = jnp.dot(a_d, wa2_ref[dr], preferred_element_type=F32) + ba2_ref[dr]
        out[...] = (jnp.minimum(z, 0.0) - jnp.log1p(jnp.exp(-jnp.abs(z)))) / GLA_GATE_NORM


def _gla_proj(h2d, modtab, w_main, w_a, w_a2, b_a2, *, tiles_per_batch, ctx_tile0):
    t, d = h2d.shape
    tt = TOK_TILE
    qk_w = w_a2.shape[-1]
    v_w = (w_main.shape[1] - 2 * qk_w) // 2
    dk = qk_w // GLA_HEADS
    mod_map = lambda i: (i // tiles_per_batch, ((i % tiles_per_batch) >= ctx_tile0).astype(I32), 0, 0)
    row = lambda w: pl.BlockSpec((tt, w), lambda i: (i, 0))
    full = lambda a: pl.BlockSpec(a.shape, lambda i: (0,) * a.ndim)
    body = functools.partial(_gla_proj_body, qk_w=qk_w, v_w=v_w, dk=dk)
    b_a2r = b_a2.reshape(2, 1, qk_w).astype(F32)
    return pl.pallas_call(
        body,
        grid=(t // tt,),
        in_specs=[row(d), pl.BlockSpec((None, None, 6, d), mod_map),
                  full(w_main), full(w_a), full(w_a2), full(b_a2r)],
        out_specs=[row(qk_w), row(qk_w), row(v_w), row(v_w), row(qk_w), row(qk_w)],
        out_shape=[jax.ShapeDtypeStruct((t, w), F32) for w in (qk_w, qk_w, v_w, v_w, qk_w, qk_w)],
        compiler_params=_cparams(("parallel",)),
        name="gla_proj",
    )(h2d, modtab, w_main, w_a, w_a2, b_a2r)


def _gla_rec_body(qf_ref, kf_ref, vf_ref, laf_ref, qb_ref, kb_ref, vb_ref, lab_ref, of_ref, ob_ref, s_ref,
                  *, ch, dk, dv, nh, nb):
    @pl.when(pl.program_id(0) == 0)
    def _():
        s_ref[...] = jnp.zeros_like(s_ref)

    _gla_chunk(qf_ref, kf_ref, vf_ref, laf_ref, of_ref, s_ref, 0, rev=False, ch=ch, dk=dk, dv=dv, nh=nh, nb=nb)
    _gla_chunk(qb_ref, kb_ref, vb_ref, lab_ref, ob_ref, s_ref, nb * nh, rev=True, ch=ch, dk=dk, dv=dv, nh=nh, nb=nb)


def _gla_chunk(q_ref, k_ref, v_ref, la_ref, o_ref, s_ref, s0, *, rev, ch, dk, dv, nh, nb):
    r_i = lax.broadcasted_iota(I32, (ch, ch), 0)
    c_i = lax.broadcasted_iota(I32, (ch, ch), 1)
    seen = (c_i >= r_i) if rev else (c_i <= r_i)
    seen_f = seen.astype(F32)
    end = 0 if rev else ch - 1
    for bi in range(nb):
        b = jnp.dot(seen_f, la_ref[bi], precision=HIGHEST, preferred_element_type=F32)
        b_end = b[end:end + 1, :]
        k_all = k_ref[bi]
        q_d_all = (q_ref[bi] * jnp.exp(b)).astype(BF16)
        k_d_all = (k_all * jnp.exp(-b)).astype(BF16)
        k_e_all = (k_all * jnp.exp(b_end - b)).astype(BF16)
        g_cols = jnp.transpose(jnp.broadcast_to(jnp.exp(b_end), (SUBLANES, nh * dk)))[:, 0:1]
        for hd in range(nh):
            qs = slice(hd * dk, (hd + 1) * dk)
            vs = slice(hd * dv, (hd + 1) * dv)
            q_d, k_d, k_e = q_d_all[:, qs], k_d_all[:, qs], k_e_all[:, qs]
            v = v_ref[bi, :, vs].astype(BF16)
            att = lax.dot_general(q_d, k_d, (((1,), (1,)), ((), ())), preferred_element_type=F32)
            att = jnp.where(seen, att, 0.0).astype(BF16)
            s_prev = s_ref[s0 + bi * nh + hd]
            o = (jnp.dot(att, v, preferred_element_type=F32)
                 + jnp.dot(q_d, s_prev.astype(BF16), preferred_element_type=F32))
            o_ref[bi, :, vs] = o
            upd = lax.dot_general(k_e, v, (((0,), (0,)), ((), ())), preferred_element_type=F32)
            s_ref[s0 + bi * nh + hd] = s_prev * g_cols[qs, :] + upd


def _gla_rec(q, k, v, la_f, la_b, *, n_lat):
    nb, ltot, qk_w = q.shape
    v_w = v.shape[-1]
    ch = GLA_CHUNK
    n_chunks = ltot // ch
    lat_chunks = n_lat // ch
    fwd = lambda j: (j + lat_chunks) % n_chunks
    bwd = lambda j: n_chunks - 1 - j
    spec = lambda w, chunk: pl.BlockSpec((nb, ch, w), lambda j: (0, chunk(j), 0))
    dk, dv = qk_w // GLA_HEADS, v_w // GLA_HEADS
    body = functools.partial(_gla_rec_body, ch=ch, dk=dk, dv=dv, nh=GLA_HEADS, nb=nb)
    out = jax.ShapeDtypeStruct((nb, ltot, v_w), F32)
    return pl.pallas_call(
        body,
        grid=(n_chunks,),
        in_specs=[spec(qk_w, fwd), spec(qk_w, fwd), spec(v_w, fwd), spec(qk_w, fwd),
                  spec(qk_w, bwd), spec(qk_w, bwd), spec(v_w, bwd), spec(qk_w, bwd)],
        out_specs=[spec(v_w, fwd), spec(v_w, bwd)],
        out_shape=[out, out],
        scratch_shapes=[pltpu.VMEM((2 * nb * GLA_HEADS, dk, dv), F32)],
        compiler_params=_cparams(("arbitrary",)),
        name="gla_rec",
    )(q, k, v, la_f, q, k, v, la_b)


def _layer_norm_rows(v, g, b):
    mu = jnp.mean(v, axis=-1, keepdims=True)
    c = v - mu
    var = jnp.mean(c * c, axis=-1, keepdims=True)
    return c * lax.rsqrt(var + LN_EPS) * g + b


def _post_mixer_body(*refs, kind, alpha, tt, dv):
    if kind == "s5":
        (h_ref, mod_ref, ya_ref, yb_ref, dsk_ref, w_ref, lng_ref, lnb_ref, wr_ref, br_ref,
         h1_ref, xr_ref, ti_ref, gt_ref, rk_ref, cnt_ref, base_ref) = refs
    else:
        (h_ref, mod_ref, ya_ref, yb_ref, gate_ref, ng_ref, w_ref, lng_ref, lnb_ref, wr_ref, br_ref,
         h1_ref, xr_ref, ti_ref, gt_ref, rk_ref, cnt_ref, base_ref) = refs
    i = pl.program_id(0)

    @pl.when(i == 0)
    def _():
        base_ref[...] = jnp.zeros_like(base_ref)

    h = h_ref[...]
    d = h.shape[-1]
    if kind == "s5":
        u = h * (1.0 + mod_ref[1:2, :]) + mod_ref[0:1, :]
        y = dsk_ref[...] * u + ya_ref[...] + yb_ref[...]
        ge = 0.5 * y * (1.0 + jnp.tanh(math.sqrt(2.0 / math.pi) * (y + 0.044715 * (y * y * y))))
        z = jnp.dot(ge.astype(BF16), w_ref[...], preferred_element_type=F32)
        mix = z[:, :d] * _sigmoid(z[:, d:])
    else:
        o = ya_ref[...] + yb_ref[...]
        parts = []
        for hd in range(d // dv):
            oh = o[:, hd * dv:(hd + 1) * dv]
            ms = jnp.mean(oh * oh, axis=-1, keepdims=True)
            parts.append(oh * lax.rsqrt(ms + LN_EPS))
        on = jnp.concatenate(parts, axis=-1) * ng_ref[...]
        gv = gate_ref[...]
        a = on * (gv * _sigmoid(gv))
        mix = jnp.dot(a.astype(BF16), w_ref[...], preferred_element_type=F32)

    h1 = _layer_norm_rows(alpha * h + mod_ref[2:3, :] * mix, lng_ref[...], lnb_ref[...])
    h1_ref[...] = h1
    u2 = h1 * (1.0 + mod_ref[4:5, :]) + mod_ref[3:4, :]
    nc = d // LANES
    for c in range(nc):
        xr_ref[pl.ds(c, tt, stride=nc), :] = u2[:, c * LANES:(c + 1) * LANES]

    logits = jnp.dot(u2, wr_ref[...], precision=HIGHEST, preferred_element_type=F32) + br_ref[...]
    lane = lax.broadcasted_iota(I32, logits.shape, 1)
    work = logits
    vals, idxs = [], []
    for _ in range(TOP_K):
        m = jnp.max(work, axis=-1, keepdims=True)
        idx = jnp.min(jnp.where(work == m, lane, ROUTE_LANES), axis=-1, keepdims=True)
        vals.append(m)
        idxs.append(idx)
        work = jnp.where(lane == idx, -jnp.inf, work)
    exps = [jnp.exp(v - vals[0]) for v in vals]
    den = exps[0]
    for e in exps[1:]:
        den = den + e

    multi = jnp.zeros(logits.shape, F32)
    for idx in idxs:
        multi = multi + (lane == idx).astype(F32)
    r_i = lax.broadcasted_iota(I32, (tt, tt), 0)
    c_i = lax.broadcasted_iota(I32, (tt, tt), 1)
    before = (c_i < r_i).astype(BF16)
    pos = jnp.dot(before, multi.astype(BF16), preferred_element_type=F32) + base_ref[0:1, :]
    ti = jnp.zeros(logits.shape, I32)
    gt = jnp.zeros(logits.shape, F32)
    rk = jnp.zeros(logits.shape, F32)
    for kk in range(TOP_K):
        ti = jnp.where(lane == kk, idxs[kk], ti)
        gt = jnp.where(lane == kk, exps[kk] / den, gt)
        rank_k = jnp.sum(jnp.where(lane == idxs[kk], pos, 0.0), axis=-1, keepdims=True)
        rk = jnp.where(lane == kk, rank_k, rk)
    ti_ref[...] = ti
    gt_ref[...] = gt
    rk_ref[...] = rk.astype(I32)
    new_base = base_ref[0:1, :] + jnp.sum(multi, axis=0, keepdims=True)
    base_ref[...] = jnp.broadcast_to(new_base, base_ref.shape)
    cnt_ref[...] = jnp.broadcast_to(new_base, cnt_ref.shape)


def _post_mixer(kind, h2d, modtab, ya, yb, extra, w_mix, ln_g, ln_b, w_router, b_router, *,
                alpha, tiles_per_batch, ctx_tile0):
    t, d = h2d.shape
    tt = TOK_TILE
    n_e = w_router.shape[-1]
    wr = jnp.zeros((d, ROUTE_LANES), F32).at[:, :n_e].set(w_router.astype(F32))
    br = jnp.full((1, ROUTE_LANES), jnp.finfo(F32).min, F32).at[0, :n_e].set(b_router.astype(F32))
    mod_map = lambda i: (i // tiles_per_batch, ((i % tiles_per_batch) >= ctx_tile0).astype(I32), 0, 0)
    row = lambda w: pl.BlockSpec((tt, w), lambda i: (i, 0))
    full = lambda a: pl.BlockSpec(a.shape, lambda i: (0,) * a.ndim)
    vec = lambda a: a.reshape(1, -1).astype(F32)
    if kind == "s5":
        (d_skip,) = extra
        ins = [h2d, modtab, ya, yb, vec(d_skip), w_mix, vec(ln_g), vec(ln_b), wr, br]
        in_specs = [row(d), pl.BlockSpec((None, None, 6, d), mod_map), row(d), row(d)]
        in_specs += [full(a) for a in ins[4:]]
        dv = d
    else:
        gate, norm_g = extra
        dv = norm_g.shape[-1]
        ng = jnp.tile(norm_g.astype(F32), d // dv).reshape(1, d)
        ins = [h2d, modtab, ya, yb, gate, ng, w_mix, vec(ln_g), vec(ln_b), wr, br]
        in_specs = [row(d), pl.BlockSpec((None, None, 6, d), mod_map), row(d), row(d), row(d)]
        in_specs += [full(a) for a in ins[5:]]
    body = functools.partial(_post_mixer_body, kind=kind, alpha=alpha, tt=tt, dv=dv)
    return pl.pallas_call(
        body,
        grid=(t // tt,),
        in_specs=in_specs,
        out_specs=[
            row(d),
            pl.BlockSpec((tt * (d // LANES), LANES), lambda i: (i, 0)),
            row(ROUTE_LANES), row(ROUTE_LANES), row(ROUTE_LANES),
            pl.BlockSpec((SUBLANES, ROUTE_LANES), lambda i: (0, 0)),
        ],
        out_shape=[
            jax.ShapeDtypeStruct((t, d), F32),
            jax.ShapeDtypeStruct((t * (d // LANES), LANES), F32),
            jax.ShapeDtypeStruct((t, ROUTE_LANES), I32),
            jax.ShapeDtypeStruct((t, ROUTE_LANES), F32),
            jax.ShapeDtypeStruct((t, ROUTE_LANES), I32),
            jax.ShapeDtypeStruct((SUBLANES, ROUTE_LANES), F32),
        ],
        scratch_shapes=[pltpu.VMEM((SUBLANES, ROUTE_LANES), F32)],
        compiler_params=_cparams(("arbitrary",)),
        name="post_mixer_" + kind,
    )(*ins)


def _row_copy_wait(src_hbm, dst, sem, n_rows):
    pltpu.make_async_copy(src_hbm.at[pl.ds(0, n_rows), :], dst.at[pl.ds(0, n_rows), :], sem).wait()


def _tok_rows(t, nc):
    return pl.ds(pl.multiple_of(t * nc, nc), nc)


def _dispatch_body(ti_ref, rk_ref, ps_ref, zs_ref, x_ref, xp_hbm, dst_ref, zbuf, sem, zsem, *, tt, nc, n_e):
    zr = zbuf.shape[0]

    @pl.when(pl.program_id(0) == 0)
    def _():
        zbuf[...] = jnp.zeros_like(zbuf)
        for e in range(2 * n_e):
            @pl.when(zs_ref[e] >= 0)
            def _():
                pltpu.make_async_copy(zbuf, xp_hbm.at[pl.ds(pl.multiple_of(zs_ref[e] * nc, nc), zr), :], zsem).start()
        for e in range(2 * n_e):
            @pl.when(zs_ref[e] >= 0)
            def _():
                pltpu.make_async_copy(zbuf, xp_hbm.at[pl.ds(pl.multiple_of(zs_ref[e] * nc, nc), zr), :], zsem).wait()

    def issue(t, c):
        for kk in range(TOP_K):
            s = t * TOP_K + kk
            dst = ps_ref[ti_ref[s]] + rk_ref[s]
            dst_ref[s] = dst
            pltpu.make_async_copy(x_ref.at[_tok_rows(t, nc), :], xp_hbm.at[_tok_rows(dst, nc), :],
                                  sem).start(priority=kk % 2)
        return c

    lax.fori_loop(0, tt, issue, 0, unroll=4)
    _row_copy_wait(xp_hbm, xp_hbm, sem, tt * TOP_K * nc)


def _dispatch(ti_flat, rk_flat, pstart, zstart, xr, n_rows, nc):
    t = xr.shape[0] // nc
    tt = TOK_TILE
    smem_blk = pl.BlockSpec((tt * TOP_K,), lambda i: (i,), memory_space=pltpu.SMEM)
    return pl.pallas_call(
        functools.partial(_dispatch_body, tt=tt, nc=nc, n_e=N_EXPERTS),
        grid=(t // tt,),
        in_specs=[smem_blk, smem_blk,
                  pl.BlockSpec(memory_space=pltpu.SMEM),
                  pl.BlockSpec(memory_space=pltpu.SMEM),
                  pl.BlockSpec((tt * nc, LANES), lambda i: (i, 0))],
        out_specs=[pl.BlockSpec(memory_space=pl.ANY), smem_blk],
        out_shape=[jax.ShapeDtypeStruct((n_rows * nc, LANES), xr.dtype),
                   jax.ShapeDtypeStruct(ti_flat.shape, I32)],
        scratch_shapes=[pltpu.VMEM((MOE_ROWS * nc, LANES), xr.dtype),
                        pltpu.SemaphoreType.DMA, pltpu.SemaphoreType.DMA],
        compiler_params=_cparams(("arbitrary",)),
        name="moe_dispatch",
    )(ti_flat, rk_flat, pstart, zstart, xr)


def _expert_body(be_ref, nu_ref, x_ref, wu_ref, bu_ref, wd_ref, bd_ref, y_ref, wub, wdb):
    i = pl.program_id(0)
    e = be_ref[i]
    prev = be_ref[jnp.maximum(i - 1, 0)]

    @pl.when((i == 0) | (e != prev))
    def _():
        wub[...] = wu_ref[...].astype(BF16)
        wdb[...] = wd_ref[...].astype(BF16)

    nc = wu_ref.shape[0] // LANES
    r = x_ref.shape[0] // nc

    @pl.when(i < nu_ref[0])
    def _():
        x = jnp.concatenate([x_ref[pl.ds(c, r, stride=nc), :] for c in range(nc)], axis=-1).astype(BF16)
        ff = wdb.shape[0]
        y = bd_ref[...]
        for j in range(ff // FF_SLICE):
            lo, hi = j * FF_SLICE, (j + 1) * FF_SLICE
            h_glu = jnp.dot(x, wub[:, lo:hi], preferred_element_type=F32) + bu_ref[:, lo:hi]
            h_lin = jnp.dot(x, wub[:, ff + lo:ff + hi], preferred_element_type=F32) + bu_ref[:, ff + lo:ff + hi]
            h_glu = jnp.minimum(h_glu, SWIGLU_LIMIT)
            h_lin = jnp.clip(h_lin, -SWIGLU_LIMIT, SWIGLU_LIMIT)
            a = h_glu * _sigmoid(SWIGLU_ALPHA * h_glu) * (h_lin + 1.0)
            y = y + jnp.dot(a.astype(BF16), wdb[lo:hi, :], preferred_element_type=F32)
        for c in range(nc):
            y_ref[pl.ds(c, r, stride=nc), :] = y[:, c * LANES:(c + 1) * LANES]

    @pl.when(i >= nu_ref[0])
    def _():
        y_ref[...] = jnp.zeros_like(y_ref)


def _experts(blk_e, n_used, xp, w_up, b_up, w_down, b_down):
    n_e, d, two_ff = w_up.shape
    nc = d // LANES
    n_rows = xp.shape[0] // nc
    r = MOE_ROWS
    ff = two_ff // 2
    grid_spec = pltpu.PrefetchScalarGridSpec(
        num_scalar_prefetch=2,
        grid=(n_rows // r,),
        in_specs=[
            pl.BlockSpec((r * nc, LANES), lambda i, be, nu: (jnp.minimum(i, nu[0] - 1), 0)),
            pl.BlockSpec((None, d, two_ff), lambda i, be, nu: (be[i], 0, 0)),
            pl.BlockSpec((None, 1, two_ff), lambda i, be, nu: (be[i], 0, 0)),
            pl.BlockSpec((None, ff, d), lambda i, be, nu: (be[i], 0, 0)),
            pl.BlockSpec((None, 1, d), lambda i, be, nu: (be[i], 0, 0)),
        ],
        out_specs=pl.BlockSpec((r * nc, LANES), lambda i, be, nu: (i, 0)),
        scratch_shapes=[pltpu.VMEM((d, two_ff), BF16), pltpu.VMEM((ff, d), BF16)],
    )
    return pl.pallas_call(
        _expert_body,
        grid_spec=grid_spec,
        out_shape=jax.ShapeDtypeStruct(xp.shape, F32),
        compiler_params=_cparams(("arbitrary",)),
        name="moe_experts",
    )(blk_e, n_used, xp, w_up, b_up.reshape(n_e, 1, two_ff), w_down, b_down.reshape(n_e, 1, d))


def _combine_body(dst_ref, gt_ref, h_ref, mod_ref, lng_ref, lnb_ref, yp_hbm,
                  o_ref, buf, sems, *, tt, nc, n_tiles, alpha):
    s = pl.program_id(0)
    slot = s % 2

    @pl.when(s < n_tiles)
    def _():
        def issue(t, c):
            for kk in range(TOP_K):
                q = t * TOP_K + kk
                src = dst_ref[q]
                pltpu.make_async_copy(yp_hbm.at[_tok_rows(src, nc), :], buf.at[slot, kk, _tok_rows(t, nc), :],
                                      sems.at[slot]).start(priority=kk % 2)
            return c

        lax.fori_loop(0, tt, issue, 0, unroll=4)

    @pl.when(s > 0)
    def _():
        prev = 1 - slot
        for kk in range(TOP_K):
            _row_copy_wait(yp_hbm, buf.at[prev, kk], sems.at[prev], tt * nc)
        gt = gt_ref[...]
        cols = []
        for c in range(nc):
            acc = gt[:, 0:1] * buf[prev, 0, pl.ds(c, tt, stride=nc), :]
            for kk in range(1, TOP_K):
                acc = acc + gt[:, kk:kk + 1] * buf[prev, kk, pl.ds(c, tt, stride=nc), :]
            cols.append(acc)
        f = jnp.concatenate(cols, axis=-1)
        o_ref[...] = _layer_norm_rows(alpha * h_ref[...] + mod_ref[5:6, :] * f, lng_ref[...], lnb_ref[...])


def _combine(dest, gates, h1, modtab, ln_g, ln_b, yp, *, alpha, tiles_per_batch, ctx_tile0):
    t, d = h1.shape
    tt = TOK_TILE
    nc = d // LANES
    n_tiles = t // tt
    nxt = lambda s: jnp.minimum(s, n_tiles - 1)
    cur = lambda s: jnp.maximum(s - 1, 0)
    smem_blk = pl.BlockSpec((tt * TOP_K,), lambda s: (nxt(s),), memory_space=pltpu.SMEM)
    mod_map = lambda s: (cur(s) // tiles_per_batch, ((cur(s) % tiles_per_batch) >= ctx_tile0).astype(I32), 0, 0)
    vec = lambda a: a.reshape(1, -1).astype(F32)
    return pl.pallas_call(
        functools.partial(_combine_body, tt=tt, nc=nc, n_tiles=n_tiles, alpha=alpha),
        grid=(n_tiles + 1,),
        in_specs=[smem_blk,
                  pl.BlockSpec((tt, ROUTE_LANES), lambda s: (cur(s), 0)),
                  pl.BlockSpec((tt, d), lambda s: (cur(s), 0)),
                  pl.BlockSpec((None, None, 6, d), mod_map),
                  pl.BlockSpec((1, d), lambda s: (0, 0)),
                  pl.BlockSpec((1, d), lambda s: (0, 0)),
                  pl.BlockSpec(memory_space=pl.ANY)],
        out_specs=pl.BlockSpec((tt, d), lambda s: (cur(s), 0)),
        out_shape=jax.ShapeDtypeStruct((t, d), F32),
        scratch_shapes=[pltpu.VMEM((2, TOP_K, tt * nc, LANES), F32), pltpu.SemaphoreType.DMA((2,))],
        compiler_params=_cparams(("arbitrary",)),
        name="moe_combine",
    )(dest, gates, h1, modtab, vec(ln_g), vec(ln_b), yp)


def _moe_layer(h1, xr, ti, gt, rk, counts, modtab, ln_g, ln_b, w_up, b_up, w_down, b_down, *,
               alpha, tiles_per_batch, ctx_tile0):
    t, d = h1.shape
    n_e = w_up.shape[0]
    r = MOE_ROWS
    n_blocks = -(-(t * TOP_K) // r) + n_e
    cnt = counts[0, :n_e].astype(I32)
    padded = (cnt + r - 1) // r * r
    pends = jnp.cumsum(padded)
    pstart = jnp.zeros((ROUTE_LANES,), I32).at[:n_e].set(pends - padded)
    tail = pends[-1] + jnp.arange(n_e, dtype=I32) * r
    zstart = jnp.concatenate([jnp.where(cnt > 0, pends - r, -1),
                              jnp.where(tail < n_blocks * r, tail, -1)]).astype(I32)
    blk_row0 = jnp.arange(n_blocks, dtype=I32) * r
    blk_e = jnp.minimum(jnp.sum((pends[None, :] <= blk_row0[:, None]).astype(I32), axis=1), n_e - 1)
    n_used = (pends[-1] // r).astype(I32).reshape(1)
    ti_flat = ti[:, :TOP_K].reshape(-1)
    rk_flat = rk[:, :TOP_K].reshape(-1)
    xp, dest = _dispatch(ti_flat, rk_flat, pstart, zstart, xr, n_blocks * r, d // LANES)
    yp = _experts(blk_e, n_used, xp, w_up, b_up, w_down, b_down)
    return _combine(dest, gt, h1, modtab, ln_g, ln_b, yp,
                    alpha=alpha, tiles_per_batch=tiles_per_batch, ctx_tile0=ctx_tile0)


def _to_cm_body(lat_ref, ctx_ref, o_ref, *, rows, k, n_lat_steps):
    w = pl.program_id(1)
    for q in range(SUBLANES // k):
        @pl.when((w < n_lat_steps) & (w % (SUBLANES // k) == q))
        def _():
            for i in range(k):
                o_ref[i * rows:(i + 1) * rows, :] = lat_ref[:, q * k + i, :]

    @pl.when(w >= n_lat_steps)
    def _():
        o_ref[...] = ctx_ref[...]


def _to_col_major(h3, n_lat):
    bn, ltot, d = h3.shape
    lc = ltot - n_lat
    rows = n_lat // GRID_W
    k = max(kk for kk in (1, 2, 4, 8) if lc % (kk * rows) == 0)
    ob = k * rows
    n_lat_steps = GRID_W // k
    per_blk = SUBLANES // k
    return pl.pallas_call(
        functools.partial(_to_cm_body, rows=rows, k=k, n_lat_steps=n_lat_steps),
        grid=(bn, n_lat_steps + lc // ob),
        in_specs=[pl.BlockSpec((None, rows, SUBLANES, d),
                               lambda b, w: (b, 0, jnp.minimum(w, n_lat_steps - 1) // per_blk, 0)),
                  pl.BlockSpec((None, ob, d), lambda b, w: (b, jnp.maximum(w, n_lat_steps), 0))],
        out_specs=pl.BlockSpec((None, ob, d), lambda b, w: (b, w, 0)),
        out_shape=jax.ShapeDtypeStruct(h3.shape, h3.dtype),
        compiler_params=_cparams(("arbitrary", "arbitrary")),
        name="to_col_major",
    )(h3.reshape(bn, ltot // GRID_W, GRID_W, d), h3)


def _to_rm_body(lat_ref, ctx_ref, o_ref, *, rows, n_ctx_rows):
    w = pl.program_id(1)
    for i in range(SUBLANES):
        o_ref[0:rows, i, :] = lat_ref[i * rows:(i + 1) * rows, :]
    for m in range(n_ctx_rows):
        o_ref[rows + m, :, :] = ctx_ref[pl.ds(pl.multiple_of(m * GRID_W + w * SUBLANES, SUBLANES), SUBLANES), :]


def _to_row_major(h3, n_lat):
    bn, ltot, d = h3.shape
    lc = ltot - n_lat
    rows = n_lat // GRID_W
    assert lc % GRID_W == 0 and n_lat % lc == 0
    out = pl.pallas_call(
        functools.partial(_to_rm_body, rows=rows, n_ctx_rows=lc // GRID_W),
        grid=(bn, GRID_W // SUBLANES),
        in_specs=[pl.BlockSpec((None, SUBLANES * rows, d), lambda b, w: (b, w, 0)),
                  pl.BlockSpec((None, lc, d), lambda b, w: (b, n_lat // lc, 0))],
        out_specs=pl.BlockSpec((None, ltot // GRID_W, SUBLANES, d), lambda b, w: (b, 0, w, 0)),
        out_shape=jax.ShapeDtypeStruct((bn, ltot // GRID_W, GRID_W, d), h3.dtype),
        compiler_params=_cparams(("arbitrary", "arbitrary")),
        name="to_row_major",
    )(h3, h3)
    return out.reshape(bn, ltot, d)


def kernel(x, c, ctx, c_ctx, w_ada, b_ada, ln1_g, ln1_b, ln2_g, ln2_b, s5_lam_re, s5_lam_im, s5_log_step, s5_b_re, s5_b_im, s5_c_re, s5_c_im, s5_d, s5_w_glu, gla_w_in, gla_w_a2, gla_b_a2, gla_norm_g, gla_w_out, moe_w_router, moe_b_router, moe_w_up, moe_b_up, moe_w_down, moe_b_down):
    bn, l, d = x.shape
    lc = ctx.shape[1]
    depth = w_ada.shape[0]
    ltot = l + lc
    rows = l // GRID_W
    alpha = (2 * depth) ** 0.25
    assert l % TOK_TILE == 0 and lc % TOK_TILE == 0 and l % S5_CHUNK == 0 and lc % S5_CHUNK == 0
    assert bn < SUBLANES
    tiles_per_batch = ltot // TOK_TILE
    ctx_tile0 = l // TOK_TILE
    tile_kw = dict(tiles_per_batch=tiles_per_batch, ctx_tile0=ctx_tile0)

    cc = jnp.zeros((SUBLANES, d), F32).at[:bn].set(c.astype(F32)).at[bn].set(c_ctx.astype(F32))
    mod = _ada_table(cc, w_ada.astype(F32), b_ada.astype(F32))
    mod_lat = mod[:, :bn].reshape(depth, bn, 1, 6, d)
    mod_ctx = jnp.broadcast_to(mod[:, bn].reshape(depth, 1, 1, 6, d), (depth, bn, 1, 6, d))
    modtab = jnp.concatenate([mod_lat, mod_ctx], axis=2)

    to_cm = functools.partial(_to_col_major, n_lat=l)
    to_rm = functools.partial(_to_row_major, n_lat=l)

    h3 = jnp.concatenate([x.astype(F32), ctx.astype(F32)], axis=1)
    for i in range(depth):
        j = i // 2
        mt = modtab[i]
        if i % 2 == 0:
            ys = []
            for dr, rev in ((0, False), (1, True)):
                wb, wc, a_re, a_im = _s5_prepare(s5_lam_re[j, dr], s5_lam_im[j, dr], s5_log_step[j, dr],
                                                 s5_b_re[j, dr], s5_b_im[j, dr], s5_c_re[j, dr], s5_c_im[j, dr], bn)
                ys.append(_s5_scan(h3, mt, wb, wc, a_re, a_im, rev=rev, n_lat=l).reshape(bn * ltot, d))
            h2d = h3.reshape(bn * ltot, d)
            outs = _post_mixer("s5", h2d, mt, ys[0], ys[1], (s5_d[j],), s5_w_glu[j].astype(BF16),
                               ln1_g[i], ln1_b[i], moe_w_router[i], moe_b_router[i], alpha=alpha, **tile_kw)
        else:
            h3 = to_cm(h3)
            h2d = h3.reshape(bn * ltot, d)
            w_in = gla_w_in[j]
            n_main = w_in.shape[1] - 2 * GLA_GATE_RANK
            w_a = w_in[:, n_main:].reshape(d, 2, GLA_GATE_RANK).transpose(1, 0, 2).astype(BF16)
            q, k, v, g, la_f, la_b = _gla_proj(h2d, mt, w_in[:, :n_main].astype(BF16), w_a,
                                               gla_w_a2[j].astype(BF16), gla_b_a2[j], **tile_kw)
            r3 = lambda a: a.reshape(bn, ltot, a.shape[-1])
            o_f, o_b = [o.reshape(bn * ltot, d) for o in _gla_rec(r3(q), r3(k), r3(v), r3(la_f), r3(la_b), n_lat=l)]
            outs = _post_mixer("gla", h2d, mt, o_f, o_b, (g, gla_norm_g[j]), gla_w_out[j].astype(BF16),
                               ln1_g[i], ln1_b[i], moe_w_router[i], moe_b_router[i], alpha=alpha, **tile_kw)
        h1, xr, ti, gt, rk, counts = outs
        h2 = _moe_layer(h1, xr, ti, gt, rk, counts, mt, ln2_g[i], ln2_b[i],
                        moe_w_up[i], moe_b_up[i], moe_w_down[i], moe_b_down[i], alpha=alpha, **tile_kw)
        h3 = h2.reshape(bn, ltot, d)
        if i % 2 == 1:
            h3 = to_rm(h3)
    return h3[:, :l].astype(x.dtype)
```

```python
import functools
import math

import jax
import jax.numpy as jnp
from jax import lax
from jax.experimental import pallas as pl
from jax.experimental.pallas import tpu as pltpu

F32 = jnp.float32
BF16 = jnp.bfloat16
I32 = jnp.int32
HIGHEST = lax.Precision.HIGHEST

GRID_W = 64
S5_GROUP = 16
S5_STATE = 64
GLA_HEADS = 4
GLA_GATE_RANK = 16
GLA_GATE_NORM = 16.0
GLA_CHUNK = 64
N_EXPERTS = 32
TOP_K = 4
SWIGLU_ALPHA = 1.702
SWIGLU_LIMIT = 7.0
LN_EPS = 1e-5

LANES = 128
SUBLANES = 8
MXU_K = 256
VMEM_LIMIT = 56 * 1024 * 1024

TOK_TILE = 256
S5_CHUNK = 128
S5_PITCH = S5_CHUNK + SUBLANES
MOE_ROWS = 512
FF_SLICE = 1024
ROUTE_LANES = LANES


def _sigmoid(x):
    return 1.0 / (1.0 + jnp.exp(-x))


def _cparams(sem):
    return pltpu.CompilerParams(dimension_semantics=sem, vmem_limit_bytes=VMEM_LIMIT)


def _ada_body(c_ref, w_ref, b_ref, o_ref):
    c = c_ref[...]
    cond = c * _sigmoid(c)
    o_ref[...] = jnp.dot(cond, w_ref[...], precision=HIGHEST, preferred_element_type=F32) + b_ref[...]


def _ada_table(cc, w_ada, b_ada):
    depth, d, six_d = w_ada.shape
    n_tiles = six_d // d
    return pl.pallas_call(
        _ada_body,
        grid=(depth, n_tiles),
        in_specs=[
            pl.BlockSpec((SUBLANES, d), lambda i, n: (0, 0)),
            pl.BlockSpec((None, d, d), lambda i, n: (i, 0, n)),
            pl.BlockSpec((None, 1, d), lambda i, n: (i, 0, n)),
        ],
        out_specs=pl.BlockSpec((None, SUBLANES, d), lambda i, n: (i, 0, n)),
        out_shape=jax.ShapeDtypeStruct((depth, SUBLANES, six_d), F32),
        compiler_params=_cparams(("arbitrary", "arbitrary")),
        name="ada_table",
    )(cc, w_ada, b_ada.reshape(depth, 1, six_d))


def _s5_scan_body(x_ref, mod_ref, wb_ref, wc_ref, are_ref, aim_ref, y_ref,
                  sre_ref, sim_ref, st_re, st_im, *, rev, tc, pitch, nb, n_kt, cpk, pk):
    j = pl.program_id(0)
    ncol = n_kt * cpk
    npc = ncol // pk
    half = cpk * LANES

    def seg_rows(c, b):
        r0 = ((c // npc) * nb + b) * pitch
        return c % npc, slice(r0, r0 + tc)

    @pl.when(j == 0)
    def _():
        st_re[...] = jnp.zeros_like(st_re)
        st_im[...] = jnp.zeros_like(st_im)

    u = jnp.concatenate(
        [(x_ref[b] * (1.0 + mod_ref[b, 1:2, :]) + mod_ref[b, 0:1, :]).astype(BF16) for b in range(nb)], axis=0)
    for kt in range(n_kt):
        r = jnp.dot(u[:, MXU_K * kt:MXU_K * (kt + 1)], wb_ref[kt], preferred_element_type=F32)
        for b in range(nb):
            rb = r[b * tc:(b + 1) * tc]
            for c in range(cpk):
                pc, rws = seg_rows(kt * cpk + c, b)
                sre_ref[pc, rws, :] = rb[:, LANES * c:LANES * (c + 1)]
                sim_ref[pc, rws, :] = rb[:, half + LANES * c:half + LANES * (c + 1)]

    grp = 8
    for cg in range(npc // grp):
        cols = list(range(cg * grp, (cg + 1) * grp))
        ar = [are_ref[c] for c in cols]
        ai = [aim_ref[c] for c in cols]
        init = tuple(st_re[c] for c in cols) + tuple(st_im[c] for c in cols)

        def step(t, carry, cols=cols, ar=ar, ai=ai):
            tt = (tc - 1 - t) if rev else t
            out_re, out_im = [], []
            for k, c in enumerate(cols):
                rows = pl.ds(tt, pk * nb, stride=pitch)
                pr, pi = carry[k], carry[grp + k]
                nr = ar[k] * pr - ai[k] * pi + sre_ref[c, rows, :]
                ni = ar[k] * pi + ai[k] * pr + sim_ref[c, rows, :]
                sre_ref[c, rows, :] = nr
                sim_ref[c, rows, :] = ni
                out_re.append(nr)
                out_im.append(ni)
            return tuple(out_re) + tuple(out_im)

        fin = lax.fori_loop(0, tc, step, init)
        for k, c in enumerate(cols):
            st_re[c] = fin[k]
            st_im[c] = fin[grp + k]

    def stacked(ref, kt):
        def piece(c, b):
            pc, rws = seg_rows(kt * cpk + c, b)
            return ref[pc, rws, :]

        return jnp.concatenate(
            [jnp.concatenate([piece(c, b) for c in range(cpk)], axis=-1) for b in range(nb)], axis=0).astype(BF16)

    for kt in range(n_kt):
        y = (jnp.dot(stacked(sre_ref, kt), wc_ref[kt, :half, :], preferred_element_type=F32)
             + jnp.dot(stacked(sim_ref, kt), wc_ref[kt, half:, :], preferred_element_type=F32))
        for b in range(nb):
            y_ref[b, :, MXU_K * kt:MXU_K * (kt + 1)] = y[b * tc:(b + 1) * tc]


def _s5_scan(h3, modtab, wb, wc, a_re, a_im, *, rev, n_lat):
    nb, ltot, d = h3.shape
    tc = S5_CHUNK
    n_chunks = ltot // tc
    lat_chunks = n_lat // tc
    n_kt, _, two_half = wb.shape
    cpk = two_half // (2 * LANES)
    ncol = n_kt * cpk
    pk = SUBLANES // nb
    npc = ncol // pk

    if rev:
        chunk = lambda j: n_chunks - 1 - j
    else:
        chunk = lambda j: (j + lat_chunks) % n_chunks
    is_ctx = lambda j: (chunk(j) >= lat_chunks).astype(I32)

    body = functools.partial(_s5_scan_body, rev=rev, tc=tc, pitch=S5_PITCH, nb=nb, n_kt=n_kt, cpk=cpk, pk=pk)
    return pl.pallas_call(
        body,
        grid=(n_chunks,),
        in_specs=[
            pl.BlockSpec((nb, tc, d), lambda j: (0, chunk(j), 0)),
            pl.BlockSpec((nb, None, 6, d), lambda j: (0, is_ctx(j), 0, 0)),
            pl.BlockSpec(wb.shape, lambda j: (0, 0, 0)),
            pl.BlockSpec(wc.shape, lambda j: (0, 0, 0)),
            pl.BlockSpec(a_re.shape, lambda j: (0, 0, 0)),
            pl.BlockSpec(a_im.shape, lambda j: (0, 0, 0)),
        ],
        out_specs=pl.BlockSpec((nb, tc, d), lambda j: (0, chunk(j), 0)),
        out_shape=jax.ShapeDtypeStruct((nb, ltot, d), F32),
        scratch_shapes=[
            pltpu.VMEM((npc, pk * nb * S5_PITCH, LANES), F32),
            pltpu.VMEM((npc, pk * nb * S5_PITCH, LANES), F32),
            pltpu.VMEM((npc, pk * nb, LANES), F32),
            pltpu.VMEM((npc, pk * nb, LANES), F32),
        ],
        compiler_params=_cparams(("arbitrary",)),
        name="s5_scan_bwd" if rev else "s5_scan_fwd",
    )(h3, modtab, wb, wc, a_re, a_im)


def _s5_prepare(lam_re, lam_im, log_step, b_re, b_im, c_re, c_im, nb):
    g, p = lam_re.shape
    hg = b_re.shape[-1]
    lr = lam_re.astype(F32)
    li = lam_im.astype(F32)
    dt = jnp.exp(log_step.astype(F32))[:, None]
    mag = jnp.exp(lr * dt)
    ar = mag * jnp.cos(li * dt)
    ai = mag * jnp.sin(li * dt)
    den = lr * lr + li * li
    nr = ar - 1.0
    kr = (nr * lr + ai * li) / den
    ki = (ai * lr - nr * li) / den
    br = b_re.astype(F32)
    bi = b_im.astype(F32)
    bbr = kr[..., None] * br - ki[..., None] * bi
    bbi = kr[..., None] * bi + ki[..., None] * br
    gpk = MXU_K // hg
    n_kt = g // gpk
    eye = jnp.eye(gpk, dtype=F32)

    def block_diag(t):
        full = t[:, :, :, None, :] * eye[None, :, None, :, None]
        return full.reshape(n_kt, gpk * t.shape[2], gpk * t.shape[3])

    def in_blocks(bb):
        return block_diag(bb.reshape(n_kt, gpk, p, hg).transpose(0, 1, 3, 2))

    def out_blocks(cc):
        return block_diag(cc.reshape(n_kt, gpk, hg, p).transpose(0, 1, 3, 2))

    wb = jnp.concatenate([in_blocks(bbr), in_blocks(bbi)], axis=-1).astype(BF16)
    wc = jnp.concatenate([out_blocks(c_re.astype(F32)), -out_blocks(c_im.astype(F32))], axis=1).astype(BF16)
    ncol = g * p // LANES
    pk = SUBLANES // nb
    npc = ncol // pk

    def packed_rows(a):
        t = a.reshape(pk, npc, 1, LANES).transpose(1, 0, 2, 3)
        return jnp.broadcast_to(t, (npc, pk, nb, LANES)).reshape(npc, pk * nb, LANES)

    return wb, wc, packed_rows(ar), packed_rows(ai)


def _gla_proj_body(x_ref, mod_ref, w_ref, wa_ref, wa2_ref, ba2_ref,
                   q_ref, k_ref, v_ref, g_ref, laf_ref, lab_ref, *, qk_w, v_w, dk):
    u = (x_ref[...] * (1.0 + mod_ref[1:2, :]) + mod_ref[0:1, :]).astype(BF16)
    p = jnp.dot(u, w_ref[...], preferred_element_type=F32)
    q_ref[...] = p[:, :qk_w] * (dk ** -0.5)
    k_ref[...] = p[:, qk_w:2 * qk_w]
    v_ref[...] = p[:, 2 * qk_w:2 * qk_w + v_w]
    g_ref[...] = p[:, 2 * qk_w + v_w:]
    for dr, out in enumerate((laf_ref, lab_ref)):
        a_d = jnp.dot(u, wa_ref[dr], preferred_element_type=F32).astype(BF16)
        z = jnp.dot(a_d, wa2_ref[dr], preferred_element_type=F32) + ba2_ref[dr]
        out[...] = (jnp.minimum(z, 0.0) - jnp.log1p(jnp.exp(-jnp.abs(z)))) / GLA_GATE_NORM


def _gla_proj(h2d, modtab, w_main, w_a, w_a2, b_a2, *, tiles_per_batch, ctx_tile0):
    t, d = h2d.shape
    tt = TOK_TILE
    qk_w = w_a2.shape[-1]
    v_w = (w_main.shape[1] - 2 * qk_w) // 2
    dk = qk_w // GLA_HEADS
    mod_map = lambda i: (i // tiles_per_batch, ((i % tiles_per_batch) >= ctx_tile0).astype(I32), 0, 0)
    row = lambda w: pl.BlockSpec((tt, w), lambda i: (i, 0))
    full = lambda a: pl.BlockSpec(a.shape, lambda i: (0,) * a.ndim)
    body = functools.partial(_gla_proj_body, qk_w=qk_w, v_w=v_w, dk=dk)
    b_a2r = b_a2.reshape(2, 1, qk_w).astype(F32)
    return pl.pallas_call(
        body,
        grid=(t // tt,),
        in_specs=[row(d), pl.BlockSpec((None, None, 6, d), mod_map),
                  full(w_main), full(w_a), full(w_a2), full(b_a2r)],
        out_specs=[row(qk_w), row(qk_w), row(v_w), row(v_w), row(qk_w), row(qk_w)],
        out_shape=[jax.ShapeDtypeStruct((t, w), F32) for w in (qk_w, qk_w, v_w, v_w, qk_w, qk_w)],
        compiler_params=_cparams(("parallel",)),
        name="gla_proj",
    )(h2d, modtab, w_main, w_a, w_a2, b_a2r)


def _gla_rec_body(qf_ref, kf_ref, vf_ref, laf_ref, qb_ref, kb_ref, vb_ref, lab_ref, of_ref, ob_ref, s_ref,
                  *, ch, dk, dv, nh, nb):
    @pl.when(pl.program_id(0) == 0)
    def _():
        s_ref[...] = jnp.zeros_like(s_ref)

    _gla_chunk(qf_ref, kf_ref, vf_ref, laf_ref, of_ref, s_ref, 0, rev=False, ch=ch, dk=dk, dv=dv, nh=nh, nb=nb)
    _gla_chunk(qb_ref, kb_ref, vb_ref, lab_ref, ob_ref, s_ref, nb * nh, rev=True, ch=ch, dk=dk, dv=dv, nh=nh, nb=nb)


def _gla_chunk(q_ref, k_ref, v_ref, la_ref, o_ref, s_ref, s0, *, rev, ch, dk, dv, nh, nb):
    r_i = lax.broadcasted_iota(I32, (ch, ch), 0)
    c_i = lax.broadcasted_iota(I32, (ch, ch), 1)
    seen = (c_i >= r_i) if rev else (c_i <= r_i)
    seen_f = seen.astype(F32)
    end = 0 if rev else ch - 1
    for bi in range(nb):
        b = jnp.dot(seen_f, la_ref[bi], precision=HIGHEST, preferred_element_type=F32)
        b_end = b[end:end + 1, :]
        k_all = k_ref[bi]
        q_d_all = (q_ref[bi] * jnp.exp(b)).astype(BF16)
        k_d_all = (k_all * jnp.exp(-b)).astype(BF16)
        k_e_all = (k_all * jnp.exp(b_end - b)).astype(BF16)
        g_cols = jnp.transpose(jnp.broadcast_to(jnp.exp(b_end), (SUBLANES, nh * dk)))[:, 0:1]
        for hd in range(nh):
            qs = slice(hd * dk, (hd + 1) * dk)
            vs = slice(hd * dv, (hd + 1) * dv)
            q_d, k_d, k_e = q_d_all[:, qs], k_d_all[:, qs], k_e_all[:, qs]
            v = v_ref[bi, :, vs].astype(BF16)
            att = lax.dot_general(q_d, k_d, (((1,), (1,)), ((), ())), preferred_element_type=F32)
            att = jnp.where(seen, att, 0.0).astype(BF16)
            s_prev = s_ref[s0 + bi * nh + hd]
            o = (jnp.dot(att, v, preferred_element_type=F32)
                 + jnp.dot(q_d, s_prev.astype(BF16), preferred_element_type=F32))
            o_ref[bi, :, vs] = o
            upd = lax.dot_general(k_e, v, (((0,), (0,)), ((), ())), preferred_element_type=F32)
            s_ref[s0 + bi * nh + hd] = s_prev * g_cols[qs, :] + upd


def _gla_rec(q, k, v, la_f, la_b, *, n_lat):
    nb, ltot, qk_w = q.shape
    v_w = v.shape[-1]
    ch = GLA_CHUNK
    n_chunks = ltot // ch
    lat_chunks = n_lat // ch
    fwd = lambda j: (j + lat_chunks) % n_chunks
    bwd = lambda j: n_chunks - 1 - j
    spec = lambda w, chunk: pl.BlockSpec((nb, ch, w), lambda j: (0, chunk(j), 0))
    dk, dv = qk_w // GLA_HEADS, v_w // GLA_HEADS
    body = functools.partial(_gla_rec_body, ch=ch, dk=dk, dv=dv, nh=GLA_HEADS, nb=nb)
    out = jax.ShapeDtypeStruct((nb, ltot, v_w), F32)
    return pl.pallas_call(
        body,
        grid=(n_chunks,),
        in_specs=[spec(qk_w, fwd), spec(qk_w, fwd), spec(v_w, fwd), spec(qk_w, fwd),
                  spec(qk_w, bwd), spec(qk_w, bwd), spec(v_w, bwd), spec(qk_w, bwd)],
        out_specs=[spec(v_w, fwd), spec(v_w, bwd)],
        out_shape=[out, out],
        scratch_shapes=[pltpu.VMEM((2 * nb * GLA_HEADS, dk, dv), F32)],
        compiler_params=_cparams(("arbitrary",)),
        name="gla_rec",
    )(q, k, v, la_f, q, k, v, la_b)


def _layer_norm_rows(v, g, b):
    mu = jnp.mean(v, axis=-1, keepdims=True)
    c = v - mu
    var = jnp.mean(c * c, axis=-1, keepdims=True)
    return c * lax.rsqrt(var + LN_EPS) * g + b


def _post_mixer_body(*refs, kind, alpha, tt, dv):
    if kind == "s5":
        (h_ref, mod_ref, ya_ref, yb_ref, dsk_ref, w_ref, lng_ref, lnb_ref, wr_ref, br_ref,
         h1_ref, xr_ref, meta_ref, gt_ref, cnt_ref, base_ref) = refs
    else:
        (h_ref, mod_ref, ya_ref, yb_ref, gate_ref, ng_ref, w_ref, lng_ref, lnb_ref, wr_ref, br_ref,
         h1_ref, xr_ref, meta_ref, gt_ref, cnt_ref, base_ref) = refs
    i = pl.program_id(0)

    @pl.when(i == 0)
    def _():
        base_ref[...] = jnp.zeros_like(base_ref)

    h = h_ref[...]
    d = h.shape[-1]
    if kind == "s5":
        u = h * (1.0 + mod_ref[1:2, :]) + mod_ref[0:1, :]
        y = dsk_ref[...] * u + ya_ref[...] + yb_ref[...]
        ge = 0.5 * y * (1.0 + jnp.tanh(math.sqrt(2.0 / math.pi) * (y + 0.044715 * (y * y * y))))
        z = jnp.dot(ge.astype(BF16), w_ref[...], preferred_element_type=F32)
        mix = z[:, :d] * _sigmoid(z[:, d:])
    else:
        o = ya_ref[...] + yb_ref[...]
        parts = []
        for hd in range(d // dv):
            oh = o[:, hd * dv:(hd + 1) * dv]
            ms = jnp.mean(oh * oh, axis=-1, keepdims=True)
            parts.append(oh * lax.rsqrt(ms + LN_EPS))
        on = jnp.concatenate(parts, axis=-1) * ng_ref[...]
        gv = gate_ref[...]
        a = on * (gv * _sigmoid(gv))
        mix = jnp.dot(a.astype(BF16), w_ref[...], preferred_element_type=F32)

    h1 = _layer_norm_rows(alpha * h + mod_ref[2:3, :] * mix, lng_ref[...], lnb_ref[...])
    h1_ref[...] = h1
    u2 = h1 * (1.0 + mod_ref[4:5, :]) + mod_ref[3:4, :]
    nc = d // LANES
    for c in range(nc):
        xr_ref[pl.ds(c, tt, stride=nc), :] = u2[:, c * LANES:(c + 1) * LANES]

    logits = lax.dot_general(wr_ref[...], u2, (((1,), (1,)), ((), ())), precision=HIGHEST,
                             preferred_element_type=F32) + br_ref[...]
    n_e = logits.shape[0]
    erow = lax.broadcasted_iota(I32, logits.shape, 0)
    work = logits
    vals, idxs = [], []
    for _ in range(TOP_K):
        m = jnp.max(work, axis=0, keepdims=True)
        idx = jnp.min(jnp.where(work == m, erow, n_e), axis=0, keepdims=True)
        vals.append(m)
        idxs.append(idx)
        work = jnp.where(erow == idx, -jnp.inf, work)
    exps = [jnp.exp(v - vals[0]) for v in vals]
    den = exps[0]
    for e in exps[1:]:
        den = den + e

    multi = jnp.zeros(logits.shape, F32)
    for idx in idxs:
        multi = multi + (erow == idx).astype(F32)
    r_i = lax.broadcasted_iota(I32, (tt, tt), 0)
    c_i = lax.broadcasted_iota(I32, (tt, tt), 1)
    earlier = (r_i < c_i).astype(BF16)
    pos = jnp.dot(multi.astype(BF16), earlier, preferred_element_type=F32) + base_ref[:, 0:1]
    row8 = lax.broadcasted_iota(I32, (SUBLANES, tt), 0)
    meta = jnp.zeros((SUBLANES, tt), I32)
    gts = jnp.zeros((SUBLANES, tt), F32)
    for kk in range(TOP_K):
        rank_k = jnp.sum(jnp.where(erow == idxs[kk], pos, 0.0), axis=0, keepdims=True)
        meta = jnp.where(row8 == kk, idxs[kk], meta)
        meta = jnp.where(row8 == TOP_K + kk, rank_k.astype(I32), meta)
        gts = jnp.where(row8 == kk, exps[kk] / den, gts)
    meta_ref[...] = meta
    gt_ref[...] = jnp.transpose(
        jnp.concatenate([gts, jnp.zeros((ROUTE_LANES - SUBLANES, tt), F32)], axis=0))
    new_base = base_ref[:, 0:1] + jnp.sum(multi, axis=1, keepdims=True)
    base_ref[...] = jnp.broadcast_to(new_base, base_ref.shape)
    cnt_ref[...] = jnp.broadcast_to(new_base, cnt_ref.shape)


def _post_mixer(kind, h2d, modtab, ya, yb, extra, w_mix, ln_g, ln_b, w_router, b_router, *,
                alpha, tiles_per_batch, ctx_tile0):
    t, d = h2d.shape
    tt = TOK_TILE
    n_e = w_router.shape[-1]
    wr = w_router.astype(F32).T
    br = b_router.astype(F32).reshape(n_e, 1)
    mod_map = lambda i: (i // tiles_per_batch, ((i % tiles_per_batch) >= ctx_tile0).astype(I32), 0, 0)
    row = lambda w: pl.BlockSpec((tt, w), lambda i: (i, 0))
    full = lambda a: pl.BlockSpec(a.shape, lambda i: (0,) * a.ndim)
    vec = lambda a: a.reshape(1, -1).astype(F32)
    if kind == "s5":
        (d_skip,) = extra
        ins = [h2d, modtab, ya, yb, vec(d_skip), w_mix, vec(ln_g), vec(ln_b), wr, br]
        in_specs = [row(d), pl.BlockSpec((None, None, 6, d), mod_map), row(d), row(d)]
        in_specs += [full(a) for a in ins[4:]]
        dv = d
    else:
        gate, norm_g = extra
        dv = norm_g.shape[-1]
        ng = jnp.tile(norm_g.astype(F32), d // dv).reshape(1, d)
        ins = [h2d, modtab, ya, yb, gate, ng, w_mix, vec(ln_g), vec(ln_b), wr, br]
        in_specs = [row(d), pl.BlockSpec((None, None, 6, d), mod_map), row(d), row(d), row(d)]
        in_specs += [full(a) for a in ins[5:]]
    body = functools.partial(_post_mixer_body, kind=kind, alpha=alpha, tt=tt, dv=dv)
    return pl.pallas_call(
        body,
        grid=(t // tt,),
        in_specs=in_specs,
        out_specs=[
            row(d),
            pl.BlockSpec((tt * (d // LANES), LANES), lambda i: (i, 0)),
            pl.BlockSpec((SUBLANES, tt), lambda i: (0, i)),
            row(ROUTE_LANES),
            pl.BlockSpec((n_e, ROUTE_LANES), lambda i: (0, 0)),
        ],
        out_shape=[
            jax.ShapeDtypeStruct((t, d), F32),
            jax.ShapeDtypeStruct((t * (d // LANES), LANES), F32),
            jax.ShapeDtypeStruct((SUBLANES, t), I32),
            jax.ShapeDtypeStruct((t, ROUTE_LANES), F32),
            jax.ShapeDtypeStruct((n_e, ROUTE_LANES), F32),
        ],
        scratch_shapes=[pltpu.VMEM((n_e, ROUTE_LANES), F32)],
        compiler_params=_cparams(("arbitrary",)),
        name="post_mixer_" + kind,
    )(*ins)


def _row_copy_wait(src_hbm, dst, sem, n_rows):
    pltpu.make_async_copy(src_hbm.at[pl.ds(0, n_rows), :], dst.at[pl.ds(0, n_rows), :], sem).wait()


def _tok_rows(t, nc):
    return pl.ds(pl.multiple_of(t * nc, nc), nc)


def _dispatch_body(ti_ref, rk_ref, ps_ref, zs_ref, x_ref, xp_hbm, dst_ref, zbuf, sem, zsem, *, tt, nc, n_e):
    zr = zbuf.shape[0]

    @pl.when(pl.program_id(0) == 0)
    def _():
        zbuf[...] = jnp.zeros_like(zbuf)
        for e in range(2 * n_e):
            @pl.when(zs_ref[e] >= 0)
            def _():
                pltpu.make_async_copy(zbuf, xp_hbm.at[pl.ds(pl.multiple_of(zs_ref[e] * nc, nc), zr), :], zsem).start()
        for e in range(2 * n_e):
            @pl.when(zs_ref[e] >= 0)
            def _():
                pltpu.make_async_copy(zbuf, xp_hbm.at[pl.ds(pl.multiple_of(zs_ref[e] * nc, nc), zr), :], zsem).wait()

    def issue(t, c):
        for kk in range(TOP_K):
            s = t * TOP_K + kk
            dst = ps_ref[ti_ref[s]] + rk_ref[s]
            dst_ref[s] = dst
            pltpu.make_async_copy(x_ref.at[_tok_rows(t, nc), :], xp_hbm.at[_tok_rows(dst, nc), :],
                                  sem).start(priority=kk % 2)
        return c

    lax.fori_loop(0, tt, issue, 0, unroll=4)
    _row_copy_wait(xp_hbm, xp_hbm, sem, tt * TOP_K * nc)


def _dispatch(ti_flat, rk_flat, pstart, zstart, xr, n_rows, nc):
    t = xr.shape[0] // nc
    tt = TOK_TILE
    smem_blk = pl.BlockSpec((tt * TOP_K,), lambda i: (i,), memory_space=pltpu.SMEM)
    return pl.pallas_call(
        functools.partial(_dispatch_body, tt=tt, nc=nc, n_e=N_EXPERTS),
        grid=(t // tt,),
        in_specs=[smem_blk, smem_blk,
                  pl.BlockSpec(memory_space=pltpu.SMEM),
                  pl.BlockSpec(memory_space=pltpu.SMEM),
                  pl.BlockSpec((tt * nc, LANES), lambda i: (i, 0))],
        out_specs=[pl.BlockSpec(memory_space=pl.ANY), smem_blk],
        out_shape=[jax.ShapeDtypeStruct((n_rows * nc, LANES), xr.dtype),
                   jax.ShapeDtypeStruct(ti_flat.shape, I32)],
        scratch_shapes=[pltpu.VMEM((MOE_ROWS * nc, LANES), xr.dtype),
                        pltpu.SemaphoreType.DMA, pltpu.SemaphoreType.DMA],
        compiler_params=_cparams(("arbitrary",)),
        name="moe_dispatch",
    )(ti_flat, rk_flat, pstart, zstart, xr)


def _expert_body(be_ref, nu_ref, x_ref, wu_ref, bu_ref, wd_ref, bd_ref, y_ref, wub, wdb):
    i = pl.program_id(0)
    e = be_ref[i]
    prev = be_ref[jnp.maximum(i - 1, 0)]

    @pl.when((i == 0) | (e != prev))
    def _():
        wub[...] = wu_ref[...].astype(BF16)
        wdb[...] = wd_ref[...].astype(BF16)

    nc = wu_ref.shape[0] // LANES
    r = x_ref.shape[0] // nc

    @pl.when(i < nu_ref[0])
    def _():
        x = jnp.concatenate([x_ref[pl.ds(c, r, stride=nc), :] for c in range(nc)], axis=-1).astype(BF16)
        ff = wdb.shape[0]
        y = bd_ref[...]
        for j in range(ff // FF_SLICE):
            lo, hi = j * FF_SLICE, (j + 1) * FF_SLICE
            h_glu = jnp.dot(x, wub[:, lo:hi], preferred_element_type=F32) + bu_ref[:, lo:hi]
            h_lin = jnp.dot(x, wub[:, ff + lo:ff + hi], preferred_element_type=F32) + bu_ref[:, ff + lo:ff + hi]
            h_glu = jnp.minimum(h_glu, SWIGLU_LIMIT)
            h_lin = jnp.clip(h_lin, -SWIGLU_LIMIT, SWIGLU_LIMIT)
            a = h_glu * _sigmoid(SWIGLU_ALPHA * h_glu) * (h_lin + 1.0)
            y = y + jnp.dot(a.astype(BF16), wdb[lo:hi, :], preferred_element_type=F32)
        for c in range(nc):
            y_ref[pl.ds(c, r, stride=nc), :] = y[:, c * LANES:(c + 1) * LANES]

    @pl.when(i >= nu_ref[0])
    def _():
        y_ref[...] = jnp.zeros_like(y_ref)


def _experts(blk_e, n_used, xp, layer, w_up, b_up, w_down, b_down):
    _, n_e, d, two_ff = w_up.shape
    nc = d // LANES
    n_rows = xp.shape[0] // nc
    r = MOE_ROWS
    ff = two_ff // 2
    grid_spec = pltpu.PrefetchScalarGridSpec(
        num_scalar_prefetch=2,
        grid=(n_rows // r,),
        in_specs=[
            pl.BlockSpec((r * nc, LANES), lambda i, be, nu: (jnp.minimum(i, nu[0] - 1), 0)),
            pl.BlockSpec((None, None, d, two_ff), lambda i, be, nu: (layer, be[i], 0, 0)),
            pl.BlockSpec((None, None, 1, two_ff), lambda i, be, nu: (layer, be[i], 0, 0)),
            pl.BlockSpec((None, None, ff, d), lambda i, be, nu: (layer, be[i], 0, 0)),
            pl.BlockSpec((None, None, 1, d), lambda i, be, nu: (layer, be[i], 0, 0)),
        ],
        out_specs=pl.BlockSpec((r * nc, LANES), lambda i, be, nu: (i, 0)),
        scratch_shapes=[pltpu.VMEM((d, two_ff), BF16), pltpu.VMEM((ff, d), BF16)],
    )
    return pl.pallas_call(
        _expert_body,
        grid_spec=grid_spec,
        out_shape=jax.ShapeDtypeStruct(xp.shape, F32),
        compiler_params=_cparams(("arbitrary",)),
        name="moe_experts",
    )(blk_e, n_used, xp, w_up, b_up.reshape(-1, n_e, 1, two_ff), w_down, b_down.reshape(-1, n_e, 1, d))


def _combine_body(dst_ref, gt_ref, h_ref, mod_ref, lng_ref, lnb_ref, yp_hbm,
                  o_ref, buf, sems, *, tt, nc, n_tiles, alpha):
    s = pl.program_id(0)
    slot = s % 2

    @pl.when(s < n_tiles)
    def _():
        def issue(t, c):
            for kk in range(TOP_K):
                q = t * TOP_K + kk
                src = dst_ref[q]
                pltpu.make_async_copy(yp_hbm.at[_tok_rows(src, nc), :], buf.at[slot, kk, _tok_rows(t, nc), :],
                                      sems.at[slot]).start(priority=kk % 2)
            return c

        lax.fori_loop(0, tt, issue, 0, unroll=4)

    @pl.when(s > 0)
    def _():
        prev = 1 - slot
        for kk in range(TOP_K):
            _row_copy_wait(yp_hbm, buf.at[prev, kk], sems.at[prev], tt * nc)
        gt = gt_ref[...]
        cols = []
        for c in range(nc):
            acc = gt[:, 0:1] * buf[prev, 0, pl.ds(c, tt, stride=nc), :]
            for kk in range(1, TOP_K):
                acc = acc + gt[:, kk:kk + 1] * buf[prev, kk, pl.ds(c, tt, stride=nc), :]
            cols.append(acc)
        f = jnp.concatenate(cols, axis=-1)
        o_ref[...] = _layer_norm_rows(alpha * h_ref[...] + mod_ref[5:6, :] * f, lng_ref[...], lnb_ref[...])


def _combine(dest, gates, h1, modtab, ln_g, ln_b, yp, *, alpha, tiles_per_batch, ctx_tile0):
    t, d = h1.shape
    tt = TOK_TILE
    nc = d // LANES
    n_tiles = t // tt
    nxt = lambda s: jnp.minimum(s, n_tiles - 1)
    cur = lambda s: jnp.maximum(s - 1, 0)
    smem_blk = pl.BlockSpec((tt * TOP_K,), lambda s: (nxt(s),), memory_space=pltpu.SMEM)
    mod_map = lambda s: (cur(s) // tiles_per_batch, ((cur(s) % tiles_per_batch) >= ctx_tile0).astype(I32), 0, 0)
    vec = lambda a: a.reshape(1, -1).astype(F32)
    return pl.pallas_call(
        functools.partial(_combine_body, tt=tt, nc=nc, n_tiles=n_tiles, alpha=alpha),
        grid=(n_tiles + 1,),
        in_specs=[smem_blk,
                  pl.BlockSpec((tt, ROUTE_LANES), lambda s: (cur(s), 0)),
                  pl.BlockSpec((tt, d), lambda s: (cur(s), 0)),
                  pl.BlockSpec((None, None, 6, d), mod_map),
                  pl.BlockSpec((1, d), lambda s: (0, 0)),
                  pl.BlockSpec((1, d), lambda s: (0, 0)),
                  pl.BlockSpec(memory_space=pl.ANY)],
        out_specs=pl.BlockSpec((tt, d), lambda s: (cur(s), 0)),
        out_shape=jax.ShapeDtypeStruct((t, d), F32),
        scratch_shapes=[pltpu.VMEM((2, TOP_K, tt * nc, LANES), F32), pltpu.SemaphoreType.DMA((2,))],
        compiler_params=_cparams(("arbitrary",)),
        name="moe_combine",
    )(dest, gates, h1, modtab, vec(ln_g), vec(ln_b), yp)


def _moe_layer(h1, xr, meta, gt, counts, modtab, ln_g, ln_b, layer, w_up, b_up, w_down, b_down, *,
               alpha, tiles_per_batch, ctx_tile0):
    t, d = h1.shape
    n_e = w_up.shape[1]
    r = MOE_ROWS
    n_blocks = -(-(t * TOP_K) // r) + n_e
    cnt = counts[:, 0].astype(I32)
    padded = (cnt + r - 1) // r * r
    pends = jnp.cumsum(padded)
    pstart = jnp.zeros((ROUTE_LANES,), I32).at[:n_e].set(pends - padded)
    tail = pends[-1] + jnp.arange(n_e, dtype=I32) * r
    zstart = jnp.concatenate([jnp.where(cnt > 0, pends - r, -1),
                              jnp.where(tail < n_blocks * r, tail, -1)]).astype(I32)
    blk_row0 = jnp.arange(n_blocks, dtype=I32) * r
    blk_e = jnp.minimum(jnp.sum((pends[None, :] <= blk_row0[:, None]).astype(I32), axis=1), n_e - 1)
    n_used = (pends[-1] // r).astype(I32).reshape(1)
    ti_flat = meta[:TOP_K].T.reshape(-1)
    rk_flat = meta[TOP_K:2 * TOP_K].T.reshape(-1)
    xp, dest = _dispatch(ti_flat, rk_flat, pstart, zstart, xr, n_blocks * r, d // LANES)
    yp = _experts(blk_e, n_used, xp, layer, w_up, b_up, w_down, b_down)
    return _combine(dest, gt, h1, modtab, ln_g, ln_b, yp,
                    alpha=alpha, tiles_per_batch=tiles_per_batch, ctx_tile0=ctx_tile0)


def _to_cm_body(lat_ref, ctx_ref, o_ref, *, rows, k, n_lat_steps):
    w = pl.program_id(1)
    for q in range(SUBLANES // k):
        @pl.when((w < n_lat_steps) & (w % (SUBLANES // k) == q))
        def _():
            for i in range(k):
                o_ref[i * rows:(i + 1) * rows, :] = lat_ref[:, q * k + i, :]

    @pl.when(w >= n_lat_steps)
    def _():
        o_ref[...] = ctx_ref[...]


def _to_col_major(h3, n_lat):
    bn, ltot, d = h3.shape
    lc = ltot - n_lat
    rows = n_lat // GRID_W
    k = max(kk for kk in (1, 2, 4, 8) if lc % (kk * rows) == 0)
    ob = k * rows
    n_lat_steps = GRID_W // k
    per_blk = SUBLANES // k
    return pl.pallas_call(
        functools.partial(_to_cm_body, rows=rows, k=k, n_lat_steps=n_lat_steps),
        grid=(bn, n_lat_steps + lc // ob),
        in_specs=[pl.BlockSpec((None, rows, SUBLANES, d),
                               lambda b, w: (b, 0, jnp.minimum(w, n_lat_steps - 1) // per_blk, 0)),
                  pl.BlockSpec((None, ob, d), lambda b, w: (b, jnp.maximum(w, n_lat_steps), 0))],
        out_specs=pl.BlockSpec((None, ob, d), lambda b, w: (b, w, 0)),
        out_shape=jax.ShapeDtypeStruct(h3.shape, h3.dtype),
        compiler_params=_cparams(("arbitrary", "arbitrary")),
        name="to_col_major",
    )(h3.reshape(bn, ltot // GRID_W, GRID_W, d), h3)


def _to_rm_body(lat_ref, ctx_ref, o_ref, *, rows, n_ctx_rows):
    w = pl.program_id(1)
    for i in range(SUBLANES):
        o_ref[0:rows, i, :] = lat_ref[i * rows:(i + 1) * rows, :]
    for m in range(n_ctx_rows):
        o_ref[rows + m, :, :] = ctx_ref[pl.ds(pl.multiple_of(m * GRID_W + w * SUBLANES, SUBLANES), SUBLANES), :]


def _to_row_major(h3, n_lat):
    bn, ltot, d = h3.shape
    lc = ltot - n_lat
    rows = n_lat // GRID_W
    assert lc % GRID_W == 0 and n_lat % lc == 0
    out = pl.pallas_call(
        functools.partial(_to_rm_body, rows=rows, n_ctx_rows=lc // GRID_W),
        grid=(bn, GRID_W // SUBLANES),
        in_specs=[pl.BlockSpec((None, SUBLANES * rows, d), lambda b, w: (b, w, 0)),
                  pl.BlockSpec((None, lc, d), lambda b, w: (b, n_lat // lc, 0))],
        out_specs=pl.BlockSpec((None, ltot // GRID_W, SUBLANES, d), lambda b, w: (b, 0, w, 0)),
        out_shape=jax.ShapeDtypeStruct((bn, ltot // GRID_W, GRID_W, d), h3.dtype),
        compiler_params=_cparams(("arbitrary", "arbitrary")),
        name="to_row_major",
    )(h3, h3)
    return out.reshape(bn, ltot, d)


def kernel(x, c, ctx, c_ctx, w_ada, b_ada, ln1_g, ln1_b, ln2_g, ln2_b, s5_lam_re, s5_lam_im, s5_log_step, s5_b_re, s5_b_im, s5_c_re, s5_c_im, s5_d, s5_w_glu, gla_w_in, gla_w_a2, gla_b_a2, gla_norm_g, gla_w_out, moe_w_router, moe_b_router, moe_w_up, moe_b_up, moe_w_down, moe_b_down):
    bn, l, d = x.shape
    lc = ctx.shape[1]
    depth = w_ada.shape[0]
    ltot = l + lc
    rows = l // GRID_W
    alpha = (2 * depth) ** 0.25
    assert l % TOK_TILE == 0 and lc % TOK_TILE == 0 and l % S5_CHUNK == 0 and lc % S5_CHUNK == 0
    assert bn < SUBLANES
    tiles_per_batch = ltot // TOK_TILE
    ctx_tile0 = l // TOK_TILE
    tile_kw = dict(tiles_per_batch=tiles_per_batch, ctx_tile0=ctx_tile0)

    cc = jnp.zeros((SUBLANES, d), F32).at[:bn].set(c.astype(F32)).at[bn].set(c_ctx.astype(F32))
    mod = _ada_table(cc, w_ada.astype(F32), b_ada.astype(F32))
    mod_lat = mod[:, :bn].reshape(depth, bn, 1, 6, d)
    mod_ctx = jnp.broadcast_to(mod[:, bn].reshape(depth, 1, 1, 6, d), (depth, bn, 1, 6, d))
    modtab = jnp.concatenate([mod_lat, mod_ctx], axis=2)

    to_cm = functools.partial(_to_col_major, n_lat=l)
    to_rm = functools.partial(_to_row_major, n_lat=l)

    h3 = jnp.concatenate([x.astype(F32), ctx.astype(F32)], axis=1)
    for i in range(depth):
        j = i // 2
        mt = modtab[i]
        if i % 2 == 0:
            ys = []
            for dr, rev in ((0, False), (1, True)):
                wb, wc, a_re, a_im = _s5_prepare(s5_lam_re[j, dr], s5_lam_im[j, dr], s5_log_step[j, dr],
                                                 s5_b_re[j, dr], s5_b_im[j, dr], s5_c_re[j, dr], s5_c_im[j, dr], bn)
                ys.append(_s5_scan(h3, mt, wb, wc, a_re, a_im, rev=rev, n_lat=l).reshape(bn * ltot, d))
            h2d = h3.reshape(bn * ltot, d)
            outs = _post_mixer("s5", h2d, mt, ys[0], ys[1], (s5_d[j],), s5_w_glu[j].astype(BF16),
                               ln1_g[i], ln1_b[i], moe_w_router[i], moe_b_router[i], alpha=alpha, **tile_kw)
        else:
            h3 = to_cm(h3)
            h2d = h3.reshape(bn * ltot, d)
            w_in = gla_w_in[j]
            n_main = w_in.shape[1] - 2 * GLA_GATE_RANK
            w_a = w_in[:, n_main:].reshape(d, 2, GLA_GATE_RANK).transpose(1, 0, 2).astype(BF16)
            q, k, v, g, la_f, la_b = _gla_proj(h2d, mt, w_in[:, :n_main].astype(BF16), w_a,
                                               gla_w_a2[j].astype(BF16), gla_b_a2[j], **tile_kw)
            r3 = lambda a: a.reshape(bn, ltot, a.shape[-1])
            o_f, o_b = [o.reshape(bn * ltot, d) for o in _gla_rec(r3(q), r3(k), r3(v), r3(la_f), r3(la_b), n_lat=l)]
            outs = _post_mixer("gla", h2d, mt, o_f, o_b, (g, gla_norm_g[j]), gla_w_out[j].astype(BF16),
                               ln1_g[i], ln1_b[i], moe_w_router[i], moe_b_router[i], alpha=alpha, **tile_kw)
        h1, xr, meta, gt, counts = outs
        h2 = _moe_layer(h1, xr, meta, gt, counts, mt, ln2_g[i], ln2_b[i],
                        i, moe_w_up, moe_b_up, moe_w_down, moe_b_down, alpha=alpha, **tile_kw)
        h3 = h2.reshape(bn, ltot, d)
        if i % 2 == 1:
            h3 = to_rm(h3)
    return h3[:, :l].astype(x.dtype)
```

```python
import functools
import math

import jax
import jax.numpy as jnp
from jax import lax
from jax.experimental import pallas as pl
from jax.experimental.pallas import tpu as pltpu

F32 = jnp.float32
BF16 = jnp.bfloat16
I32 = jnp.int32
HIGHEST = lax.Precision.HIGHEST

GRID_W = 64
S5_GROUP = 16
S5_STATE = 64
GLA_HEADS = 4
GLA_GATE_RANK = 16
GLA_GATE_NORM = 16.0
GLA_CHUNK = 64
GLA_STEP_CHUNKS = 2
N_EXPERTS = 32
TOP_K = 4
SWIGLU_ALPHA = 1.702
SWIGLU_LIMIT = 7.0
LN_EPS = 1e-5

LANES = 128
SUBLANES = 8
MXU_K = 256
VMEM_LIMIT = 56 * 1024 * 1024

TOK_TILE = 256
S5_CHUNK = 128
S5_PITCH = S5_CHUNK + SUBLANES
MOE_ROWS = 512
FF_SLICE = 1024
ROUTE_LANES = LANES


def _sigmoid(x):
    return 1.0 / (1.0 + jnp.exp(-x))


def _cparams(sem):
    return pltpu.CompilerParams(dimension_semantics=sem, vmem_limit_bytes=VMEM_LIMIT)


def _ada_body(c_ref, w_ref, b_ref, o_ref):
    c = c_ref[...]
    cond = c * _sigmoid(c)
    o_ref[...] = jnp.dot(cond, w_ref[...], precision=HIGHEST, preferred_element_type=F32) + b_ref[...]


def _ada_table(cc, w_ada, b_ada):
    depth, d, six_d = w_ada.shape
    n_tiles = six_d // d
    return pl.pallas_call(
        _ada_body,
        grid=(depth, n_tiles),
        in_specs=[
            pl.BlockSpec((SUBLANES, d), lambda i, n: (0, 0)),
            pl.BlockSpec((None, d, d), lambda i, n: (i, 0, n)),
            pl.BlockSpec((None, 1, d), lambda i, n: (i, 0, n)),
        ],
        out_specs=pl.BlockSpec((None, SUBLANES, d), lambda i, n: (i, 0, n)),
        out_shape=jax.ShapeDtypeStruct((depth, SUBLANES, six_d), F32),
        compiler_params=_cparams(("arbitrary", "arbitrary")),
        name="ada_table",
    )(cc, w_ada, b_ada.reshape(depth, 1, six_d))


def _s5_scan_body(x_ref, mod_ref, wb_ref, wc_ref, are_ref, aim_ref, y_ref,
                  sre_ref, sim_ref, st_re, st_im, *, rev, tc, pitch, nb, n_kt, cpk, pk):
    j = pl.program_id(0)
    ncol = n_kt * cpk
    npc = ncol // pk
    half = cpk * LANES

    def seg_rows(c, b):
        r0 = ((c // npc) * nb + b) * pitch
        return c % npc, slice(r0, r0 + tc)

    @pl.when(j == 0)
    def _():
        st_re[...] = jnp.zeros_like(st_re)
        st_im[...] = jnp.zeros_like(st_im)

    u = jnp.concatenate(
        [(x_ref[b] * (1.0 + mod_ref[b, 1:2, :]) + mod_ref[b, 0:1, :]).astype(BF16) for b in range(nb)], axis=0)
    for kt in range(n_kt):
        r = jnp.dot(u[:, MXU_K * kt:MXU_K * (kt + 1)], wb_ref[kt], preferred_element_type=F32)
        for b in range(nb):
            rb = r[b * tc:(b + 1) * tc]
            for c in range(cpk):
                pc, rws = seg_rows(kt * cpk + c, b)
                sre_ref[pc, rws, :] = rb[:, LANES * c:LANES * (c + 1)]
                sim_ref[pc, rws, :] = rb[:, half + LANES * c:half + LANES * (c + 1)]

    grp = 8
    for cg in range(npc // grp):
        cols = list(range(cg * grp, (cg + 1) * grp))
        ar = [are_ref[c] for c in cols]
        ai = [aim_ref[c] for c in cols]
        init = tuple(st_re[c] for c in cols) + tuple(st_im[c] for c in cols)

        def step(t, carry, cols=cols, ar=ar, ai=ai):
            tt = (tc - 1 - t) if rev else t
            out_re, out_im = [], []
            for k, c in enumerate(cols):
                rows = pl.ds(tt, pk * nb, stride=pitch)
                pr, pi = carry[k], carry[grp + k]
                nr = ar[k] * pr - ai[k] * pi + sre_ref[c, rows, :]
                ni = ar[k] * pi + ai[k] * pr + sim_ref[c, rows, :]
                sre_ref[c, rows, :] = nr
                sim_ref[c, rows, :] = ni
                out_re.append(nr)
                out_im.append(ni)
            return tuple(out_re) + tuple(out_im)

        fin = lax.fori_loop(0, tc, step, init)
        for k, c in enumerate(cols):
            st_re[c] = fin[k]
            st_im[c] = fin[grp + k]

    def stacked(ref, kt):
        def piece(c, b):
            pc, rws = seg_rows(kt * cpk + c, b)
            return ref[pc, rws, :]

        return jnp.concatenate(
            [jnp.concatenate([piece(c, b) for c in range(cpk)], axis=-1) for b in range(nb)], axis=0).astype(BF16)

    for kt in range(n_kt):
        y = (jnp.dot(stacked(sre_ref, kt), wc_ref[kt, :half, :], preferred_element_type=F32)
             + jnp.dot(stacked(sim_ref, kt), wc_ref[kt, half:, :], preferred_element_type=F32))
        for b in range(nb):
            y_ref[b, :, MXU_K * kt:MXU_K * (kt + 1)] = y[b * tc:(b + 1) * tc]


def _s5_scan(h3, modtab, wb, wc, a_re, a_im, *, rev, n_lat):
    nb, ltot, d = h3.shape
    tc = S5_CHUNK
    n_chunks = ltot // tc
    lat_chunks = n_lat // tc
    n_kt, _, two_half = wb.shape
    cpk = two_half // (2 * LANES)
    ncol = n_kt * cpk
    pk = SUBLANES // nb
    npc = ncol // pk

    if rev:
        chunk = lambda j: n_chunks - 1 - j
    else:
        chunk = lambda j: (j + lat_chunks) % n_chunks
    is_ctx = lambda j: (chunk(j) >= lat_chunks).astype(I32)

    body = functools.partial(_s5_scan_body, rev=rev, tc=tc, pitch=S5_PITCH, nb=nb, n_kt=n_kt, cpk=cpk, pk=pk)
    return pl.pallas_call(
        body,
        grid=(n_chunks,),
        in_specs=[
            pl.BlockSpec((nb, tc, d), lambda j: (0, chunk(j), 0)),
            pl.BlockSpec((nb, None, 6, d), lambda j: (0, is_ctx(j), 0, 0)),
            pl.BlockSpec(wb.shape, lambda j: (0, 0, 0)),
            pl.BlockSpec(wc.shape, lambda j: (0, 0, 0)),
            pl.BlockSpec(a_re.shape, lambda j: (0, 0, 0)),
            pl.BlockSpec(a_im.shape, lambda j: (0, 0, 0)),
        ],
        out_specs=pl.BlockSpec((nb, tc, d), lambda j: (0, chunk(j), 0)),
        out_shape=jax.ShapeDtypeStruct((nb, ltot, d), F32),
        scratch_shapes=[
            pltpu.VMEM((npc, pk * nb * S5_PITCH, LANES), F32),
            pltpu.VMEM((npc, pk * nb * S5_PITCH, LANES), F32),
            pltpu.VMEM((npc, pk * nb, LANES), F32),
            pltpu.VMEM((npc, pk * nb, LANES), F32),
        ],
        compiler_params=_cparams(("arbitrary",)),
        name="s5_scan_bwd" if rev else "s5_scan_fwd",
    )(h3, modtab, wb, wc, a_re, a_im)


def _s5_prepare(lam_re, lam_im, log_step, b_re, b_im, c_re, c_im, nb):
    g, p = lam_re.shape
    hg = b_re.shape[-1]
    lr = lam_re.astype(F32)
    li = lam_im.astype(F32)
    dt = jnp.exp(log_step.astype(F32))[:, None]
    mag = jnp.exp(lr * dt)
    ar = mag * jnp.cos(li * dt)
    ai = mag * jnp.sin(li * dt)
    den = lr * lr + li * li
    nr = ar - 1.0
    kr = (nr * lr + ai * li) / den
    ki = (ai * lr - nr * li) / den
    br = b_re.astype(F32)
    bi = b_im.astype(F32)
    bbr = kr[..., None] * br - ki[..., None] * bi
    bbi = kr[..., None] * bi + ki[..., None] * br
    gpk = MXU_K // hg
    n_kt = g // gpk
    eye = jnp.eye(gpk, dtype=F32)

    def block_diag(t):
        full = t[:, :, :, None, :] * eye[None, :, None, :, None]
        return full.reshape(n_kt, gpk * t.shape[2], gpk * t.shape[3])

    def in_blocks(bb):
        return block_diag(bb.reshape(n_kt, gpk, p, hg).transpose(0, 1, 3, 2))

    def out_blocks(cc):
        return block_diag(cc.reshape(n_kt, gpk, hg, p).transpose(0, 1, 3, 2))

    wb = jnp.concatenate([in_blocks(bbr), in_blocks(bbi)], axis=-1).astype(BF16)
    wc = jnp.concatenate([out_blocks(c_re.astype(F32)), -out_blocks(c_im.astype(F32))], axis=1).astype(BF16)
    ncol = g * p // LANES
    pk = SUBLANES // nb
    npc = ncol // pk

    def packed_rows(a):
        t = a.reshape(pk, npc, 1, LANES).transpose(1, 0, 2, 3)
        return jnp.broadcast_to(t, (npc, pk, nb, LANES)).reshape(npc, pk * nb, LANES)

    return wb, wc, packed_rows(ar), packed_rows(ai)


def _gla_proj_body(x_ref, mod_ref, w_ref, wa_ref, wa2_ref, ba2_ref,
                   q_ref, k_ref, v_ref, g_ref, laf_ref, lab_ref, *, qk_w, v_w, dk):
    u = (x_ref[...] * (1.0 + mod_ref[1:2, :]) + mod_ref[0:1, :]).astype(BF16)
    p = jnp.dot(u, w_ref[...], preferred_element_type=F32)
    q_ref[...] = p[:, :qk_w] * (dk ** -0.5)
    k_ref[...] = p[:, qk_w:2 * qk_w]
    v_ref[...] = p[:, 2 * qk_w:2 * qk_w + v_w]
    g_ref[...] = p[:, 2 * qk_w + v_w:]
    for dr, out in enumerate((laf_ref, lab_ref)):
        a_d = jnp.dot(u, wa_ref[dr], preferred_element_type=F32).astype(BF16)
        z = jnp.dot(a_d, wa2_ref[dr], preferred_element_type=F32) + ba2_ref[dr]
        out[...] = (jnp.minimum(z, 0.0) - jnp.log1p(jnp.exp(-jnp.abs(z)))) / GLA_GATE_NORM


def _gla_proj(h2d, modtab, w_main, w_a, w_a2, b_a2, *, tiles_per_batch, ctx_tile0):
    t, d = h2d.shape
    tt = TOK_TILE
    qk_w = w_a2.shape[-1]
    v_w = (w_main.shape[1] - 2 * qk_w) // 2
    dk = qk_w // GLA_HEADS
    mod_map = lambda i: (i // tiles_per_batch, ((i % tiles_per_batch) >= ctx_tile0).astype(I32), 0, 0)
    row = lambda w: pl.BlockSpec((tt, w), lambda i: (i, 0))
    full = lambda a: pl.BlockSpec(a.shape, lambda i: (0,) * a.ndim)
    body = functools.partial(_gla_proj_body, qk_w=qk_w, v_w=v_w, dk=dk)
    b_a2r = b_a2.reshape(2, 1, qk_w).astype(F32)
    return pl.pallas_call(
        body,
        grid=(t // tt,),
        in_specs=[row(d), pl.BlockSpec((None, None, 6, d), mod_map),
                  full(w_main), full(w_a), full(w_a2), full(b_a2r)],
        out_specs=[row(qk_w), row(qk_w), row(v_w), row(v_w), row(qk_w), row(qk_w)],
        out_shape=[jax.ShapeDtypeStruct((t, w), F32) for w in (qk_w, qk_w, v_w, v_w, qk_w, qk_w)],
        compiler_params=_cparams(("parallel",)),
        name="gla_proj",
    )(h2d, modtab, w_main, w_a, w_a2, b_a2r)


def _gla_rec_body(qf_ref, kf_ref, vf_ref, laf_ref, qb_ref, kb_ref, vb_ref, lab_ref, of_ref, ob_ref, *s_refs,
                  ch, dk, dv, nh, nb):
    @pl.when(pl.program_id(0) == 0)
    def _():
        for s_ref in s_refs:
            s_ref[...] = jnp.zeros_like(s_ref)

    spc = qf_ref.shape[1] // ch
    for ci in range(spc):
        _gla_chunk(qf_ref, kf_ref, vf_ref, laf_ref, of_ref, s_refs[:nh], slice(ci * ch, (ci + 1) * ch),
                   rev=False, ch=ch, dk=dk, dv=dv, nh=nh, nb=nb)
        cb = spc - 1 - ci
        _gla_chunk(qb_ref, kb_ref, vb_ref, lab_ref, ob_ref, s_refs[nh:], slice(cb * ch, (cb + 1) * ch),
                   rev=True, ch=ch, dk=dk, dv=dv, nh=nh, nb=nb)


def _gla_chunk(q_ref, k_ref, v_ref, la_ref, o_ref, s_refs, rows, *, rev, ch, dk, dv, nh, nb):
    n = nb * ch
    stack = lambda ref: jnp.concatenate([ref[bi, rows, :] for bi in range(nb)], axis=0)
    r_i = lax.broadcasted_iota(I32, (n, n), 0)
    c_i = lax.broadcasted_iota(I32, (n, n), 1)
    seen = ((r_i // ch) == (c_i // ch)) & ((c_i >= r_i) if rev else (c_i <= r_i))
    end = 0 if rev else ch - 1
    k_all = stack(k_ref)
    v_all = stack(v_ref).astype(BF16)
    b = jnp.dot(seen.astype(F32), stack(la_ref), precision=HIGHEST, preferred_element_type=F32)
    b_ends = [b[bi * ch + end:bi * ch + end + 1, :] for bi in range(nb)]
    b_end = jnp.concatenate([jnp.broadcast_to(be, (ch, nh * dk)) for be in b_ends], axis=0)
    q_d_all = (stack(q_ref) * jnp.exp(b)).astype(BF16)
    k_d_all = (k_all * jnp.exp(-b)).astype(BF16)
    k_e_all = (k_all * jnp.exp(b_end - b)).astype(BF16)
    own = (lax.broadcasted_iota(I32, (n, nb * dk), 0) // ch) == (lax.broadcasted_iota(I32, (n, nb * dk), 1) // dk)
    zero = jnp.zeros((n, nb * dk), BF16)
    for hd in range(nh):
        qs = slice(hd * dk, (hd + 1) * dk)
        vs = slice(hd * dv, (hd + 1) * dv)
        q_d, k_d, k_e = q_d_all[:, qs], k_d_all[:, qs], k_e_all[:, qs]
        v = v_all[:, vs]
        att = lax.dot_general(q_d, k_d, (((1,), (1,)), ((), ())), preferred_element_type=F32)
        att = jnp.where(seen, att, 0.0).astype(BF16)
        q_bd = jnp.where(own, jnp.concatenate([q_d] * nb, axis=1), zero)
        k_bd = jnp.where(own, jnp.concatenate([k_e] * nb, axis=1), zero)
        s_t = s_refs[hd][...]
        o = (jnp.dot(att, v, preferred_element_type=F32)
             + lax.dot_general(q_bd, s_t.astype(BF16), (((1,), (1,)), ((), ())), preferred_element_type=F32))
        for bi in range(nb):
            o_ref[bi, rows, vs] = o[bi * ch:(bi + 1) * ch]
        g_row = jnp.concatenate([jnp.exp(be[:, qs]) for be in b_ends], axis=1)
        upd = lax.dot_general(v, k_bd, (((0,), (0,)), ((), ())), preferred_element_type=F32)
        s_refs[hd][...] = s_t * g_row + upd


def _gla_rec(q, k, v, la_f, la_b, *, n_lat):
    nb, ltot, qk_w = q.shape
    v_w = v.shape[-1]
    ch = GLA_CHUNK
    blk = GLA_STEP_CHUNKS * ch
    assert n_lat % blk == 0 and (ltot - n_lat) % blk == 0
    n_chunks = ltot // blk
    lat_chunks = n_lat // blk
    fwd = lambda j: (j + lat_chunks) % n_chunks
    bwd = lambda j: n_chunks - 1 - j
    spec = lambda w, chunk: pl.BlockSpec((nb, blk, w), lambda j: (0, chunk(j), 0))
    dk, dv = qk_w // GLA_HEADS, v_w // GLA_HEADS
    body = functools.partial(_gla_rec_body, ch=ch, dk=dk, dv=dv, nh=GLA_HEADS, nb=nb)
    out = jax.ShapeDtypeStruct((nb, ltot, v_w), F32)
    return pl.pallas_call(
        body,
        grid=(n_chunks,),
        in_specs=[spec(qk_w, fwd), spec(qk_w, fwd), spec(v_w, fwd), spec(qk_w, fwd),
                  spec(qk_w, bwd), spec(qk_w, bwd), spec(v_w, bwd), spec(qk_w, bwd)],
        out_specs=[spec(v_w, fwd), spec(v_w, bwd)],
        out_shape=[out, out],
        scratch_shapes=[pltpu.VMEM((dv, nb * dk), F32) for _ in range(2 * GLA_HEADS)],
        compiler_params=_cparams(("arbitrary",)),
        name="gla_rec",
    )(q, k, v, la_f, q, k, v, la_b)


def _layer_norm_rows(v, g, b):
    mu = jnp.mean(v, axis=-1, keepdims=True)
    c = v - mu
    var = jnp.mean(c * c, axis=-1, keepdims=True)
    return c * lax.rsqrt(var + LN_EPS) * g + b


def _post_mixer_body(*refs, kind, alpha, tt, dv):
    if kind == "s5":
        (h_ref, mod_ref, ya_ref, yb_ref, dsk_ref, w_ref, lng_ref, lnb_ref, wr_ref, br_ref,
         h1_ref, xr_ref, meta_ref, gt_ref, cnt_ref, base_ref) = refs
    else:
        (h_ref, mod_ref, ya_ref, yb_ref, gate_ref, ng_ref, w_ref, lng_ref, lnb_ref, wr_ref, br_ref,
         h1_ref, xr_ref, meta_ref, gt_ref, cnt_ref, base_ref) = refs
    i = pl.program_id(0)

    @pl.when(i == 0)
    def _():
        base_ref[...] = jnp.zeros_like(base_ref)

    h = h_ref[...]
    d = h.shape[-1]
    if kind == "s5":
        u = h * (1.0 + mod_ref[1:2, :]) + mod_ref[0:1, :]
        y = dsk_ref[...] * u + ya_ref[...] + yb_ref[...]
        ge = 0.5 * y * (1.0 + jnp.tanh(math.sqrt(2.0 / math.pi) * (y + 0.044715 * (y * y * y))))
        z = jnp.dot(ge.astype(BF16), w_ref[...], preferred_element_type=F32)
        mix = z[:, :d] * _sigmoid(z[:, d:])
    else:
        o = ya_ref[...] + yb_ref[...]
        parts = []
        for hd in range(d // dv):
            oh = o[:, hd * dv:(hd + 1) * dv]
            ms = jnp.mean(oh * oh, axis=-1, keepdims=True)
            parts.append(oh * lax.rsqrt(ms + LN_EPS))
        on = jnp.concatenate(parts, axis=-1) * ng_ref[...]
        gv = gate_ref[...]
        a = on * (gv * _sigmoid(gv))
        mix = jnp.dot(a.astype(BF16), w_ref[...], preferred_element_type=F32)

    h1 = _layer_norm_rows(alpha * h + mod_ref[2:3, :] * mix, lng_ref[...], lnb_ref[...])
    h1_ref[...] = h1
    u2 = h1 * (1.0 + mod_ref[4:5, :]) + mod_ref[3:4, :]
    nc = d // LANES
    for c in range(nc):
        xr_ref[pl.ds(c, tt, stride=nc), :] = u2[:, c * LANES:(c + 1) * LANES]

    logits = lax.dot_general(wr_ref[...], u2, (((1,), (1,)), ((), ())), precision=HIGHEST,
                             preferred_element_type=F32) + br_ref[...]
    n_e = logits.shape[0]
    erow = lax.broadcasted_iota(I32, logits.shape, 0)
    work = logits
    vals, idxs = [], []
    for _ in range(TOP_K):
        m = jnp.max(work, axis=0, keepdims=True)
        idx = jnp.min(jnp.where(work == m, erow, n_e), axis=0, keepdims=True)
        vals.append(m)
        idxs.append(idx)
        work = jnp.where(erow == idx, -jnp.inf, work)
    exps = [jnp.exp(v - vals[0]) for v in vals]
    den = exps[0]
    for e in exps[1:]:
        den = den + e

    multi = jnp.zeros(logits.shape, F32)
    for idx in idxs:
        multi = multi + (erow == idx).astype(F32)
    r_i = lax.broadcasted_iota(I32, (tt, tt), 0)
    c_i = lax.broadcasted_iota(I32, (tt, tt), 1)
    earlier = (r_i < c_i).astype(BF16)
    pos = jnp.dot(multi.astype(BF16), earlier, preferred_element_type=F32) + base_ref[:, 0:1]
    row8 = lax.broadcasted_iota(I32, (SUBLANES, tt), 0)
    meta = jnp.zeros((SUBLANES, tt), I32)
    gts = jnp.zeros((SUBLANES, tt), F32)
    for kk in range(TOP_K):
        rank_k = jnp.sum(jnp.where(erow == idxs[kk], pos, 0.0), axis=0, keepdims=True)
        meta = jnp.where(row8 == kk, idxs[kk], meta)
        meta = jnp.where(row8 == TOP_K + kk, rank_k.astype(I32), meta)
        gts = jnp.where(row8 == kk, exps[kk] / den, gts)
    meta_ref[...] = meta
    gt_ref[...] = jnp.transpose(
        jnp.concatenate([gts, jnp.zeros((ROUTE_LANES - SUBLANES, tt), F32)], axis=0))
    new_base = base_ref[:, 0:1] + jnp.sum(multi, axis=1, keepdims=True)
    base_ref[...] = jnp.broadcast_to(new_base, base_ref.shape)
    cnt_ref[...] = jnp.broadcast_to(new_base, cnt_ref.shape)


def _post_mixer(kind, h2d, modtab, ya, yb, extra, w_mix, ln_g, ln_b, w_router, b_router, *,
                alpha, tiles_per_batch, ctx_tile0):
    t, d = h2d.shape
    tt = TOK_TILE
    n_e = w_router.shape[-1]
    wr = w_router.astype(F32).T
    br = b_router.astype(F32).reshape(n_e, 1)
    mod_map = lambda i: (i // tiles_per_batch, ((i % tiles_per_batch) >= ctx_tile0).astype(I32), 0, 0)
    row = lambda w: pl.BlockSpec((tt, w), lambda i: (i, 0))
    full = lambda a: pl.BlockSpec(a.shape, lambda i: (0,) * a.ndim)
    vec = lambda a: a.reshape(1, -1).astype(F32)
    if kind == "s5":
        (d_skip,) = extra
        ins = [h2d, modtab, ya, yb, vec(d_skip), w_mix, vec(ln_g), vec(ln_b), wr, br]
        in_specs = [row(d), pl.BlockSpec((None, None, 6, d), mod_map), row(d), row(d)]
        in_specs += [full(a) for a in ins[4:]]
        dv = d
    else:
        gate, norm_g = extra
        dv = norm_g.shape[-1]
        ng = jnp.tile(norm_g.astype(F32), d // dv).reshape(1, d)
        ins = [h2d, modtab, ya, yb, gate, ng, w_mix, vec(ln_g), vec(ln_b), wr, br]
        in_specs = [row(d), pl.BlockSpec((None, None, 6, d), mod_map), row(d), row(d), row(d)]
        in_specs += [full(a) for a in ins[5:]]
    body = functools.partial(_post_mixer_body, kind=kind, alpha=alpha, tt=tt, dv=dv)
    return pl.pallas_call(
        body,
        grid=(t // tt,),
        in_specs=in_specs,
        out_specs=[
            row(d),
            pl.BlockSpec((tt * (d // LANES), LANES), lambda i: (i, 0)),
            pl.BlockSpec((SUBLANES, tt), lambda i: (0, i)),
            row(ROUTE_LANES),
            pl.BlockSpec((n_e, ROUTE_LANES), lambda i: (0, 0)),
        ],
        out_shape=[
            jax.ShapeDtypeStruct((t, d), F32),
            jax.ShapeDtypeStruct((t * (d // LANES), LANES), F32),
            jax.ShapeDtypeStruct((SUBLANES, t), I32),
            jax.ShapeDtypeStruct((t, ROUTE_LANES), F32),
            jax.ShapeDtypeStruct((n_e, ROUTE_LANES), F32),
        ],
        scratch_shapes=[pltpu.VMEM((n_e, ROUTE_LANES), F32)],
        compiler_params=_cparams(("arbitrary",)),
        name="post_mixer_" + kind,
    )(*ins)


def _row_copy_wait(src_hbm, dst, sem, n_rows):
    pltpu.make_async_copy(src_hbm.at[pl.ds(0, n_rows), :], dst.at[pl.ds(0, n_rows), :], sem).wait()


def _tok_rows(t, nc):
    return pl.ds(pl.multiple_of(t * nc, nc), nc)


def _dispatch_body(dst_ref, zs_ref, x_ref, xp_hbm, zbuf, sem, zsem, *, tt, nc, n_e):
    zr = zbuf.shape[0]

    @pl.when(pl.program_id(0) == 0)
    def _():
        zbuf[...] = jnp.zeros_like(zbuf)
        for e in range(2 * n_e):
            @pl.when(zs_ref[e] >= 0)
            def _():
                pltpu.make_async_copy(zbuf, xp_hbm.at[pl.ds(pl.multiple_of(zs_ref[e] * nc, nc), zr), :], zsem).start()
        for e in range(2 * n_e):
            @pl.when(zs_ref[e] >= 0)
            def _():
                pltpu.make_async_copy(zbuf, xp_hbm.at[pl.ds(pl.multiple_of(zs_ref[e] * nc, nc), zr), :], zsem).wait()

    def issue(t, c):
        for kk in range(TOP_K):
            dst = dst_ref[t * TOP_K + kk]
            pltpu.make_async_copy(x_ref.at[_tok_rows(t, nc), :], xp_hbm.at[_tok_rows(dst, nc), :],
                                  sem).start(priority=kk % 2)
        return c

    lax.fori_loop(0, tt, issue, 0, unroll=8)
    _row_copy_wait(xp_hbm, xp_hbm, sem, tt * TOP_K * nc)


def _dispatch(dest, zstart, xr, n_rows, nc):
    t = xr.shape[0] // nc
    tt = TOK_TILE
    return pl.pallas_call(
        functools.partial(_dispatch_body, tt=tt, nc=nc, n_e=N_EXPERTS),
        grid=(t // tt,),
        in_specs=[pl.BlockSpec((tt * TOP_K,), lambda i: (i,), memory_space=pltpu.SMEM),
                  pl.BlockSpec(memory_space=pltpu.SMEM),
                  pl.BlockSpec((tt * nc, LANES), lambda i: (i, 0))],
        out_specs=pl.BlockSpec(memory_space=pl.ANY),
        out_shape=jax.ShapeDtypeStruct((n_rows * nc, LANES), xr.dtype),
        scratch_shapes=[pltpu.VMEM((MOE_ROWS * nc, LANES), xr.dtype),
                        pltpu.SemaphoreType.DMA, pltpu.SemaphoreType.DMA],
        compiler_params=_cparams(("arbitrary",)),
        name="moe_dispatch",
    )(dest, zstart, xr)


def _expert_body(be_ref, nu_ref, x_ref, wu_ref, bu_ref, wd_ref, bd_ref, y_ref, wub, wdb):
    i = pl.program_id(0)
    e = be_ref[i]
    prev = be_ref[jnp.maximum(i - 1, 0)]

    @pl.when((i == 0) | (e != prev))
    def _():
        wub[...] = wu_ref[...].astype(BF16)
        wdb[...] = wd_ref[...].astype(BF16)

    nc = wu_ref.shape[0] // LANES
    r = x_ref.shape[0] // nc

    @pl.when(i < nu_ref[0])
    def _():
        x = jnp.concatenate([x_ref[pl.ds(c, r, stride=nc), :] for c in range(nc)], axis=-1).astype(BF16)
        ff = wdb.shape[0]
        y = bd_ref[...]
        for j in range(ff // FF_SLICE):
            lo, hi = j * FF_SLICE, (j + 1) * FF_SLICE
            h_glu = jnp.dot(x, wub[:, lo:hi], preferred_element_type=F32) + bu_ref[:, lo:hi]
            h_lin = jnp.dot(x, wub[:, ff + lo:ff + hi], preferred_element_type=F32) + bu_ref[:, ff + lo:ff + hi]
            h_glu = jnp.minimum(h_glu, SWIGLU_LIMIT)
            h_lin = jnp.clip(h_lin, -SWIGLU_LIMIT, SWIGLU_LIMIT)
            a = h_glu * _sigmoid(SWIGLU_ALPHA * h_glu) * (h_lin + 1.0)
            y = y + jnp.dot(a.astype(BF16), wdb[lo:hi, :], preferred_element_type=F32)
        for c in range(nc):
            y_ref[pl.ds(c, r, stride=nc), :] = y[:, c * LANES:(c + 1) * LANES]

    @pl.when(i >= nu_ref[0])
    def _():
        y_ref[...] = jnp.zeros_like(y_ref)


def _experts(blk_e, n_used, xp, layer, w_up, b_up, w_down, b_down):
    _, n_e, d, two_ff = w_up.shape
    nc = d // LANES
    n_rows = xp.shape[0] // nc
    r = MOE_ROWS
    ff = two_ff // 2
    grid_spec = pltpu.PrefetchScalarGridSpec(
        num_scalar_prefetch=2,
        grid=(n_rows // r,),
        in_specs=[
            pl.BlockSpec((r * nc, LANES), lambda i, be, nu: (jnp.minimum(i, nu[0] - 1), 0)),
            pl.BlockSpec((None, None, d, two_ff), lambda i, be, nu: (layer, be[i], 0, 0)),
            pl.BlockSpec((None, None, 1, two_ff), lambda i, be, nu: (layer, be[i], 0, 0)),
            pl.BlockSpec((None, None, ff, d), lambda i, be, nu: (layer, be[i], 0, 0)),
            pl.BlockSpec((None, None, 1, d), lambda i, be, nu: (layer, be[i], 0, 0)),
        ],
        out_specs=pl.BlockSpec((r * nc, LANES), lambda i, be, nu: (i, 0)),
        scratch_shapes=[pltpu.VMEM((d, two_ff), BF16), pltpu.VMEM((ff, d), BF16)],
    )
    return pl.pallas_call(
        _expert_body,
        grid_spec=grid_spec,
        out_shape=jax.ShapeDtypeStruct(xp.shape, F32),
        compiler_params=_cparams(("arbitrary",)),
        name="moe_experts",
    )(blk_e, n_used, xp, w_up, b_up.reshape(-1, n_e, 1, two_ff), w_down, b_down.reshape(-1, n_e, 1, d))


def _combine_body(dst_ref, gt_ref, h_ref, mod_ref, lng_ref, lnb_ref, yp_hbm,
                  o_ref, buf, sems, *, tt, nc, n_tiles, alpha):
    s = pl.program_id(0)
    slot = s % 2

    @pl.when(s < n_tiles)
    def _():
        def issue(t, c):
            for kk in range(TOP_K):
                q = t * TOP_K + kk
                src = dst_ref[q]
                pltpu.make_async_copy(yp_hbm.at[_tok_rows(src, nc), :], buf.at[slot, kk, _tok_rows(t, nc), :],
                                      sems.at[slot]).start(priority=kk % 2)
            return c

        lax.fori_loop(0, tt, issue, 0, unroll=4)

    @pl.when(s > 0)
    def _():
        prev = 1 - slot
        for kk in range(TOP_K):
            _row_copy_wait(yp_hbm, buf.at[prev, kk], sems.at[prev], tt * nc)
        gt = gt_ref[...]
        cols = []
        for c in range(nc):
            acc = gt[:, 0:1] * buf[prev, 0, pl.ds(c, tt, stride=nc), :]
            for kk in range(1, TOP_K):
                acc = acc + gt[:, kk:kk + 1] * buf[prev, kk, pl.ds(c, tt, stride=nc), :]
            cols.append(acc)
        f = jnp.concatenate(cols, axis=-1)
        o_ref[...] = _layer_norm_rows(alpha * h_ref[...] + mod_ref[5:6, :] * f, lng_ref[...], lnb_ref[...])


def _combine(dest, gates, h1, modtab, ln_g, ln_b, yp, *, alpha, tiles_per_batch, ctx_tile0):
    t, d = h1.shape
    tt = TOK_TILE
    nc = d // LANES
    n_tiles = t // tt
    nxt = lambda s: jnp.minimum(s, n_tiles - 1)
    cur = lambda s: jnp.maximum(s - 1, 0)
    smem_blk = pl.BlockSpec((tt * TOP_K,), lambda s: (nxt(s),), memory_space=pltpu.SMEM)
    mod_map = lambda s: (cur(s) // tiles_per_batch, ((cur(s) % tiles_per_batch) >= ctx_tile0).astype(I32), 0, 0)
    vec = lambda a: a.reshape(1, -1).astype(F32)
    return pl.pallas_call(
        functools.partial(_combine_body, tt=tt, nc=nc, n_tiles=n_tiles, alpha=alpha),
        grid=(n_tiles + 1,),
        in_specs=[smem_blk,
                  pl.BlockSpec((tt, ROUTE_LANES), lambda s: (cur(s), 0)),
                  pl.BlockSpec((tt, d), lambda s: (cur(s), 0)),
                  pl.BlockSpec((None, None, 6, d), mod_map),
                  pl.BlockSpec((1, d), lambda s: (0, 0)),
                  pl.BlockSpec((1, d), lambda s: (0, 0)),
                  pl.BlockSpec(memory_space=pl.ANY)],
        out_specs=pl.BlockSpec((tt, d), lambda s: (cur(s), 0)),
        out_shape=jax.ShapeDtypeStruct((t, d), F32),
        scratch_shapes=[pltpu.VMEM((2, TOP_K, tt * nc, LANES), F32), pltpu.SemaphoreType.DMA((2,))],
        compiler_params=_cparams(("arbitrary",)),
        name="moe_combine",
    )(dest, gates, h1, modtab, vec(ln_g), vec(ln_b), yp)


def _moe_layer(h1, xr, meta, gt, counts, modtab, ln_g, ln_b, layer, w_up, b_up, w_down, b_down, *,
               alpha, tiles_per_batch, ctx_tile0):
    t, d = h1.shape
    n_e = w_up.shape[1]
    r = MOE_ROWS
    n_blocks = -(-(t * TOP_K) // r) + n_e
    cnt = counts[:, 0].astype(I32)
    padded = (cnt + r - 1) // r * r
    pends = jnp.cumsum(padded)
    pstart = pends - padded
    tail = pends[-1] + jnp.arange(n_e, dtype=I32) * r
    zstart = jnp.concatenate([jnp.where(cnt > 0, pends - r, -1),
                              jnp.where(tail < n_blocks * r, tail, -1)]).astype(I32)
    blk_row0 = jnp.arange(n_blocks, dtype=I32) * r
    blk_e = jnp.minimum(jnp.sum((pends[None, :] <= blk_row0[:, None]).astype(I32), axis=1), n_e - 1)
    n_used = (pends[-1] // r).astype(I32).reshape(1)
    ti, rk = meta[:TOP_K], meta[TOP_K:2 * TOP_K]
    first = jnp.sum(jnp.where(ti[:, :, None] == jnp.arange(n_e, dtype=I32), pstart, 0), axis=-1)
    dest = (first + rk).T.reshape(-1).astype(I32)
    xp = _dispatch(dest, zstart, xr, n_blocks * r, d // LANES)
    yp = _experts(blk_e, n_used, xp, layer, w_up, b_up, w_down, b_down)
    return _combine(dest, gt, h1, modtab, ln_g, ln_b, yp,
                    alpha=alpha, tiles_per_batch=tiles_per_batch, ctx_tile0=ctx_tile0)


def _to_cm_body(lat_ref, ctx_ref, o_ref, *, rows, k, n_lat_steps):
    w = pl.program_id(1)
    for q in range(SUBLANES // k):
        @pl.when((w < n_lat_steps) & (w % (SUBLANES // k) == q))
        def _():
            for i in range(k):
                o_ref[i * rows:(i + 1) * rows, :] = lat_ref[:, q * k + i, :]

    @pl.when(w >= n_lat_steps)
    def _():
        o_ref[...] = ctx_ref[...]


def _to_col_major(h3, n_lat):
    bn, ltot, d = h3.shape
    lc = ltot - n_lat
    rows = n_lat // GRID_W
    k = max(kk for kk in (1, 2, 4, 8) if lc % (kk * rows) == 0)
    ob = k * rows
    n_lat_steps = GRID_W // k
    per_blk = SUBLANES // k
    return pl.pallas_call(
        functools.partial(_to_cm_body, rows=rows, k=k, n_lat_steps=n_lat_steps),
        grid=(bn, n_lat_steps + lc // ob),
        in_specs=[pl.BlockSpec((None, rows, SUBLANES, d),
                               lambda b, w: (b, 0, jnp.minimum(w, n_lat_steps - 1) // per_blk, 0)),
                  pl.BlockSpec((None, ob, d), lambda b, w: (b, jnp.maximum(w, n_lat_steps), 0))],
        out_specs=pl.BlockSpec((None, ob, d), lambda b, w: (b, w, 0)),
        out_shape=jax.ShapeDtypeStruct(h3.shape, h3.dtype),
        compiler_params=_cparams(("arbitrary", "arbitrary")),
        name="to_col_major",
    )(h3.reshape(bn, ltot // GRID_W, GRID_W, d), h3)


def _to_rm_body(lat_ref, ctx_ref, o_ref, *, rows, n_ctx_rows):
    w = pl.program_id(1)
    for i in range(SUBLANES):
        o_ref[0:rows, i, :] = lat_ref[i * rows:(i + 1) * rows, :]
    for m in range(n_ctx_rows):
        o_ref[rows + m, :, :] = ctx_ref[pl.ds(pl.multiple_of(m * GRID_W + w * SUBLANES, SUBLANES), SUBLANES), :]


def _to_row_major(h3, n_lat):
    bn, ltot, d = h3.shape
    lc = ltot - n_lat
    rows = n_lat // GRID_W
    assert lc % GRID_W == 0 and n_lat % lc == 0
    out = pl.pallas_call(
        functools.partial(_to_rm_body, rows=rows, n_ctx_rows=lc // GRID_W),
        grid=(bn, GRID_W // SUBLANES),
        in_specs=[pl.BlockSpec((None, SUBLANES * rows, d), lambda b, w: (b, w, 0)),
                  pl.BlockSpec((None, lc, d), lambda b, w: (b, n_lat // lc, 0))],
        out_specs=pl.BlockSpec((None, ltot // GRID_W, SUBLANES, d), lambda b, w: (b, 0, w, 0)),
        out_shape=jax.ShapeDtypeStruct((bn, ltot // GRID_W, GRID_W, d), h3.dtype),
        compiler_params=_cparams(("arbitrary", "arbitrary")),
        name="to_row_major",
    )(h3, h3)
    return out.reshape(bn, ltot, d)


def kernel(x, c, ctx, c_ctx, w_ada, b_ada, ln1_g, ln1_b, ln2_g, ln2_b, s5_lam_re, s5_lam_im, s5_log_step, s5_b_re, s5_b_im, s5_c_re, s5_c_im, s5_d, s5_w_glu, gla_w_in, gla_w_a2, gla_b_a2, gla_norm_g, gla_w_out, moe_w_router, moe_b_router, moe_w_up, moe_b_up, moe_w_down, moe_b_down):
    bn, l, d = x.shape
    lc = ctx.shape[1]
    depth = w_ada.shape[0]
    ltot = l + lc
    rows = l // GRID_W
    alpha = (2 * depth) ** 0.25
    assert l % TOK_TILE == 0 and lc % TOK_TILE == 0 and l % S5_CHUNK == 0 and lc % S5_CHUNK == 0
    assert bn < SUBLANES
    tiles_per_batch = ltot // TOK_TILE
    ctx_tile0 = l // TOK_TILE
    tile_kw = dict(tiles_per_batch=tiles_per_batch, ctx_tile0=ctx_tile0)

    cc = jnp.zeros((SUBLANES, d), F32).at[:bn].set(c.astype(F32)).at[bn].set(c_ctx.astype(F32))
    mod = _ada_table(cc, w_ada.astype(F32), b_ada.astype(F32))
    mod_lat = mod[:, :bn].reshape(depth, bn, 1, 6, d)
    mod_ctx = jnp.broadcast_to(mod[:, bn].reshape(depth, 1, 1, 6, d), (depth, bn, 1, 6, d))
    modtab = jnp.concatenate([mod_lat, mod_ctx], axis=2)

    to_cm = functools.partial(_to_col_major, n_lat=l)
    to_rm = functools.partial(_to_row_major, n_lat=l)

    h3 = jnp.concatenate([x.astype(F32), ctx.astype(F32)], axis=1)
    for i in range(depth):
        j = i // 2
        mt = modtab[i]
        if i % 2 == 0:
            ys = []
            for dr, rev in ((0, False), (1, True)):
                wb, wc, a_re, a_im = _s5_prepare(s5_lam_re[j, dr], s5_lam_im[j, dr], s5_log_step[j, dr],
                                                 s5_b_re[j, dr], s5_b_im[j, dr], s5_c_re[j, dr], s5_c_im[j, dr], bn)
                ys.append(_s5_scan(h3, mt, wb, wc, a_re, a_im, rev=rev, n_lat=l).reshape(bn * ltot, d))
            h2d = h3.reshape(bn * ltot, d)
            outs = _post_mixer("s5", h2d, mt, ys[0], ys[1], (s5_d[j],), s5_w_glu[j].astype(BF16),
                               ln1_g[i], ln1_b[i], moe_w_router[i], moe_b_router[i], alpha=alpha, **tile_kw)
        else:
            h3 = to_cm(h3)
            h2d = h3.reshape(bn * ltot, d)
            w_in = gla_w_in[j]
            n_main = w_in.shape[1] - 2 * GLA_GATE_RANK
            w_a = w_in[:, n_main:].reshape(d, 2, GLA_GATE_RANK).transpose(1, 0, 2).astype(BF16)
            q, k, v, g, la_f, la_b = _gla_proj(h2d, mt, w_in[:, :n_main].astype(BF16), w_a,
                                               gla_w_a2[j].astype(BF16), gla_b_a2[j], **tile_kw)
            r3 = lambda a: a.reshape(bn, ltot, a.shape[-1])
            o_f, o_b = [o.reshape(bn * ltot, d) for o in _gla_rec(r3(q), r3(k), r3(v), r3(la_f), r3(la_b), n_lat=l)]
            outs = _post_mixer("gla", h2d, mt, o_f, o_b, (g, gla_norm_g[j]), gla_w_out[j].astype(BF16),
                               ln1_g[i], ln1_b[i], moe_w_router[i], moe_b_router[i], alpha=alpha, **tile_kw)
        h1, xr, meta, gt, counts = outs
        h2 = _moe_layer(h1, xr, meta, gt, counts, mt, ln2_g[i], ln2_b[i],
                        i, moe_w_up, moe_b_up, moe_w_down, moe_b_down, alpha=alpha, **tile_kw)
        h3 = h2.reshape(bn, ltot, d)
        if i % 2 == 1:
            h3 = to_rm(h3)
    return h3[:, :l].astype(x.dtype)
```

```python
import functools
import math

import jax
import jax.numpy as jnp
from jax import lax
from jax.experimental import pallas as pl
from jax.experimental.pallas import tpu as pltpu

F32 = jnp.float32
BF16 = jnp.bfloat16
I32 = jnp.int32
HIGHEST = lax.Precision.HIGHEST

GRID_W = 64
S5_GROUP = 16
S5_STATE = 64
GLA_HEADS = 4
GLA_GATE_RANK = 16
GLA_GATE_NORM = 16.0
GLA_CHUNK = 64
GLA_STEP_CHUNKS = 2
N_EXPERTS = 32
TOP_K = 4
SWIGLU_ALPHA = 1.702
SWIGLU_LIMIT = 7.0
LN_EPS = 1e-5

LANES = 128
SUBLANES = 8
MXU_K = 256
VMEM_LIMIT = 56 * 1024 * 1024

TOK_TILE = 256
S5_CHUNK = 128
S5_PITCH = S5_CHUNK + SUBLANES // 2
MOE_ROWS = 512
FF_SLICE = 1024
ROUTE_LANES = LANES


def _sigmoid(x):
    return 1.0 / (1.0 + jnp.exp(-x))


def _cparams(sem):
    return pltpu.CompilerParams(dimension_semantics=sem, vmem_limit_bytes=VMEM_LIMIT)


def _ada_body(c_ref, w_ref, b_ref, o_ref):
    c = c_ref[...]
    cond = c * _sigmoid(c)
    o_ref[...] = jnp.dot(cond, w_ref[...], precision=HIGHEST, preferred_element_type=F32) + b_ref[...]


def _ada_table(cc, w_ada, b_ada):
    depth, d, six_d = w_ada.shape
    n_tiles = six_d // d
    return pl.pallas_call(
        _ada_body,
        grid=(depth, n_tiles),
        in_specs=[
            pl.BlockSpec((SUBLANES, d), lambda i, n: (0, 0)),
            pl.BlockSpec((None, d, d), lambda i, n: (i, 0, n)),
            pl.BlockSpec((None, 1, d), lambda i, n: (i, 0, n)),
        ],
        out_specs=pl.BlockSpec((None, SUBLANES, d), lambda i, n: (i, 0, n)),
        out_shape=jax.ShapeDtypeStruct((depth, SUBLANES, six_d), F32),
        compiler_params=_cparams(("arbitrary", "arbitrary")),
        name="ada_table",
    )(cc, w_ada, b_ada.reshape(depth, 1, six_d))


def _s5_scan_body(x_ref, mod_ref, wb_ref, wc_ref, are_ref, aim_ref, y_ref,
                  sre_ref, sim_ref, st_re, st_im, u_ref, *, rev, tc, pitch, nb, n_kt, cpk, pk):
    j = pl.program_id(0)
    ncol = n_kt * cpk
    npc = ncol // pk
    half = cpk * LANES
    gr = nb * pitch

    @pl.when(j == 0)
    def _():
        st_re[...] = jnp.zeros_like(st_re)
        st_im[...] = jnp.zeros_like(st_im)
        u_ref[...] = jnp.zeros_like(u_ref)

    for b in range(nb):
        u_ref[b * pitch:b * pitch + tc, :] = x_ref[b] * (1.0 + mod_ref[b, 1:2, :]) + mod_ref[b, 0:1, :]
    u = u_ref[...].astype(BF16)
    for kt in range(n_kt):
        r = jnp.dot(u[:, MXU_K * kt:MXU_K * (kt + 1)], wb_ref[kt], preferred_element_type=F32)
        for c in range(cpk):
            col = kt * cpk + c
            grows = slice((col // npc) * gr, (col // npc + 1) * gr)
            sre_ref[col % npc, grows, :] = r[:, LANES * c:LANES * (c + 1)]
            sim_ref[col % npc, grows, :] = r[:, half + LANES * c:half + LANES * (c + 1)]

    grp = 8
    for cg in range(npc // grp):
        cols = list(range(cg * grp, (cg + 1) * grp))
        ar = [are_ref[c] for c in cols]
        ai = [aim_ref[c] for c in cols]
        init = tuple(st_re[c] for c in cols) + tuple(st_im[c] for c in cols)

        def step(t, carry, cols=cols, ar=ar, ai=ai):
            tt = (tc - 1 - t) if rev else t
            out_re, out_im = [], []
            for k, c in enumerate(cols):
                rows = pl.ds(tt, pk * nb, stride=pitch)
                pr, pi = carry[k], carry[grp + k]
                nr = ar[k] * pr - ai[k] * pi + sre_ref[c, rows, :]
                ni = ar[k] * pi + ai[k] * pr + sim_ref[c, rows, :]
                sre_ref[c, rows, :] = nr
                sim_ref[c, rows, :] = ni
                out_re.append(nr)
                out_im.append(ni)
            return tuple(out_re) + tuple(out_im)

        fin = lax.fori_loop(0, tc, step, init)
        for k, c in enumerate(cols):
            st_re[c] = fin[k]
            st_im[c] = fin[grp + k]

    def stacked(ref, kt):
        def piece(c):
            col = kt * cpk + c
            return ref[col % npc, (col // npc) * gr:(col // npc + 1) * gr, :]

        return jnp.concatenate([piece(c) for c in range(cpk)], axis=-1).astype(BF16)

    for kt in range(n_kt):
        y = (jnp.dot(stacked(sre_ref, kt), wc_ref[kt, :half, :], preferred_element_type=F32)
             + jnp.dot(stacked(sim_ref, kt), wc_ref[kt, half:, :], preferred_element_type=F32))
        for b in range(nb):
            y_ref[b, :, MXU_K * kt:MXU_K * (kt + 1)] = y[b * pitch:b * pitch + tc]


def _s5_scan(h3, modtab, wb, wc, a_re, a_im, *, rev, n_lat):
    nb, ltot, d = h3.shape
    tc = S5_CHUNK
    n_chunks = ltot // tc
    lat_chunks = n_lat // tc
    n_kt, _, two_half = wb.shape
    cpk = two_half // (2 * LANES)
    ncol = n_kt * cpk
    pk = SUBLANES // nb
    npc = ncol // pk

    if rev:
        chunk = lambda j: n_chunks - 1 - j
    else:
        chunk = lambda j: (j + lat_chunks) % n_chunks
    is_ctx = lambda j: (chunk(j) >= lat_chunks).astype(I32)

    body = functools.partial(_s5_scan_body, rev=rev, tc=tc, pitch=S5_PITCH, nb=nb, n_kt=n_kt, cpk=cpk, pk=pk)
    return pl.pallas_call(
        body,
        grid=(n_chunks,),
        in_specs=[
            pl.BlockSpec((nb, tc, d), lambda j: (0, chunk(j), 0)),
            pl.BlockSpec((nb, None, 6, d), lambda j: (0, is_ctx(j), 0, 0)),
            pl.BlockSpec(wb.shape, lambda j: (0, 0, 0)),
            pl.BlockSpec(wc.shape, lambda j: (0, 0, 0)),
            pl.BlockSpec(a_re.shape, lambda j: (0, 0, 0)),
            pl.BlockSpec(a_im.shape, lambda j: (0, 0, 0)),
        ],
        out_specs=pl.BlockSpec((nb, tc, d), lambda j: (0, chunk(j), 0)),
        out_shape=jax.ShapeDtypeStruct((nb, ltot, d), F32),
        scratch_shapes=[
            pltpu.VMEM((npc, pk * nb * S5_PITCH, LANES), F32),
            pltpu.VMEM((npc, pk * nb * S5_PITCH, LANES), F32),
            pltpu.VMEM((npc, pk * nb, LANES), F32),
            pltpu.VMEM((npc, pk * nb, LANES), F32),
            pltpu.VMEM((nb * S5_PITCH, d), F32),
        ],
        compiler_params=_cparams(("arbitrary",)),
        name="s5_scan_bwd" if rev else "s5_scan_fwd",
    )(h3, modtab, wb, wc, a_re, a_im)


def _s5_prepare(lam_re, lam_im, log_step, b_re, b_im, c_re, c_im, nb):
    g, p = lam_re.shape
    hg = b_re.shape[-1]
    lr = lam_re.astype(F32)
    li = lam_im.astype(F32)
    dt = jnp.exp(log_step.astype(F32))[:, None]
    mag = jnp.exp(lr * dt)
    ar = mag * jnp.cos(li * dt)
    ai = mag * jnp.sin(li * dt)
    den = lr * lr + li * li
    nr = ar - 1.0
    kr = (nr * lr + ai * li) / den
    ki = (ai * lr - nr * li) / den
    br = b_re.astype(F32)
    bi = b_im.astype(F32)
    bbr = kr[..., None] * br - ki[..., None] * bi
    bbi = kr[..., None] * bi + ki[..., None] * br
    gpk = MXU_K // hg
    n_kt = g // gpk
    eye = jnp.eye(gpk, dtype=F32)

    def block_diag(t):
        full = t[:, :, :, None, :] * eye[None, :, None, :, None]
        return full.reshape(n_kt, gpk * t.shape[2], gpk * t.shape[3])

    def in_blocks(bb):
        return block_diag(bb.reshape(n_kt, gpk, p, hg).transpose(0, 1, 3, 2))

    def out_blocks(cc):
        return block_diag(cc.reshape(n_kt, gpk, hg, p).transpose(0, 1, 3, 2))

    wb = jnp.concatenate([in_blocks(bbr), in_blocks(bbi)], axis=-1).astype(BF16)
    wc = jnp.concatenate([out_blocks(c_re.astype(F32)), -out_blocks(c_im.astype(F32))], axis=1).astype(BF16)
    ncol = g * p // LANES
    pk = SUBLANES // nb
    npc = ncol // pk

    def packed_rows(a):
        t = a.reshape(pk, npc, 1, LANES).transpose(1, 0, 2, 3)
        return jnp.broadcast_to(t, (npc, pk, nb, LANES)).reshape(npc, pk * nb, LANES)

    return wb, wc, packed_rows(ar), packed_rows(ai)


def _gla_proj_body(x_ref, mod_ref, w_ref, wa_ref, wa2_ref, ba2_ref,
                   q_ref, k_ref, v_ref, g_ref, laf_ref, lab_ref, *, qk_w, v_w, dk):
    u = (x_ref[...] * (1.0 + mod_ref[1:2, :]) + mod_ref[0:1, :]).astype(BF16)
    p = jnp.dot(u, w_ref[...], preferred_element_type=F32)
    q_ref[...] = p[:, :qk_w] * (dk ** -0.5)
    k_ref[...] = p[:, qk_w:2 * qk_w]
    v_ref[...] = p[:, 2 * qk_w:2 * qk_w + v_w]
    g_ref[...] = p[:, 2 * qk_w + v_w:]
    for dr, out in enumerate((laf_ref, lab_ref)):
        a_d = jnp.dot(u, wa_ref[dr], preferred_element_type=F32).astype(BF16)
        z = jnp.dot(a_d, wa2_ref[dr], preferred_element_type=F32) + ba2_ref[dr]
        out[...] = (jnp.minimum(z, 0.0) - jnp.log1p(jnp.exp(-jnp.abs(z)))) / GLA_GATE_NORM


def _gla_proj(h2d, modtab, w_main, w_a, w_a2, b_a2, *, tiles_per_batch, ctx_tile0):
    t, d = h2d.shape
    tt = TOK_TILE
    qk_w = w_a2.shape[-1]
    v_w = (w_main.shape[1] - 2 * qk_w) // 2
    dk = qk_w // GLA_HEADS
    mod_map = lambda i: (i // tiles_per_batch, ((i % tiles_per_batch) >= ctx_tile0).astype(I32), 0, 0)
    row = lambda w: pl.BlockSpec((tt, w), lambda i: (i, 0))
    full = lambda a: pl.BlockSpec(a.shape, lambda i: (0,) * a.ndim)
    body = functools.partial(_gla_proj_body, qk_w=qk_w, v_w=v_w, dk=dk)
    b_a2r = b_a2.reshape(2, 1, qk_w).astype(F32)
    return pl.pallas_call(
        body,
        grid=(t // tt,),
        in_specs=[row(d), pl.BlockSpec((None, None, 6, d), mod_map),
                  full(w_main), full(w_a), full(w_a2), full(b_a2r)],
        out_specs=[row(qk_w), row(qk_w), row(v_w), row(v_w), row(qk_w), row(qk_w)],
        out_shape=[jax.ShapeDtypeStruct((t, w), F32) for w in (qk_w, qk_w, v_w, v_w, qk_w, qk_w)],
        compiler_params=_cparams(("parallel",)),
        name="gla_proj",
    )(h2d, modtab, w_main, w_a, w_a2, b_a2r)


def _gla_rec_body(qf_ref, kf_ref, vf_ref, laf_ref, qb_ref, kb_ref, vb_ref, lab_ref, of_ref, ob_ref, *s_refs,
                  ch, dk, dv, nh, nb):
    @pl.when(pl.program_id(0) == 0)
    def _():
        for s_ref in s_refs:
            s_ref[...] = jnp.zeros_like(s_ref)

    spc = qf_ref.shape[1] // ch
    for ci in range(spc):
        _gla_chunk(qf_ref, kf_ref, vf_ref, laf_ref, of_ref, s_refs[:nh], slice(ci * ch, (ci + 1) * ch),
                   rev=False, ch=ch, dk=dk, dv=dv, nh=nh, nb=nb)
        cb = spc - 1 - ci
        _gla_chunk(qb_ref, kb_ref, vb_ref, lab_ref, ob_ref, s_refs[nh:], slice(cb * ch, (cb + 1) * ch),
                   rev=True, ch=ch, dk=dk, dv=dv, nh=nh, nb=nb)


def _gla_chunk(q_ref, k_ref, v_ref, la_ref, o_ref, s_refs, rows, *, rev, ch, dk, dv, nh, nb):
    n = nb * ch
    stack = lambda ref: jnp.concatenate([ref[bi, rows, :] for bi in range(nb)], axis=0)
    r_i = lax.broadcasted_iota(I32, (n, n), 0)
    c_i = lax.broadcasted_iota(I32, (n, n), 1)
    seen = ((r_i // ch) == (c_i // ch)) & ((c_i >= r_i) if rev else (c_i <= r_i))
    end = 0 if rev else ch - 1
    k_all = stack(k_ref)
    v_all = stack(v_ref).astype(BF16)
    b = jnp.dot(seen.astype(F32), stack(la_ref), precision=HIGHEST, preferred_element_type=F32)
    b_ends = [b[bi * ch + end:bi * ch + end + 1, :] for bi in range(nb)]
    b_end = jnp.concatenate([jnp.broadcast_to(be, (ch, nh * dk)) for be in b_ends], axis=0)
    q_d_all = (stack(q_ref) * jnp.exp(b)).astype(BF16)
    k_d_all = (k_all * jnp.exp(-b)).astype(BF16)
    k_e_all = (k_all * jnp.exp(b_end - b)).astype(BF16)
    own = (lax.broadcasted_iota(I32, (n, nb * dk), 0) // ch) == (lax.broadcasted_iota(I32, (n, nb * dk), 1) // dk)
    zero = jnp.zeros((n, nb * dk), BF16)
    for hd in range(nh):
        qs = slice(hd * dk, (hd + 1) * dk)
        vs = slice(hd * dv, (hd + 1) * dv)
        q_d, k_d, k_e = q_d_all[:, qs], k_d_all[:, qs], k_e_all[:, qs]
        v = v_all[:, vs]
        att = lax.dot_general(q_d, k_d, (((1,), (1,)), ((), ())), preferred_element_type=F32)
        att = jnp.where(seen, att, 0.0).astype(BF16)
        q_bd = jnp.where(own, jnp.concatenate([q_d] * nb, axis=1), zero)
        k_bd = jnp.where(own, jnp.concatenate([k_e] * nb, axis=1), zero)
        s_t = s_refs[hd][...]
        o = (jnp.dot(att, v, preferred_element_type=F32)
             + lax.dot_general(q_bd, s_t.astype(BF16), (((1,), (1,)), ((), ())), preferred_element_type=F32))
        for bi in range(nb):
            o_ref[bi, rows, vs] = o[bi * ch:(bi + 1) * ch]
        g_row = jnp.concatenate([jnp.exp(be[:, qs]) for be in b_ends], axis=1)
        upd = lax.dot_general(v, k_bd, (((0,), (0,)), ((), ())), preferred_element_type=F32)
        s_refs[hd][...] = s_t * g_row + upd


def _gla_rec(q, k, v, la_f, la_b, *, n_lat):
    nb, ltot, qk_w = q.shape
    v_w = v.shape[-1]
    ch = GLA_CHUNK
    blk = GLA_STEP_CHUNKS * ch
    assert n_lat % blk == 0 and (ltot - n_lat) % blk == 0
    n_chunks = ltot // blk
    lat_chunks = n_lat // blk
    fwd = lambda j: (j + lat_chunks) % n_chunks
    bwd = lambda j: n_chunks - 1 - j
    spec = lambda w, chunk: pl.BlockSpec((nb, blk, w), lambda j: (0, chunk(j), 0))
    dk, dv = qk_w // GLA_HEADS, v_w // GLA_HEADS
    body = functools.partial(_gla_rec_body, ch=ch, dk=dk, dv=dv, nh=GLA_HEADS, nb=nb)
    out = jax.ShapeDtypeStruct((nb, ltot, v_w), F32)
    return pl.pallas_call(
        body,
        grid=(n_chunks,),
        in_specs=[spec(qk_w, fwd), spec(qk_w, fwd), spec(v_w, fwd), spec(qk_w, fwd),
                  spec(qk_w, bwd), spec(qk_w, bwd), spec(v_w, bwd), spec(qk_w, bwd)],
        out_specs=[spec(v_w, fwd), spec(v_w, bwd)],
        out_shape=[out, out],
        scratch_shapes=[pltpu.VMEM((dv, nb * dk), F32) for _ in range(2 * GLA_HEADS)],
        compiler_params=_cparams(("arbitrary",)),
        name="gla_rec",
    )(q, k, v, la_f, q, k, v, la_b)


def _layer_norm_rows(v, g, b):
    mu = jnp.mean(v, axis=-1, keepdims=True)
    c = v - mu
    var = jnp.mean(c * c, axis=-1, keepdims=True)
    return c * lax.rsqrt(var + LN_EPS) * g + b


def _post_mixer_body(*refs, kind, alpha, tt, dv):
    if kind == "s5":
        (h_ref, mod_ref, ya_ref, yb_ref, dsk_ref, w_ref, lng_ref, lnb_ref, wr_ref, br_ref,
         h1_ref, xr_ref, meta_ref, gt_ref, cnt_ref, base_ref) = refs
    else:
        (h_ref, mod_ref, ya_ref, yb_ref, gate_ref, ng_ref, w_ref, lng_ref, lnb_ref, wr_ref, br_ref,
         h1_ref, xr_ref, meta_ref, gt_ref, cnt_ref, base_ref) = refs
    i = pl.program_id(0)

    @pl.when(i == 0)
    def _():
        base_ref[...] = jnp.zeros_like(base_ref)

    h = h_ref[...]
    d = h.shape[-1]
    if kind == "s5":
        u = h * (1.0 + mod_ref[1:2, :]) + mod_ref[0:1, :]
        y = dsk_ref[...] * u + ya_ref[...] + yb_ref[...]
        ge = 0.5 * y * (1.0 + jnp.tanh(math.sqrt(2.0 / math.pi) * (y + 0.044715 * (y * y * y))))
        z = jnp.dot(ge.astype(BF16), w_ref[...], preferred_element_type=F32)
        mix = z[:, :d] * _sigmoid(z[:, d:])
    else:
        o = ya_ref[...] + yb_ref[...]
        parts = []
        for hd in range(d // dv):
            oh = o[:, hd * dv:(hd + 1) * dv]
            ms = jnp.mean(oh * oh, axis=-1, keepdims=True)
            parts.append(oh * lax.rsqrt(ms + LN_EPS))
        on = jnp.concatenate(parts, axis=-1) * ng_ref[...]
        gv = gate_ref[...]
        a = on * (gv * _sigmoid(gv))
        mix = jnp.dot(a.astype(BF16), w_ref[...], preferred_element_type=F32)

    h1 = _layer_norm_rows(alpha * h + mod_ref[2:3, :] * mix, lng_ref[...], lnb_ref[...])
    h1_ref[...] = h1
    u2 = h1 * (1.0 + mod_ref[4:5, :]) + mod_ref[3:4, :]
    nc = d // LANES
    for c in range(nc):
        xr_ref[pl.ds(c, tt, stride=nc), :] = u2[:, c * LANES:(c + 1) * LANES]

    logits = lax.dot_general(wr_ref[...], u2, (((1,), (1,)), ((), ())), precision=HIGHEST,
                             preferred_element_type=F32) + br_ref[...]
    n_e = logits.shape[0]
    erow = lax.broadcasted_iota(I32, logits.shape, 0)
    work = logits
    vals, idxs = [], []
    for _ in range(TOP_K):
        m = jnp.max(work, axis=0, keepdims=True)
        idx = jnp.min(jnp.where(work == m, erow, n_e), axis=0, keepdims=True)
        vals.append(m)
        idxs.append(idx)
        work = jnp.where(erow == idx, -jnp.inf, work)
    exps = [jnp.exp(v - vals[0]) for v in vals]
    den = exps[0]
    for e in exps[1:]:
        den = den + e

    multi = jnp.zeros(logits.shape, F32)
    for idx in idxs:
        multi = multi + (erow == idx).astype(F32)
    r_i = lax.broadcasted_iota(I32, (tt, tt), 0)
    c_i = lax.broadcasted_iota(I32, (tt, tt), 1)
    earlier = (r_i < c_i).astype(BF16)
    pos = jnp.dot(multi.astype(BF16), earlier, preferred_element_type=F32) + base_ref[:, 0:1]
    row8 = lax.broadcasted_iota(I32, (SUBLANES, tt), 0)
    meta = jnp.zeros((SUBLANES, tt), I32)
    gts = jnp.zeros((SUBLANES, tt), F32)
    for kk in range(TOP_K):
        rank_k = jnp.sum(jnp.where(erow == idxs[kk], pos, 0.0), axis=0, keepdims=True)
        meta = jnp.where(row8 == kk, idxs[kk], meta)
        meta = jnp.where(row8 == TOP_K + kk, rank_k.astype(I32), meta)
        gts = jnp.where(row8 == kk, exps[kk] / den, gts)
    meta_ref[...] = meta
    gt_ref[...] = jnp.transpose(
        jnp.concatenate([gts, jnp.zeros((ROUTE_LANES - SUBLANES, tt), F32)], axis=0))
    new_base = base_ref[:, 0:1] + jnp.sum(multi, axis=1, keepdims=True)
    base_ref[...] = jnp.broadcast_to(new_base, base_ref.shape)
    cnt_ref[...] = jnp.broadcast_to(new_base, cnt_ref.shape)


def _post_mixer(kind, h2d, modtab, ya, yb, extra, w_mix, ln_g, ln_b, w_router, b_router, *,
                alpha, tiles_per_batch, ctx_tile0):
    t, d = h2d.shape
    tt = TOK_TILE
    n_e = w_router.shape[-1]
    wr = w_router.astype(F32).T
    br = b_router.astype(F32).reshape(n_e, 1)
    mod_map = lambda i: (i // tiles_per_batch, ((i % tiles_per_batch) >= ctx_tile0).astype(I32), 0, 0)
    row = lambda w: pl.BlockSpec((tt, w), lambda i: (i, 0))
    full = lambda a: pl.BlockSpec(a.shape, lambda i: (0,) * a.ndim)
    vec = lambda a: a.reshape(1, -1).astype(F32)
    if kind == "s5":
        (d_skip,) = extra
        ins = [h2d, modtab, ya, yb, vec(d_skip), w_mix, vec(ln_g), vec(ln_b), wr, br]
        in_specs = [row(d), pl.BlockSpec((None, None, 6, d), mod_map), row(d), row(d)]
        in_specs += [full(a) for a in ins[4:]]
        dv = d
    else:
        gate, norm_g = extra
        dv = norm_g.shape[-1]
        ng = jnp.tile(norm_g.astype(F32), d // dv).reshape(1, d)
        ins = [h2d, modtab, ya, yb, gate, ng, w_mix, vec(ln_g), vec(ln_b), wr, br]
        in_specs = [row(d), pl.BlockSpec((None, None, 6, d), mod_map), row(d), row(d), row(d)]
        in_specs += [full(a) for a in ins[5:]]
    body = functools.partial(_post_mixer_body, kind=kind, alpha=alpha, tt=tt, dv=dv)
    return pl.pallas_call(
        body,
        grid=(t // tt,),
        in_specs=in_specs,
        out_specs=[
            row(d),
            pl.BlockSpec((tt * (d // LANES), LANES), lambda i: (i, 0)),
            pl.BlockSpec((SUBLANES, tt), lambda i: (0, i)),
            row(ROUTE_LANES),
            pl.BlockSpec((n_e, ROUTE_LANES), lambda i: (0, 0)),
        ],
        out_shape=[
            jax.ShapeDtypeStruct((t, d), F32),
            jax.ShapeDtypeStruct((t * (d // LANES), LANES), F32),
            jax.ShapeDtypeStruct((SUBLANES, t), I32),
            jax.ShapeDtypeStruct((t, ROUTE_LANES), F32),
            jax.ShapeDtypeStruct((n_e, ROUTE_LANES), F32),
        ],
        scratch_shapes=[pltpu.VMEM((n_e, ROUTE_LANES), F32)],
        compiler_params=_cparams(("arbitrary",)),
        name="post_mixer_" + kind,
    )(*ins)


def _row_copy_wait(src_hbm, dst, sem, n_rows):
    pltpu.make_async_copy(src_hbm.at[pl.ds(0, n_rows), :], dst.at[pl.ds(0, n_rows), :], sem).wait()


def _tok_rows(t, nc):
    return pl.ds(pl.multiple_of(t * nc, nc), nc)


def _dispatch_body(dst_ref, zs_ref, x_ref, xp_hbm, zbuf, sem, zsem, *, tt, nc, n_e):
    zr = zbuf.shape[0]

    @pl.when(pl.program_id(0) == 0)
    def _():
        zbuf[...] = jnp.zeros_like(zbuf)
        for e in range(2 * n_e):
            @pl.when(zs_ref[e] >= 0)
            def _():
                pltpu.make_async_copy(zbuf, xp_hbm.at[pl.ds(pl.multiple_of(zs_ref[e] * nc, nc), zr), :], zsem).start()
        for e in range(2 * n_e):
            @pl.when(zs_ref[e] >= 0)
            def _():
                pltpu.make_async_copy(zbuf, xp_hbm.at[pl.ds(pl.multiple_of(zs_ref[e] * nc, nc), zr), :], zsem).wait()

    def issue(t, c):
        for kk in range(TOP_K):
            dst = dst_ref[t * TOP_K + kk]
            pltpu.make_async_copy(x_ref.at[_tok_rows(t, nc), :], xp_hbm.at[_tok_rows(dst, nc), :],
                                  sem).start(priority=kk % 2)
        return c

    lax.fori_loop(0, tt, issue, 0, unroll=8)
    _row_copy_wait(xp_hbm, xp_hbm, sem, tt * TOP_K * nc)


def _dispatch(dest, zstart, xr, n_rows, nc):
    t = xr.shape[0] // nc
    tt = TOK_TILE
    return pl.pallas_call(
        functools.partial(_dispatch_body, tt=tt, nc=nc, n_e=N_EXPERTS),
        grid=(t // tt,),
        in_specs=[pl.BlockSpec((tt * TOP_K,), lambda i: (i,), memory_space=pltpu.SMEM),
                  pl.BlockSpec(memory_space=pltpu.SMEM),
                  pl.BlockSpec((tt * nc, LANES), lambda i: (i, 0))],
        out_specs=pl.BlockSpec(memory_space=pl.ANY),
        out_shape=jax.ShapeDtypeStruct((n_rows * nc, LANES), xr.dtype),
        scratch_shapes=[pltpu.VMEM((MOE_ROWS * nc, LANES), xr.dtype),
                        pltpu.SemaphoreType.DMA, pltpu.SemaphoreType.DMA],
        compiler_params=_cparams(("arbitrary",)),
        name="moe_dispatch",
    )(dest, zstart, xr)


def _expert_body(be_ref, nu_ref, x_ref, wu_ref, bu_ref, wd_ref, bd_ref, y_ref, wub, wdb):
    i = pl.program_id(0)
    e = be_ref[i]
    prev = be_ref[jnp.maximum(i - 1, 0)]

    @pl.when((i == 0) | (e != prev))
    def _():
        wub[...] = wu_ref[...].astype(BF16)
        wdb[...] = wd_ref[...].astype(BF16)

    nc = wu_ref.shape[0] // LANES
    r = x_ref.shape[0] // nc

    @pl.when(i < nu_ref[0])
    def _():
        x = jnp.concatenate([x_ref[pl.ds(c, r, stride=nc), :] for c in range(nc)], axis=-1).astype(BF16)
        ff = wdb.shape[0]
        y = bd_ref[...]
        for j in range(ff // FF_SLICE):
            lo, hi = j * FF_SLICE, (j + 1) * FF_SLICE
            h_glu = jnp.dot(x, wub[:, lo:hi], preferred_element_type=F32) + bu_ref[:, lo:hi]
            h_lin = jnp.dot(x, wub[:, ff + lo:ff + hi], preferred_element_type=F32) + bu_ref[:, ff + lo:ff + hi]
            h_glu = jnp.minimum(h_glu, SWIGLU_LIMIT)
            h_lin = jnp.clip(h_lin, -SWIGLU_LIMIT, SWIGLU_LIMIT)
            a = h_glu * _sigmoid(SWIGLU_ALPHA * h_glu) * (h_lin + 1.0)
            y = y + jnp.dot(a.astype(BF16), wdb[lo:hi, :], preferred_element_type=F32)
        for c in range(nc):
            y_ref[pl.ds(c, r, stride=nc), :] = y[:, c * LANES:(c + 1) * LANES]

    @pl.when(i >= nu_ref[0])
    def _():
        y_ref[...] = jnp.zeros_like(y_ref)


def _experts(blk_e, n_used, xp, layer, w_up, b_up, w_down, b_down):
    _, n_e, d, two_ff = w_up.shape
    nc = d // LANES
    n_rows = xp.shape[0] // nc
    r = MOE_ROWS
    ff = two_ff // 2
    grid_spec = pltpu.PrefetchScalarGridSpec(
        num_scalar_prefetch=2,
        grid=(n_rows // r,),
        in_specs=[
            pl.BlockSpec((r * nc, LANES), lambda i, be, nu: (jnp.minimum(i, nu[0] - 1), 0)),
            pl.BlockSpec((None, None, d, two_ff), lambda i, be, nu: (layer, be[i], 0, 0)),
            pl.BlockSpec((None, None, 1, two_ff), lambda i, be, nu: (layer, be[i], 0, 0)),
            pl.BlockSpec((None, None, ff, d), lambda i, be, nu: (layer, be[i], 0, 0)),
            pl.BlockSpec((None, None, 1, d), lambda i, be, nu: (layer, be[i], 0, 0)),
        ],
        out_specs=pl.BlockSpec((r * nc, LANES), lambda i, be, nu: (i, 0)),
        scratch_shapes=[pltpu.VMEM((d, two_ff), BF16), pltpu.VMEM((ff, d), BF16)],
    )
    return pl.pallas_call(
        _expert_body,
        grid_spec=grid_spec,
        out_shape=jax.ShapeDtypeStruct(xp.shape, F32),
        compiler_params=_cparams(("arbitrary",)),
        name="moe_experts",
    )(blk_e, n_used, xp, w_up, b_up.reshape(-1, n_e, 1, two_ff), w_down, b_down.reshape(-1, n_e, 1, d))


def _combine_body(dst_ref, gt_ref, h_ref, mod_ref, lng_ref, lnb_ref, yp_hbm,
                  o_ref, buf, sems, *, tt, nc, n_tiles, alpha):
    s = pl.program_id(0)
    slot = s % 2

    @pl.when(s < n_tiles)
    def _():
        def issue(t, c):
            for kk in range(TOP_K):
                q = t * TOP_K + kk
                src = dst_ref[q]
                pltpu.make_async_copy(yp_hbm.at[_tok_rows(src, nc), :], buf.at[slot, kk, _tok_rows(t, nc), :],
                                      sems.at[slot]).start(priority=kk % 2)
            return c

        lax.fori_loop(0, tt, issue, 0, unroll=4)

    @pl.when(s > 0)
    def _():
        prev = 1 - slot
        for kk in range(TOP_K):
            _row_copy_wait(yp_hbm, buf.at[prev, kk], sems.at[prev], tt * nc)
        gt = gt_ref[...]
        cols = []
        for c in range(nc):
            acc = gt[:, 0:1] * buf[prev, 0, pl.ds(c, tt, stride=nc), :]
            for kk in range(1, TOP_K):
                acc = acc + gt[:, kk:kk + 1] * buf[prev, kk, pl.ds(c, tt, stride=nc), :]
            cols.append(acc)
        f = jnp.concatenate(cols, axis=-1)
        o_ref[...] = _layer_norm_rows(alpha * h_ref[...] + mod_ref[5:6, :] * f, lng_ref[...], lnb_ref[...])


def _combine(dest, gates, h1, modtab, ln_g, ln_b, yp, *, alpha, tiles_per_batch, ctx_tile0):
    t, d = h1.shape
    tt = TOK_TILE
    nc = d // LANES
    n_tiles = t // tt
    nxt = lambda s: jnp.minimum(s, n_tiles - 1)
    cur = lambda s: jnp.maximum(s - 1, 0)
    smem_blk = pl.BlockSpec((tt * TOP_K,), lambda s: (nxt(s),), memory_space=pltpu.SMEM)
    mod_map = lambda s: (cur(s) // tiles_per_batch, ((cur(s) % tiles_per_batch) >= ctx_tile0).astype(I32), 0, 0)
    vec = lambda a: a.reshape(1, -1).astype(F32)
    return pl.pallas_call(
        functools.partial(_combine_body, tt=tt, nc=nc, n_tiles=n_tiles, alpha=alpha),
        grid=(n_tiles + 1,),
        in_specs=[smem_blk,
                  pl.BlockSpec((tt, ROUTE_LANES), lambda s: (cur(s), 0)),
                  pl.BlockSpec((tt, d), lambda s: (cur(s), 0)),
                  pl.BlockSpec((None, None, 6, d), mod_map),
                  pl.BlockSpec((1, d), lambda s: (0, 0)),
                  pl.BlockSpec((1, d), lambda s: (0, 0)),
                  pl.BlockSpec(memory_space=pl.ANY)],
        out_specs=pl.BlockSpec((tt, d), lambda s: (cur(s), 0)),
        out_shape=jax.ShapeDtypeStruct((t, d), F32),
        scratch_shapes=[pltpu.VMEM((2, TOP_K, tt * nc, LANES), F32), pltpu.SemaphoreType.DMA((2,))],
        compiler_params=_cparams(("arbitrary",)),
        name="moe_combine",
    )(dest, gates, h1, modtab, vec(ln_g), vec(ln_b), yp)


def _moe_layer(h1, xr, meta, gt, counts, modtab, ln_g, ln_b, layer, w_up, b_up, w_down, b_down, *,
               alpha, tiles_per_batch, ctx_tile0):
    t, d = h1.shape
    n_e = w_up.shape[1]
    r = MOE_ROWS
    n_blocks = -(-(t * TOP_K) // r) + n_e
    cnt = counts[:, 0].astype(I32)
    padded = (cnt + r - 1) // r * r
    pends = jnp.cumsum(padded)
    pstart = pends - padded
    tail = pends[-1] + jnp.arange(n_e, dtype=I32) * r
    zstart = jnp.concatenate([jnp.where(cnt > 0, pends - r, -1),
                              jnp.where(tail < n_blocks * r, tail, -1)]).astype(I32)
    blk_row0 = jnp.arange(n_blocks, dtype=I32) * r
    blk_e = jnp.minimum(jnp.sum((pends[None, :] <= blk_row0[:, None]).astype(I32), axis=1), n_e - 1)
    n_used = (pends[-1] // r).astype(I32).reshape(1)
    ti, rk = meta[:TOP_K], meta[TOP_K:2 * TOP_K]
    first = jnp.sum(jnp.where(ti[:, :, None] == jnp.arange(n_e, dtype=I32), pstart, 0), axis=-1)
    dest = (first + rk).T.reshape(-1).astype(I32)
    xp = _dispatch(dest, zstart, xr, n_blocks * r, d // LANES)
    yp = _experts(blk_e, n_used, xp, layer, w_up, b_up, w_down, b_down)
    return _combine(dest, gt, h1, modtab, ln_g, ln_b, yp,
                    alpha=alpha, tiles_per_batch=tiles_per_batch, ctx_tile0=ctx_tile0)


def _to_cm_body(lat_ref, ctx_ref, o_ref, *, rows, k, n_lat_steps):
    w = pl.program_id(1)
    for q in range(SUBLANES // k):
        @pl.when((w < n_lat_steps) & (w % (SUBLANES // k) == q))
        def _():
            for i in range(k):
                o_ref[i * rows:(i + 1) * rows, :] = lat_ref[:, q * k + i, :]

    @pl.when(w >= n_lat_steps)
    def _():
        o_ref[...] = ctx_ref[...]


def _to_col_major(h3, n_lat):
    bn, ltot, d = h3.shape
    lc = ltot - n_lat
    rows = n_lat // GRID_W
    k = max(kk for kk in (1, 2, 4, 8) if lc % (kk * rows) == 0)
    ob = k * rows
    n_lat_steps = GRID_W // k
    per_blk = SUBLANES // k
    return pl.pallas_call(
        functools.partial(_to_cm_body, rows=rows, k=k, n_lat_steps=n_lat_steps),
        grid=(bn, n_lat_steps + lc // ob),
        in_specs=[pl.BlockSpec((None, rows, SUBLANES, d),
                               lambda b, w: (b, 0, jnp.minimum(w, n_lat_steps - 1) // per_blk, 0)),
                  pl.BlockSpec((None, ob, d), lambda b, w: (b, jnp.maximum(w, n_lat_steps), 0))],
        out_specs=pl.BlockSpec((None, ob, d), lambda b, w: (b, w, 0)),
        out_shape=jax.ShapeDtypeStruct(h3.shape, h3.dtype),
        compiler_params=_cparams(("arbitrary", "arbitrary")),
        name="to_col_major",
    )(h3.reshape(bn, ltot // GRID_W, GRID_W, d), h3)


def _to_rm_body(lat_ref, ctx_ref, o_ref, *, rows, n_ctx_rows):
    w = pl.program_id(1)
    for i in range(SUBLANES):
        o_ref[0:rows, i, :] = lat_ref[i * rows:(i + 1) * rows, :]
    for m in range(n_ctx_rows):
        o_ref[rows + m, :, :] = ctx_ref[pl.ds(pl.multiple_of(m * GRID_W + w * SUBLANES, SUBLANES), SUBLANES), :]


def _to_row_major(h3, n_lat, keep_ctx=True):
    bn, ltot, d = h3.shape
    lc = ltot - n_lat
    rows = n_lat // GRID_W
    assert lc % GRID_W == 0 and n_lat % lc == 0
    n_ctx_rows = lc // GRID_W if keep_ctx else 0
    out = pl.pallas_call(
        functools.partial(_to_rm_body, rows=rows, n_ctx_rows=n_ctx_rows),
        grid=(bn, GRID_W // SUBLANES),
        in_specs=[pl.BlockSpec((None, SUBLANES * rows, d), lambda b, w: (b, w, 0)),
                  pl.BlockSpec((None, lc, d), lambda b, w: (b, n_lat // lc, 0))],
        out_specs=pl.BlockSpec((None, rows + n_ctx_rows, SUBLANES, d), lambda b, w: (b, 0, w, 0)),
        out_shape=jax.ShapeDtypeStruct((bn, rows + n_ctx_rows, GRID_W, d), h3.dtype),
        compiler_params=_cparams(("arbitrary", "arbitrary")),
        name="to_row_major",
    )(h3, h3)
    return out.reshape(bn, (rows + n_ctx_rows) * GRID_W, d)


def kernel(x, c, ctx, c_ctx, w_ada, b_ada, ln1_g, ln1_b, ln2_g, ln2_b, s5_lam_re, s5_lam_im, s5_log_step, s5_b_re, s5_b_im, s5_c_re, s5_c_im, s5_d, s5_w_glu, gla_w_in, gla_w_a2, gla_b_a2, gla_norm_g, gla_w_out, moe_w_router, moe_b_router, moe_w_up, moe_b_up, moe_w_down, moe_b_down):
    bn, l, d = x.shape
    lc = ctx.shape[1]
    depth = w_ada.shape[0]
    ltot = l + lc
    rows = l // GRID_W
    alpha = (2 * depth) ** 0.25
    assert l % TOK_TILE == 0 and lc % TOK_TILE == 0 and l % S5_CHUNK == 0 and lc % S5_CHUNK == 0
    assert bn < SUBLANES
    tiles_per_batch = ltot // TOK_TILE
    ctx_tile0 = l // TOK_TILE
    tile_kw = dict(tiles_per_batch=tiles_per_batch, ctx_tile0=ctx_tile0)

    cc = jnp.zeros((SUBLANES, d), F32).at[:bn].set(c.astype(F32)).at[bn].set(c_ctx.astype(F32))
    mod = _ada_table(cc, w_ada.astype(F32), b_ada.astype(F32))
    mod_lat = mod[:, :bn].reshape(depth, bn, 1, 6, d)
    mod_ctx = jnp.broadcast_to(mod[:, bn].reshape(depth, 1, 1, 6, d), (depth, bn, 1, 6, d))
    modtab = jnp.concatenate([mod_lat, mod_ctx], axis=2)

    to_cm = functools.partial(_to_col_major, n_lat=l)
    to_rm = functools.partial(_to_row_major, n_lat=l)

    h3 = jnp.concatenate([x.astype(F32), ctx.astype(F32)], axis=1)
    for i in range(depth):
        j = i // 2
        mt = modtab[i]
        if i % 2 == 0:
            ys = []
            for dr, rev in ((0, False), (1, True)):
                wb, wc, a_re, a_im = _s5_prepare(s5_lam_re[j, dr], s5_lam_im[j, dr], s5_log_step[j, dr],
                                                 s5_b_re[j, dr], s5_b_im[j, dr], s5_c_re[j, dr], s5_c_im[j, dr], bn)
                ys.append(_s5_scan(h3, mt, wb, wc, a_re, a_im, rev=rev, n_lat=l).reshape(bn * ltot, d))
            h2d = h3.reshape(bn * ltot, d)
            outs = _post_mixer("s5", h2d, mt, ys[0], ys[1], (s5_d[j],), s5_w_glu[j].astype(BF16),
                               ln1_g[i], ln1_b[i], moe_w_router[i], moe_b_router[i], alpha=alpha, **tile_kw)
        else:
            h3 = to_cm(h3)
            h2d = h3.reshape(bn * ltot, d)
            w_in = gla_w_in[j]
            n_main = w_in.shape[1] - 2 * GLA_GATE_RANK
            w_a = w_in[:, n_main:].reshape(d, 2, GLA_GATE_RANK).transpose(1, 0, 2).astype(BF16)
            q, k, v, g, la_f, la_b = _gla_proj(h2d, mt, w_in[:, :n_main].astype(BF16), w_a,
                                               gla_w_a2[j].astype(BF16), gla_b_a2[j], **tile_kw)
            r3 = lambda a: a.reshape(bn, ltot, a.shape[-1])
            o_f, o_b = [o.reshape(bn * ltot, d) for o in _gla_rec(r3(q), r3(k), r3(v), r3(la_f), r3(la_b), n_lat=l)]
            outs = _post_mixer("gla", h2d, mt, o_f, o_b, (g, gla_norm_g[j]), gla_w_out[j].astype(BF16),
                               ln1_g[i], ln1_b[i], moe_w_router[i], moe_b_router[i], alpha=alpha, **tile_kw)
        h1, xr, meta, gt, counts = outs
        h2 = _moe_layer(h1, xr, meta, gt, counts, mt, ln2_g[i], ln2_b[i],
                        i, moe_w_up, moe_b_up, moe_w_down, moe_b_down, alpha=alpha, **tile_kw)
        h3 = h2.reshape(bn, ltot, d)
        if i % 2 == 1:
            h3 = to_rm(h3, keep_ctx=i < depth - 1)
    return h3[:, :l].astype(x.dtype)
```

```python
import functools
import math

import jax
import jax.numpy as jnp
from jax import lax
from jax.experimental import pallas as pl
from jax.experimental.pallas import tpu as pltpu

F32 = jnp.float32
BF16 = jnp.bfloat16
I32 = jnp.int32
HIGHEST = lax.Precision.HIGHEST

GRID_W = 64
S5_GROUP = 16
S5_STATE = 64
GLA_HEADS = 4
GLA_GATE_RANK = 16
GLA_GATE_NORM = 16.0
GLA_CHUNK = 64
GLA_STEP_CHUNKS = 2
N_EXPERTS = 32
TOP_K = 4
SWIGLU_ALPHA = 1.702
SWIGLU_LIMIT = 7.0
LN_EPS = 1e-5

LANES = 128
SUBLANES = 8
MXU_K = 256
VMEM_LIMIT = 56 * 1024 * 1024

TOK_TILE = 256
S5_CHUNK = 128
S5_PITCH = S5_CHUNK + SUBLANES // 2
MOE_ROWS = 512
FF_SLICE = 1024
ROUTE_LANES = LANES


def _sigmoid(x):
    return 1.0 / (1.0 + jnp.exp(-x))


def _cparams(sem):
    return pltpu.CompilerParams(dimension_semantics=sem, vmem_limit_bytes=VMEM_LIMIT)


def _ada_body(c_ref, w_ref, b_ref, o_ref):
    c = c_ref[...]
    cond = c * _sigmoid(c)
    o_ref[...] = jnp.dot(cond, w_ref[...], precision=HIGHEST, preferred_element_type=F32) + b_ref[...]


def _ada_table(cc, w_ada, b_ada):
    depth, d, six_d = w_ada.shape
    n_tiles = six_d // d
    return pl.pallas_call(
        _ada_body,
        grid=(depth, n_tiles),
        in_specs=[
            pl.BlockSpec((SUBLANES, d), lambda i, n: (0, 0)),
            pl.BlockSpec((None, d, d), lambda i, n: (i, 0, n)),
            pl.BlockSpec((None, 1, d), lambda i, n: (i, 0, n)),
        ],
        out_specs=pl.BlockSpec((None, SUBLANES, d), lambda i, n: (i, 0, n)),
        out_shape=jax.ShapeDtypeStruct((depth, SUBLANES, six_d), F32),
        compiler_params=_cparams(("arbitrary", "arbitrary")),
        name="ada_table",
    )(cc, w_ada, b_ada.reshape(depth, 1, six_d))


def _s5_scan_body(x_ref, mod_ref, wb_ref, wc_ref, are_ref, aim_ref, y_ref,
                  sre_ref, sim_ref, st_re, st_im, u_ref, *, rev, tc, pitch, nb, n_kt, cpk, pk):
    j = pl.program_id(0)
    ncol = n_kt * cpk
    npc = ncol // pk
    half = cpk * LANES
    gr = nb * pitch

    @pl.when(j == 0)
    def _():
        st_re[...] = jnp.zeros_like(st_re)
        st_im[...] = jnp.zeros_like(st_im)
        u_ref[...] = jnp.zeros_like(u_ref)

    for b in range(nb):
        u_ref[b * pitch:b * pitch + tc, :] = x_ref[b] * (1.0 + mod_ref[b, 1:2, :]) + mod_ref[b, 0:1, :]
    u = u_ref[...].astype(BF16)
    for kt in range(n_kt):
        r = jnp.dot(u[:, MXU_K * kt:MXU_K * (kt + 1)], wb_ref[kt], preferred_element_type=F32)
        for c in range(cpk):
            col = kt * cpk + c
            grows = slice((col // npc) * gr, (col // npc + 1) * gr)
            sre_ref[col % npc, grows, :] = r[:, LANES * c:LANES * (c + 1)]
            sim_ref[col % npc, grows, :] = r[:, half + LANES * c:half + LANES * (c + 1)]

    grp = 8
    for cg in range(npc // grp):
        cols = list(range(cg * grp, (cg + 1) * grp))
        ar = [are_ref[c] for c in cols]
        ai = [aim_ref[c] for c in cols]
        init = tuple(st_re[c] for c in cols) + tuple(st_im[c] for c in cols)

        def step(t, carry, cols=cols, ar=ar, ai=ai):
            tt = (tc - 1 - t) if rev else t
            out_re, out_im = [], []
            for k, c in enumerate(cols):
                rows = pl.ds(tt, pk * nb, stride=pitch)
                pr, pi = carry[k], carry[grp + k]
                nr = ar[k] * pr - ai[k] * pi + sre_ref[c, rows, :]
                ni = ar[k] * pi + ai[k] * pr + sim_ref[c, rows, :]
                sre_ref[c, rows, :] = nr
                sim_ref[c, rows, :] = ni
                out_re.append(nr)
                out_im.append(ni)
            return tuple(out_re) + tuple(out_im)

        fin = lax.fori_loop(0, tc, step, init)
        for k, c in enumerate(cols):
            st_re[c] = fin[k]
            st_im[c] = fin[grp + k]

    def stacked(ref, kt):
        def piece(c):
            col = kt * cpk + c
            return ref[col % npc, (col // npc) * gr:(col // npc + 1) * gr, :]

        return jnp.concatenate([piece(c) for c in range(cpk)], axis=-1).astype(BF16)

    for kt in range(n_kt):
        y = (jnp.dot(stacked(sre_ref, kt), wc_ref[kt, :half, :], preferred_element_type=F32)
             + jnp.dot(stacked(sim_ref, kt), wc_ref[kt, half:, :], preferred_element_type=F32))
        for b in range(nb):
            y_ref[b, :, MXU_K * kt:MXU_K * (kt + 1)] = y[b * pitch:b * pitch + tc]


def _s5_scan(h3, modtab, wb, wc, a_re, a_im, *, rev, n_lat):
    nb, ltot, d = h3.shape
    tc = S5_CHUNK
    n_chunks = ltot // tc
    lat_chunks = n_lat // tc
    n_kt, _, two_half = wb.shape
    cpk = two_half // (2 * LANES)
    ncol = n_kt * cpk
    pk = SUBLANES // nb
    npc = ncol // pk

    if rev:
        chunk = lambda j: n_chunks - 1 - j
    else:
        chunk = lambda j: (j + lat_chunks) % n_chunks
    is_ctx = lambda j: (chunk(j) >= lat_chunks).astype(I32)

    body = functools.partial(_s5_scan_body, rev=rev, tc=tc, pitch=S5_PITCH, nb=nb, n_kt=n_kt, cpk=cpk, pk=pk)
    return pl.pallas_call(
        body,
        grid=(n_chunks,),
        in_specs=[
            pl.BlockSpec((nb, tc, d), lambda j: (0, chunk(j), 0)),
            pl.BlockSpec((nb, None, 6, d), lambda j: (0, is_ctx(j), 0, 0)),
            pl.BlockSpec(wb.shape, lambda j: (0, 0, 0)),
            pl.BlockSpec(wc.shape, lambda j: (0, 0, 0)),
            pl.BlockSpec(a_re.shape, lambda j: (0, 0, 0)),
            pl.BlockSpec(a_im.shape, lambda j: (0, 0, 0)),
        ],
        out_specs=pl.BlockSpec((nb, tc, d), lambda j: (0, chunk(j), 0)),
        out_shape=jax.ShapeDtypeStruct((nb, ltot, d), F32),
        scratch_shapes=[
            pltpu.VMEM((npc, pk * nb * S5_PITCH, LANES), F32),
            pltpu.VMEM((npc, pk * nb * S5_PITCH, LANES), F32),
            pltpu.VMEM((npc, pk * nb, LANES), F32),
            pltpu.VMEM((npc, pk * nb, LANES), F32),
            pltpu.VMEM((nb * S5_PITCH, d), F32),
        ],
        compiler_params=_cparams(("arbitrary",)),
        name="s5_scan_bwd" if rev else "s5_scan_fwd",
    )(h3, modtab, wb, wc, a_re, a_im)


def _s5_prepare(lam_re, lam_im, log_step, b_re, b_im, c_re, c_im, nb):
    g, p = lam_re.shape
    hg = b_re.shape[-1]
    lr = lam_re.astype(F32)
    li = lam_im.astype(F32)
    dt = jnp.exp(log_step.astype(F32))[:, None]
    mag = jnp.exp(lr * dt)
    ar = mag * jnp.cos(li * dt)
    ai = mag * jnp.sin(li * dt)
    den = lr * lr + li * li
    nr = ar - 1.0
    kr = (nr * lr + ai * li) / den
    ki = (ai * lr - nr * li) / den
    br = b_re.astype(F32)
    bi = b_im.astype(F32)
    bbr = kr[..., None] * br - ki[..., None] * bi
    bbi = kr[..., None] * bi + ki[..., None] * br
    gpk = MXU_K // hg
    n_kt = g // gpk
    eye = jnp.eye(gpk, dtype=F32)

    def block_diag(t):
        full = t[:, :, :, None, :] * eye[None, :, None, :, None]
        return full.reshape(n_kt, gpk * t.shape[2], gpk * t.shape[3])

    def in_blocks(bb):
        return block_diag(bb.reshape(n_kt, gpk, p, hg).transpose(0, 1, 3, 2))

    def out_blocks(cc):
        return block_diag(cc.reshape(n_kt, gpk, hg, p).transpose(0, 1, 3, 2))

    wb = jnp.concatenate([in_blocks(bbr), in_blocks(bbi)], axis=-1).astype(BF16)
    wc = jnp.concatenate([out_blocks(c_re.astype(F32)), -out_blocks(c_im.astype(F32))], axis=1).astype(BF16)
    ncol = g * p // LANES
    pk = SUBLANES // nb
    npc = ncol // pk

    def packed_rows(a):
        t = a.reshape(pk, npc, 1, LANES).transpose(1, 0, 2, 3)
        return jnp.broadcast_to(t, (npc, pk, nb, LANES)).reshape(npc, pk * nb, LANES)

    return wb, wc, packed_rows(ar), packed_rows(ai)


def _gla_proj_body(x_ref, mod_ref, w_ref, wa_ref, wa2_ref, ba2_ref,
                   q_ref, k_ref, v_ref, g_ref, laf_ref, lab_ref, *, qk_w, v_w, dk):
    u = (x_ref[...] * (1.0 + mod_ref[1:2, :]) + mod_ref[0:1, :]).astype(BF16)
    p = jnp.dot(u, w_ref[...], preferred_element_type=F32)
    q_ref[...] = p[:, :qk_w] * (dk ** -0.5)
    k_ref[...] = p[:, qk_w:2 * qk_w]
    v_ref[...] = p[:, 2 * qk_w:2 * qk_w + v_w]
    g_ref[...] = p[:, 2 * qk_w + v_w:]
    for dr, out in enumerate((laf_ref, lab_ref)):
        a_d = jnp.dot(u, wa_ref[dr], preferred_element_type=F32).astype(BF16)
        z = jnp.dot(a_d, wa2_ref[dr], preferred_element_type=F32) + ba2_ref[dr]
        out[...] = (jnp.minimum(z, 0.0) - jnp.log1p(jnp.exp(-jnp.abs(z)))) / GLA_GATE_NORM


def _gla_proj(h2d, modtab, w_main, w_a, w_a2, b_a2, *, tiles_per_batch, ctx_tile0):
    t, d = h2d.shape
    tt = TOK_TILE
    qk_w = w_a2.shape[-1]
    v_w = (w_main.shape[1] - 2 * qk_w) // 2
    dk = qk_w // GLA_HEADS
    mod_map = lambda i: (i // tiles_per_batch, ((i % tiles_per_batch) >= ctx_tile0).astype(I32), 0, 0)
    row = lambda w: pl.BlockSpec((tt, w), lambda i: (i, 0))
    full = lambda a: pl.BlockSpec(a.shape, lambda i: (0,) * a.ndim)
    body = functools.partial(_gla_proj_body, qk_w=qk_w, v_w=v_w, dk=dk)
    b_a2r = b_a2.reshape(2, 1, qk_w).astype(F32)
    return pl.pallas_call(
        body,
        grid=(t // tt,),
        in_specs=[row(d), pl.BlockSpec((None, None, 6, d), mod_map),
                  full(w_main), full(w_a), full(w_a2), full(b_a2r)],
        out_specs=[row(qk_w), row(qk_w), row(v_w), row(v_w), row(qk_w), row(qk_w)],
        out_shape=[jax.ShapeDtypeStruct((t, w), F32) for w in (qk_w, qk_w, v_w, v_w, qk_w, qk_w)],
        compiler_params=_cparams(("parallel",)),
        name="gla_proj",
    )(h2d, modtab, w_main, w_a, w_a2, b_a2r)


def _gla_rec_body(qf_ref, kf_ref, vf_ref, laf_ref, qb_ref, kb_ref, vb_ref, lab_ref, of_ref, ob_ref, *s_refs,
                  ch, dk, dv, nh, nb):
    @pl.when(pl.program_id(0) == 0)
    def _():
        for s_ref in s_refs:
            s_ref[...] = jnp.zeros_like(s_ref)

    spc = qf_ref.shape[1] // ch
    for ci in range(spc):
        _gla_chunk(qf_ref, kf_ref, vf_ref, laf_ref, of_ref, s_refs[:nh], slice(ci * ch, (ci + 1) * ch),
                   rev=False, ch=ch, dk=dk, dv=dv, nh=nh, nb=nb)
        cb = spc - 1 - ci
        _gla_chunk(qb_ref, kb_ref, vb_ref, lab_ref, ob_ref, s_refs[nh:], slice(cb * ch, (cb + 1) * ch),
                   rev=True, ch=ch, dk=dk, dv=dv, nh=nh, nb=nb)


def _gla_chunk(q_ref, k_ref, v_ref, la_ref, o_ref, s_refs, rows, *, rev, ch, dk, dv, nh, nb):
    n = nb * ch
    stack = lambda ref: jnp.concatenate([ref[bi, rows, :] for bi in range(nb)], axis=0)
    r_i = lax.broadcasted_iota(I32, (n, n), 0)
    c_i = lax.broadcasted_iota(I32, (n, n), 1)
    seen = ((r_i // ch) == (c_i // ch)) & ((c_i >= r_i) if rev else (c_i <= r_i))
    end = 0 if rev else ch - 1
    k_all = stack(k_ref)
    v_all = stack(v_ref).astype(BF16)
    b = jnp.dot(seen.astype(F32), stack(la_ref), precision=HIGHEST, preferred_element_type=F32)
    b_ends = [b[bi * ch + end:bi * ch + end + 1, :] for bi in range(nb)]
    b_end = jnp.concatenate([jnp.broadcast_to(be, (ch, nh * dk)) for be in b_ends], axis=0)
    q_d_all = (stack(q_ref) * jnp.exp(b)).astype(BF16)
    k_d_all = (k_all * jnp.exp(-b)).astype(BF16)
    k_e_all = (k_all * jnp.exp(b_end - b)).astype(BF16)
    own = (lax.broadcasted_iota(I32, (n, nb * dk), 0) // ch) == (lax.broadcasted_iota(I32, (n, nb * dk), 1) // dk)
    zero = jnp.zeros((n, nb * dk), BF16)
    for hd in range(nh):
        qs = slice(hd * dk, (hd + 1) * dk)
        vs = slice(hd * dv, (hd + 1) * dv)
        q_d, k_d, k_e = q_d_all[:, qs], k_d_all[:, qs], k_e_all[:, qs]
        v = v_all[:, vs]
        att = lax.dot_general(q_d, k_d, (((1,), (1,)), ((), ())), preferred_element_type=F32)
        att = jnp.where(seen, att, 0.0).astype(BF16)
        q_bd = jnp.where(own, jnp.concatenate([q_d] * nb, axis=1), zero)
        k_bd = jnp.where(own, jnp.concatenate([k_e] * nb, axis=1), zero)
        s_t = s_refs[hd][...]
        o = (jnp.dot(att, v, preferred_element_type=F32)
             + lax.dot_general(q_bd, s_t.astype(BF16), (((1,), (1,)), ((), ())), preferred_element_type=F32))
        for bi in range(nb):
            o_ref[bi, rows, vs] = o[bi * ch:(bi + 1) * ch]
        g_row = jnp.concatenate([jnp.exp(be[:, qs]) for be in b_ends], axis=1)
        upd = lax.dot_general(v, k_bd, (((0,), (0,)), ((), ())), preferred_element_type=F32)
        s_refs[hd][...] = s_t * g_row + upd


def _gla_rec(q, k, v, la_f, la_b, *, n_lat):
    nb, ltot, qk_w = q.shape
    v_w = v.shape[-1]
    ch = GLA_CHUNK
    blk = GLA_STEP_CHUNKS * ch
    assert n_lat % blk == 0 and (ltot - n_lat) % blk == 0
    n_chunks = ltot // blk
    lat_chunks = n_lat // blk
    fwd = lambda j: (j + lat_chunks) % n_chunks
    bwd = lambda j: n_chunks - 1 - j
    spec = lambda w, chunk: pl.BlockSpec((nb, blk, w), lambda j: (0, chunk(j), 0))
    dk, dv = qk_w // GLA_HEADS, v_w // GLA_HEADS
    body = functools.partial(_gla_rec_body, ch=ch, dk=dk, dv=dv, nh=GLA_HEADS, nb=nb)
    out = jax.ShapeDtypeStruct((nb, ltot, v_w), F32)
    return pl.pallas_call(
        body,
        grid=(n_chunks,),
        in_specs=[spec(qk_w, fwd), spec(qk_w, fwd), spec(v_w, fwd), spec(qk_w, fwd),
                  spec(qk_w, bwd), spec(qk_w, bwd), spec(v_w, bwd), spec(qk_w, bwd)],
        out_specs=[spec(v_w, fwd), spec(v_w, bwd)],
        out_shape=[out, out],
        scratch_shapes=[pltpu.VMEM((dv, nb * dk), F32) for _ in range(2 * GLA_HEADS)],
        compiler_params=_cparams(("arbitrary",)),
        name="gla_rec",
    )(q, k, v, la_f, q, k, v, la_b)


def _layer_norm_rows(v, g, b):
    mu = jnp.mean(v, axis=-1, keepdims=True)
    c = v - mu
    var = jnp.mean(c * c, axis=-1, keepdims=True)
    return c * lax.rsqrt(var + LN_EPS) * g + b


def _post_mixer_body(*refs, kind, alpha, tt, dv):
    if kind == "s5":
        (h_ref, mod_ref, ya_ref, yb_ref, dsk_ref, w_ref, lng_ref, lnb_ref, wr_ref, br_ref,
         h1_ref, xr_ref, meta_ref, gt_ref, cnt_ref, base_ref) = refs
    else:
        (h_ref, mod_ref, ya_ref, yb_ref, gate_ref, ng_ref, w_ref, lng_ref, lnb_ref, wr_ref, br_ref,
         h1_ref, xr_ref, meta_ref, gt_ref, cnt_ref, base_ref) = refs
    i = pl.program_id(0)

    @pl.when(i == 0)
    def _():
        base_ref[...] = jnp.zeros_like(base_ref)

    h = h_ref[...]
    d = h.shape[-1]
    if kind == "s5":
        u = h * (1.0 + mod_ref[1:2, :]) + mod_ref[0:1, :]
        y = dsk_ref[...] * u + ya_ref[...] + yb_ref[...]
        ge = 0.5 * y * (1.0 + jnp.tanh(math.sqrt(2.0 / math.pi) * (y + 0.044715 * (y * y * y))))
        z = jnp.dot(ge.astype(BF16), w_ref[...], preferred_element_type=F32)
        mix = z[:, :d] * _sigmoid(z[:, d:])
    else:
        o = ya_ref[...] + yb_ref[...]
        parts = []
        for hd in range(d // dv):
            oh = o[:, hd * dv:(hd + 1) * dv]
            ms = jnp.mean(oh * oh, axis=-1, keepdims=True)
            parts.append(oh * lax.rsqrt(ms + LN_EPS))
        on = jnp.concatenate(parts, axis=-1) * ng_ref[...]
        gv = gate_ref[...]
        a = on * (gv * _sigmoid(gv))
        mix = jnp.dot(a.astype(BF16), w_ref[...], preferred_element_type=F32)

    h1 = _layer_norm_rows(alpha * h + mod_ref[2:3, :] * mix, lng_ref[...], lnb_ref[...])
    h1_ref[...] = h1
    u2 = h1 * (1.0 + mod_ref[4:5, :]) + mod_ref[3:4, :]
    ncp = d // (2 * LANES)
    lo = lax.bitcast_convert_type(u2[:, :d // 2].astype(BF16).astype(F32), jnp.uint32)
    hi = lax.bitcast_convert_type(u2[:, d // 2:].astype(BF16).astype(F32), jnp.uint32)
    words = (lo >> 16) | (hi & jnp.uint32(0xFFFF0000))
    for c in range(ncp):
        xr_ref[pl.ds(c, tt, stride=ncp), :] = words[:, c * LANES:(c + 1) * LANES]

    logits = lax.dot_general(wr_ref[...], u2, (((1,), (1,)), ((), ())), precision=HIGHEST,
                             preferred_element_type=F32) + br_ref[...]
    n_e = logits.shape[0]
    erow = lax.broadcasted_iota(I32, logits.shape, 0)
    work = logits
    vals, idxs = [], []
    for _ in range(TOP_K):
        m = jnp.max(work, axis=0, keepdims=True)
        idx = jnp.min(jnp.where(work == m, erow, n_e), axis=0, keepdims=True)
        vals.append(m)
        idxs.append(idx)
        work = jnp.where(erow == idx, -jnp.inf, work)
    exps = [jnp.exp(v - vals[0]) for v in vals]
    den = exps[0]
    for e in exps[1:]:
        den = den + e

    multi = jnp.zeros(logits.shape, F32)
    for idx in idxs:
        multi = multi + (erow == idx).astype(F32)
    r_i = lax.broadcasted_iota(I32, (tt, tt), 0)
    c_i = lax.broadcasted_iota(I32, (tt, tt), 1)
    earlier = (r_i < c_i).astype(BF16)
    pos = jnp.dot(multi.astype(BF16), earlier, preferred_element_type=F32) + base_ref[:, 0:1]
    row8 = lax.broadcasted_iota(I32, (SUBLANES, tt), 0)
    meta = jnp.zeros((SUBLANES, tt), I32)
    gts = jnp.zeros((SUBLANES, tt), F32)
    for kk in range(TOP_K):
        rank_k = jnp.sum(jnp.where(erow == idxs[kk], pos, 0.0), axis=0, keepdims=True)
        meta = jnp.where(row8 == kk, idxs[kk], meta)
        meta = jnp.where(row8 == TOP_K + kk, rank_k.astype(I32), meta)
        gts = jnp.where(row8 == kk, exps[kk] / den, gts)
    meta_ref[...] = meta
    gt_ref[...] = jnp.transpose(
        jnp.concatenate([gts, jnp.zeros((ROUTE_LANES - SUBLANES, tt), F32)], axis=0))
    new_base = base_ref[:, 0:1] + jnp.sum(multi, axis=1, keepdims=True)
    base_ref[...] = jnp.broadcast_to(new_base, base_ref.shape)
    cnt_ref[...] = jnp.broadcast_to(new_base, cnt_ref.shape)


def _post_mixer(kind, h2d, modtab, ya, yb, extra, w_mix, ln_g, ln_b, w_router, b_router, *,
                alpha, tiles_per_batch, ctx_tile0):
    t, d = h2d.shape
    tt = TOK_TILE
    n_e = w_router.shape[-1]
    wr = w_router.astype(F32).T
    br = b_router.astype(F32).reshape(n_e, 1)
    mod_map = lambda i: (i // tiles_per_batch, ((i % tiles_per_batch) >= ctx_tile0).astype(I32), 0, 0)
    row = lambda w: pl.BlockSpec((tt, w), lambda i: (i, 0))
    full = lambda a: pl.BlockSpec(a.shape, lambda i: (0,) * a.ndim)
    vec = lambda a: a.reshape(1, -1).astype(F32)
    if kind == "s5":
        (d_skip,) = extra
        ins = [h2d, modtab, ya, yb, vec(d_skip), w_mix, vec(ln_g), vec(ln_b), wr, br]
        in_specs = [row(d), pl.BlockSpec((None, None, 6, d), mod_map), row(d), row(d)]
        in_specs += [full(a) for a in ins[4:]]
        dv = d
    else:
        gate, norm_g = extra
        dv = norm_g.shape[-1]
        ng = jnp.tile(norm_g.astype(F32), d // dv).reshape(1, d)
        ins = [h2d, modtab, ya, yb, gate, ng, w_mix, vec(ln_g), vec(ln_b), wr, br]
        in_specs = [row(d), pl.BlockSpec((None, None, 6, d), mod_map), row(d), row(d), row(d)]
        in_specs += [full(a) for a in ins[5:]]
    body = functools.partial(_post_mixer_body, kind=kind, alpha=alpha, tt=tt, dv=dv)
    return pl.pallas_call(
        body,
        grid=(t // tt,),
        in_specs=in_specs,
        out_specs=[
            row(d),
            pl.BlockSpec((tt * (d // (2 * LANES)), LANES), lambda i: (i, 0)),
            pl.BlockSpec((SUBLANES, tt), lambda i: (0, i)),
            row(ROUTE_LANES),
            pl.BlockSpec((n_e, ROUTE_LANES), lambda i: (0, 0)),
        ],
        out_shape=[
            jax.ShapeDtypeStruct((t, d), F32),
            jax.ShapeDtypeStruct((t * (d // (2 * LANES)), LANES), jnp.uint32),
            jax.ShapeDtypeStruct((SUBLANES, t), I32),
            jax.ShapeDtypeStruct((t, ROUTE_LANES), F32),
            jax.ShapeDtypeStruct((n_e, ROUTE_LANES), F32),
        ],
        scratch_shapes=[pltpu.VMEM((n_e, ROUTE_LANES), F32)],
        compiler_params=_cparams(("arbitrary",)),
        name="post_mixer_" + kind,
    )(*ins)


def _row_copy_wait(src_hbm, dst, sem, n_rows):
    pltpu.make_async_copy(src_hbm.at[pl.ds(0, n_rows), :], dst.at[pl.ds(0, n_rows), :], sem).wait()


def _tok_rows(t, nc):
    return pl.ds(pl.multiple_of(t * nc, nc), nc)


def _dispatch_body(dst_ref, zs_ref, x_ref, xp_hbm, zbuf, sem, zsem, *, tt, nc, n_e):
    zr = zbuf.shape[0]

    @pl.when(pl.program_id(0) == 0)
    def _():
        zbuf[...] = jnp.zeros_like(zbuf)
        for e in range(2 * n_e):
            @pl.when(zs_ref[e] >= 0)
            def _():
                pltpu.make_async_copy(zbuf, xp_hbm.at[pl.ds(pl.multiple_of(zs_ref[e] * nc, nc), zr), :], zsem).start()
        for e in range(2 * n_e):
            @pl.when(zs_ref[e] >= 0)
            def _():
                pltpu.make_async_copy(zbuf, xp_hbm.at[pl.ds(pl.multiple_of(zs_ref[e] * nc, nc), zr), :], zsem).wait()

    def issue(t, c):
        for kk in range(TOP_K):
            dst = dst_ref[t * TOP_K + kk]
            pltpu.make_async_copy(x_ref.at[_tok_rows(t, nc), :], xp_hbm.at[_tok_rows(dst, nc), :],
                                  sem).start(priority=kk % 2)
        return c

    lax.fori_loop(0, tt, issue, 0, unroll=8)
    _row_copy_wait(xp_hbm, xp_hbm, sem, tt * TOP_K * nc)


def _dispatch(dest, zstart, xr, n_rows, nc):
    t = xr.shape[0] // nc
    tt = TOK_TILE
    return pl.pallas_call(
        functools.partial(_dispatch_body, tt=tt, nc=nc, n_e=N_EXPERTS),
        grid=(t // tt,),
        in_specs=[pl.BlockSpec((tt * TOP_K,), lambda i: (i,), memory_space=pltpu.SMEM),
                  pl.BlockSpec(memory_space=pltpu.SMEM),
                  pl.BlockSpec((tt * nc, LANES), lambda i: (i, 0))],
        out_specs=pl.BlockSpec(memory_space=pl.ANY),
        out_shape=jax.ShapeDtypeStruct((n_rows * nc, LANES), xr.dtype),
        scratch_shapes=[pltpu.VMEM((MOE_ROWS * nc, LANES), xr.dtype),
                        pltpu.SemaphoreType.DMA, pltpu.SemaphoreType.DMA],
        compiler_params=_cparams(("arbitrary",)),
        name="moe_dispatch",
    )(dest, zstart, xr)


def _expert_body(be_ref, nu_ref, x_ref, wu_ref, bu_ref, wd_ref, bd_ref, y_ref, wub, wdb):
    i = pl.program_id(0)
    e = be_ref[i]
    prev = be_ref[jnp.maximum(i - 1, 0)]

    @pl.when((i == 0) | (e != prev))
    def _():
        wub[...] = wu_ref[...].astype(BF16)
        wdb[...] = wd_ref[...].astype(BF16)

    nc = wu_ref.shape[0] // LANES
    ncp = nc // 2
    r = y_ref.shape[0] // nc

    @pl.when(i < nu_ref[0])
    def _():
        words = [x_ref[pl.ds(c, r, stride=ncp), :] for c in range(ncp)]
        x = jnp.concatenate(
            [lax.bitcast_convert_type(w << 16, F32) for w in words]
            + [lax.bitcast_convert_type(w & jnp.uint32(0xFFFF0000), F32) for w in words], axis=-1).astype(BF16)
        ff = wdb.shape[0]
        y = bd_ref[...]
        for j in range(ff // FF_SLICE):
            lo, hi = j * FF_SLICE, (j + 1) * FF_SLICE
            h_glu = jnp.dot(x, wub[:, lo:hi], preferred_element_type=F32) + bu_ref[:, lo:hi]
            h_lin = jnp.dot(x, wub[:, ff + lo:ff + hi], preferred_element_type=F32) + bu_ref[:, ff + lo:ff + hi]
            h_glu = jnp.minimum(h_glu, SWIGLU_LIMIT)
            h_lin = jnp.clip(h_lin, -SWIGLU_LIMIT, SWIGLU_LIMIT)
            a = h_glu * _sigmoid(SWIGLU_ALPHA * h_glu) * (h_lin + 1.0)
            y = y + jnp.dot(a.astype(BF16), wdb[lo:hi, :], preferred_element_type=F32)
        for c in range(nc):
            y_ref[pl.ds(c, r, stride=nc), :] = y[:, c * LANES:(c + 1) * LANES]

    @pl.when(i >= nu_ref[0])
    def _():
        y_ref[...] = jnp.zeros_like(y_ref)


def _experts(blk_e, n_used, xp, layer, w_up, b_up, w_down, b_down):
    _, n_e, d, two_ff = w_up.shape
    nc = d // LANES
    ncp = nc // 2
    n_rows = xp.shape[0] // ncp
    r = MOE_ROWS
    ff = two_ff // 2
    grid_spec = pltpu.PrefetchScalarGridSpec(
        num_scalar_prefetch=2,
        grid=(n_rows // r,),
        in_specs=[
            pl.BlockSpec((r * ncp, LANES), lambda i, be, nu: (jnp.minimum(i, nu[0] - 1), 0)),
            pl.BlockSpec((None, None, d, two_ff), lambda i, be, nu: (layer, be[i], 0, 0)),
            pl.BlockSpec((None, None, 1, two_ff), lambda i, be, nu: (layer, be[i], 0, 0)),
            pl.BlockSpec((None, None, ff, d), lambda i, be, nu: (layer, be[i], 0, 0)),
            pl.BlockSpec((None, None, 1, d), lambda i, be, nu: (layer, be[i], 0, 0)),
        ],
        out_specs=pl.BlockSpec((r * nc, LANES), lambda i, be, nu: (i, 0)),
        scratch_shapes=[pltpu.VMEM((d, two_ff), BF16), pltpu.VMEM((ff, d), BF16)],
    )
    return pl.pallas_call(
        _expert_body,
        grid_spec=grid_spec,
        out_shape=jax.ShapeDtypeStruct((n_rows * nc, LANES), F32),
        compiler_params=_cparams(("arbitrary",)),
        name="moe_experts",
    )(blk_e, n_used, xp, w_up, b_up.reshape(-1, n_e, 1, two_ff), w_down, b_down.reshape(-1, n_e, 1, d))


def _combine_body(dst_ref, gt_ref, h_ref, mod_ref, lng_ref, lnb_ref, yp_hbm,
                  o_ref, buf, sems, *, tt, nc, n_tiles, alpha):
    s = pl.program_id(0)
    slot = s % 2

    @pl.when(s < n_tiles)
    def _():
        def issue(t, c):
            for kk in range(TOP_K):
                q = t * TOP_K + kk
                src = dst_ref[q]
                pltpu.make_async_copy(yp_hbm.at[_tok_rows(src, nc), :], buf.at[slot, kk, _tok_rows(t, nc), :],
                                      sems.at[slot]).start(priority=kk % 2)
            return c

        lax.fori_loop(0, tt, issue, 0, unroll=4)

    @pl.when(s > 0)
    def _():
        prev = 1 - slot
        for kk in range(TOP_K):
            _row_copy_wait(yp_hbm, buf.at[prev, kk], sems.at[prev], tt * nc)
        gt = gt_ref[...]
        cols = []
        for c in range(nc):
            acc = gt[:, 0:1] * buf[prev, 0, pl.ds(c, tt, stride=nc), :]
            for kk in range(1, TOP_K):
                acc = acc + gt[:, kk:kk + 1] * buf[prev, kk, pl.ds(c, tt, stride=nc), :]
            cols.append(acc)
        f = jnp.concatenate(cols, axis=-1)
        o_ref[...] = _layer_norm_rows(alpha * h_ref[...] + mod_ref[5:6, :] * f, lng_ref[...], lnb_ref[...])


def _combine(dest, gates, h1, modtab, ln_g, ln_b, yp, *, alpha, tiles_per_batch, ctx_tile0):
    t, d = h1.shape
    tt = TOK_TILE
    nc = d // LANES
    n_tiles = t // tt
    nxt = lambda s: jnp.minimum(s, n_tiles - 1)
    cur = lambda s: jnp.maximum(s - 1, 0)
    smem_blk = pl.BlockSpec((tt * TOP_K,), lambda s: (nxt(s),), memory_space=pltpu.SMEM)
    mod_map = lambda s: (cur(s) // tiles_per_batch, ((cur(s) % tiles_per_batch) >= ctx_tile0).astype(I32), 0, 0)
    vec = lambda a: a.reshape(1, -1).astype(F32)
    return pl.pallas_call(
        functools.partial(_combine_body, tt=tt, nc=nc, n_tiles=n_tiles, alpha=alpha),
        grid=(n_tiles + 1,),
        in_specs=[smem_blk,
                  pl.BlockSpec((tt, ROUTE_LANES), lambda s: (cur(s), 0)),
                  pl.BlockSpec((tt, d), lambda s: (cur(s), 0)),
                  pl.BlockSpec((None, None, 6, d), mod_map),
                  pl.BlockSpec((1, d), lambda s: (0, 0)),
                  pl.BlockSpec((1, d), lambda s: (0, 0)),
                  pl.BlockSpec(memory_space=pl.ANY)],
        out_specs=pl.BlockSpec((tt, d), lambda s: (cur(s), 0)),
        out_shape=jax.ShapeDtypeStruct((t, d), F32),
        scratch_shapes=[pltpu.VMEM((2, TOP_K, tt * nc, LANES), F32), pltpu.SemaphoreType.DMA((2,))],
        compiler_params=_cparams(("arbitrary",)),
        name="moe_combine",
    )(dest, gates, h1, modtab, vec(ln_g), vec(ln_b), yp)


def _moe_layer(h1, xr, meta, gt, counts, modtab, ln_g, ln_b, layer, w_up, b_up, w_down, b_down, *,
               alpha, tiles_per_batch, ctx_tile0):
    t, d = h1.shape
    n_e = w_up.shape[1]
    r = MOE_ROWS
    n_blocks = -(-(t * TOP_K) // r) + n_e
    cnt = counts[:, 0].astype(I32)
    padded = (cnt + r - 1) // r * r
    pends = jnp.cumsum(padded)
    pstart = pends - padded
    tail = pends[-1] + jnp.arange(n_e, dtype=I32) * r
    zstart = jnp.concatenate([jnp.where(cnt > 0, pends - r, -1),
                              jnp.where(tail < n_blocks * r, tail, -1)]).astype(I32)
    blk_row0 = jnp.arange(n_blocks, dtype=I32) * r
    blk_e = jnp.minimum(jnp.sum((pends[None, :] <= blk_row0[:, None]).astype(I32), axis=1), n_e - 1)
    n_used = (pends[-1] // r).astype(I32).reshape(1)
    ti, rk = meta[:TOP_K], meta[TOP_K:2 * TOP_K]
    first = jnp.sum(jnp.where(ti[:, :, None] == jnp.arange(n_e, dtype=I32), pstart, 0), axis=-1)
    dest = (first + rk).T.reshape(-1).astype(I32)
    xp = _dispatch(dest, zstart, xr, n_blocks * r, d // (2 * LANES))
    yp = _experts(blk_e, n_used, xp, layer, w_up, b_up, w_down, b_down)
    return _combine(dest, gt, h1, modtab, ln_g, ln_b, yp,
                    alpha=alpha, tiles_per_batch=tiles_per_batch, ctx_tile0=ctx_tile0)


def _to_cm_body(lat_ref, ctx_ref, o_ref, *, rows, k, n_lat_steps):
    w = pl.program_id(1)
    for q in range(SUBLANES // k):
        @pl.when((w < n_lat_steps) & (w % (SUBLANES // k) == q))
        def _():
            for i in range(k):
                o_ref[i * rows:(i + 1) * rows, :] = lat_ref[:, q * k + i, :]

    @pl.when(w >= n_lat_steps)
    def _():
        o_ref[...] = ctx_ref[...]


def _to_col_major(h3, n_lat):
    bn, ltot, d = h3.shape
    lc = ltot - n_lat
    rows = n_lat // GRID_W
    k = max(kk for kk in (1, 2, 4, 8) if lc % (kk * rows) == 0)
    ob = k * rows
    n_lat_steps = GRID_W // k
    per_blk = SUBLANES // k
    return pl.pallas_call(
        functools.partial(_to_cm_body, rows=rows, k=k, n_lat_steps=n_lat_steps),
        grid=(bn, n_lat_steps + lc // ob),
        in_specs=[pl.BlockSpec((None, rows, SUBLANES, d),
                               lambda b, w: (b, 0, jnp.minimum(w, n_lat_steps - 1) // per_blk, 0)),
                  pl.BlockSpec((None, ob, d), lambda b, w: (b, jnp.maximum(w, n_lat_steps), 0))],
        out_specs=pl.BlockSpec((None, ob, d), lambda b, w: (b, w, 0)),
        out_shape=jax.ShapeDtypeStruct(h3.shape, h3.dtype),
        compiler_params=_cparams(("arbitrary", "arbitrary")),
        name="to_col_major",
    )(h3.reshape(bn, ltot // GRID_W, GRID_W, d), h3)


def _to_rm_body(lat_ref, ctx_ref, o_ref, *, rows, n_ctx_rows):
    w = pl.program_id(1)
    for i in range(SUBLANES):
        o_ref[0:rows, i, :] = lat_ref[i * rows:(i + 1) * rows, :]
    for m in range(n_ctx_rows):
        o_ref[rows + m, :, :] = ctx_ref[pl.ds(pl.multiple_of(m * GRID_W + w * SUBLANES, SUBLANES), SUBLANES), :]


def _to_row_major(h3, n_lat, keep_ctx=True):
    bn, ltot, d = h3.shape
    lc = ltot - n_lat
    rows = n_lat // GRID_W
    assert lc % GRID_W == 0 and n_lat % lc == 0
    n_ctx_rows = lc // GRID_W if keep_ctx else 0
    out = pl.pallas_call(
        functools.partial(_to_rm_body, rows=rows, n_ctx_rows=n_ctx_rows),
        grid=(bn, GRID_W // SUBLANES),
        in_specs=[pl.BlockSpec((None, SUBLANES * rows, d), lambda b, w: (b, w, 0)),
                  pl.BlockSpec((None, lc, d), lambda b, w: (b, n_lat // lc, 0))],
        out_specs=pl.BlockSpec((None, rows + n_ctx_rows, SUBLANES, d), lambda b, w: (b, 0, w, 0)),
        out_shape=jax.ShapeDtypeStruct((bn, rows + n_ctx_rows, GRID_W, d), h3.dtype),
        compiler_params=_cparams(("arbitrary", "arbitrary")),
        name="to_row_major",
    )(h3, h3)
    return out.reshape(bn, (rows + n_ctx_rows) * GRID_W, d)


def kernel(x, c, ctx, c_ctx, w_ada, b_ada, ln1_g, ln1_b, ln2_g, ln2_b, s5_lam_re, s5_lam_im, s5_log_step, s5_b_re, s5_b_im, s5_c_re, s5_c_im, s5_d, s5_w_glu, gla_w_in, gla_w_a2, gla_b_a2, gla_norm_g, gla_w_out, moe_w_router, moe_b_router, moe_w_up, moe_b_up, moe_w_down, moe_b_down):
    bn, l, d = x.shape
    lc = ctx.shape[1]
    depth = w_ada.shape[0]
    ltot = l + lc
    rows = l // GRID_W
    alpha = (2 * depth) ** 0.25
    assert l % TOK_TILE == 0 and lc % TOK_TILE == 0 and l % S5_CHUNK == 0 and lc % S5_CHUNK == 0
    assert bn < SUBLANES
    tiles_per_batch = ltot // TOK_TILE
    ctx_tile0 = l // TOK_TILE
    tile_kw = dict(tiles_per_batch=tiles_per_batch, ctx_tile0=ctx_tile0)

    cc = jnp.zeros((SUBLANES, d), F32).at[:bn].set(c.astype(F32)).at[bn].set(c_ctx.astype(F32))
    mod = _ada_table(cc, w_ada.astype(F32), b_ada.astype(F32))
    mod_lat = mod[:, :bn].reshape(depth, bn, 1, 6, d)
    mod_ctx = jnp.broadcast_to(mod[:, bn].reshape(depth, 1, 1, 6, d), (depth, bn, 1, 6, d))
    modtab = jnp.concatenate([mod_lat, mod_ctx], axis=2)

    to_cm = functools.partial(_to_col_major, n_lat=l)
    to_rm = functools.partial(_to_row_major, n_lat=l)

    h3 = jnp.concatenate([x.astype(F32), ctx.astype(F32)], axis=1)
    for i in range(depth):
        j = i // 2
        mt = modtab[i]
        if i % 2 == 0:
            ys = []
            for dr, rev in ((0, False), (1, True)):
                wb, wc, a_re, a_im = _s5_prepare(s5_lam_re[j, dr], s5_lam_im[j, dr], s5_log_step[j, dr],
                                                 s5_b_re[j, dr], s5_b_im[j, dr], s5_c_re[j, dr], s5_c_im[j, dr], bn)
                ys.append(_s5_scan(h3, mt, wb, wc, a_re, a_im, rev=rev, n_lat=l).reshape(bn * ltot, d))
            h2d = h3.reshape(bn * ltot, d)
            outs = _post_mixer("s5", h2d, mt, ys[0], ys[1], (s5_d[j],), s5_w_glu[j].astype(BF16),
                               ln1_g[i], ln1_b[i], moe_w_router[i], moe_b_router[i], alpha=alpha, **tile_kw)
        else:
            h3 = to_cm(h3)
            h2d = h3.reshape(bn * ltot, d)
            w_in = gla_w_in[j]
            n_main = w_in.shape[1] - 2 * GLA_GATE_RANK
            w_a = w_in[:, n_main:].reshape(d, 2, GLA_GATE_RANK).transpose(1, 0, 2).astype(BF16)
            q, k, v, g, la_f, la_b = _gla_proj(h2d, mt, w_in[:, :n_main].astype(BF16), w_a,
                                               gla_w_a2[j].astype(BF16), gla_b_a2[j], **tile_kw)
            r3 = lambda a: a.reshape(bn, ltot, a.shape[-1])
            o_f, o_b = [o.reshape(bn * ltot, d) for o in _gla_rec(r3(q), r3(k), r3(v), r3(la_f), r3(la_b), n_lat=l)]
            outs = _post_mixer("gla", h2d, mt, o_f, o_b, (g, gla_norm_g[j]), gla_w_out[j].astype(BF16),
                               ln1_g[i], ln1_b[i], moe_w_router[i], moe_b_router[i], alpha=alpha, **tile_kw)
        h1, xr, meta, gt, counts = outs
        h2 = _moe_layer(h1, xr, meta, gt, counts, mt, ln2_g[i], ln2_b[i],
                        i, moe_w_up, moe_b_up, moe_w_down, moe_b_down, alpha=alpha, **tile_kw)
        h3 = h2.reshape(bn, ltot, d)
        if i % 2 == 1:
            h3 = to_rm(h3, keep_ctx=i < depth - 1)
    return h3[:, :l].astype(x.dtype)
```

```python
import functools
import math

import jax
import jax.numpy as jnp
from jax import lax
from jax.experimental import pallas as pl
from jax.experimental.pallas import tpu as pltpu

F32 = jnp.float32
BF16 = jnp.bfloat16
I32 = jnp.int32
HIGHEST = lax.Precision.HIGHEST

GRID_W = 64
S5_GROUP = 16
S5_STATE = 64
GLA_HEADS = 4
GLA_GATE_RANK = 16
GLA_GATE_NORM = 16.0
GLA_CHUNK = 64
GLA_STEP_CHUNKS = 2
N_EXPERTS = 32
TOP_K = 4
SWIGLU_ALPHA = 1.702
SWIGLU_LIMIT = 7.0
LN_EPS = 1e-5

LANES = 128
SUBLANES = 8
MXU_K = 256
VMEM_LIMIT = 56 * 1024 * 1024

TOK_TILE = 256
S5_CHUNK = 128
S5_PITCH = S5_CHUNK + SUBLANES // 2
MOE_ROWS = 512
FF_SLICE = 1024
ROUTE_LANES = LANES


def _sigmoid(x):
    return 1.0 / (1.0 + jnp.exp(-x))


def _cparams(sem):
    return pltpu.CompilerParams(dimension_semantics=sem, vmem_limit_bytes=VMEM_LIMIT)


def _ada_body(c_ref, w_ref, b_ref, o_ref):
    c = c_ref[...]
    cond = c * _sigmoid(c)
    o_ref[...] = jnp.dot(cond, w_ref[...], precision=HIGHEST, preferred_element_type=F32) + b_ref[...]


def _ada_table(cc, w_ada, b_ada):
    depth, d, six_d = w_ada.shape
    n_tiles = six_d // d
    return pl.pallas_call(
        _ada_body,
        grid=(depth, n_tiles),
        in_specs=[
            pl.BlockSpec((SUBLANES, d), lambda i, n: (0, 0)),
            pl.BlockSpec((None, d, d), lambda i, n: (i, 0, n)),
            pl.BlockSpec((None, 1, d), lambda i, n: (i, 0, n)),
        ],
        out_specs=pl.BlockSpec((None, SUBLANES, d), lambda i, n: (i, 0, n)),
        out_shape=jax.ShapeDtypeStruct((depth, SUBLANES, six_d), F32),
        compiler_params=_cparams(("arbitrary", "arbitrary")),
        name="ada_table",
    )(cc, w_ada, b_ada.reshape(depth, 1, six_d))


def _s5_scan_body(x_ref, mod_ref, wb_ref, wc_ref, are_ref, aim_ref, y_ref,
                  sre_ref, sim_ref, st_re, st_im, u_ref, *, rev, tc, pitch, nb, n_kt, cpk, pk):
    j = pl.program_id(0)
    ncol = n_kt * cpk
    npc = ncol // pk
    half = cpk * LANES
    gr = nb * pitch

    @pl.when(j == 0)
    def _():
        st_re[...] = jnp.zeros_like(st_re)
        st_im[...] = jnp.zeros_like(st_im)
        u_ref[...] = jnp.zeros_like(u_ref)

    for b in range(nb):
        u_ref[b * pitch:b * pitch + tc, :] = x_ref[b] * (1.0 + mod_ref[b, 1:2, :]) + mod_ref[b, 0:1, :]
    u = u_ref[...].astype(BF16)
    for kt in range(n_kt):
        r = jnp.dot(u[:, MXU_K * kt:MXU_K * (kt + 1)], wb_ref[kt], preferred_element_type=F32)
        for c in range(cpk):
            col = kt * cpk + c
            grows = slice((col // npc) * gr, (col // npc + 1) * gr)
            sre_ref[col % npc, grows, :] = r[:, LANES * c:LANES * (c + 1)]
            sim_ref[col % npc, grows, :] = r[:, half + LANES * c:half + LANES * (c + 1)]

    grp = 8
    for cg in range(npc // grp):
        cols = list(range(cg * grp, (cg + 1) * grp))
        ar = [are_ref[c] for c in cols]
        ai = [aim_ref[c] for c in cols]
        init = tuple(st_re[c] for c in cols) + tuple(st_im[c] for c in cols)

        def step(t, carry, cols=cols, ar=ar, ai=ai):
            tt = (tc - 1 - t) if rev else t
            out_re, out_im = [], []
            for k, c in enumerate(cols):
                rows = pl.ds(tt, pk * nb, stride=pitch)
                pr, pi = carry[k], carry[grp + k]
                nr = ar[k] * pr - ai[k] * pi + sre_ref[c, rows, :]
                ni = ar[k] * pi + ai[k] * pr + sim_ref[c, rows, :]
                sre_ref[c, rows, :] = nr
                sim_ref[c, rows, :] = ni
                out_re.append(nr)
                out_im.append(ni)
            return tuple(out_re) + tuple(out_im)

        fin = lax.fori_loop(0, tc, step, init)
        for k, c in enumerate(cols):
            st_re[c] = fin[k]
            st_im[c] = fin[grp + k]

    def stacked(ref, kt):
        def piece(c):
            col = kt * cpk + c
            return ref[col % npc, (col // npc) * gr:(col // npc + 1) * gr, :]

        return jnp.concatenate([piece(c) for c in range(cpk)], axis=-1).astype(BF16)

    for kt in range(n_kt):
        y = (jnp.dot(stacked(sre_ref, kt), wc_ref[kt, :half, :], preferred_element_type=F32)
             + jnp.dot(stacked(sim_ref, kt), wc_ref[kt, half:, :], preferred_element_type=F32))
        for b in range(nb):
            y_ref[b, :, MXU_K * kt:MXU_K * (kt + 1)] = y[b * pitch:b * pitch + tc]


def _s5_scan(h3, modtab, wb, wc, a_re, a_im, *, rev, n_lat):
    nb, ltot, d = h3.shape
    tc = S5_CHUNK
    n_chunks = ltot // tc
    lat_chunks = n_lat // tc
    n_kt, _, two_half = wb.shape
    cpk = two_half // (2 * LANES)
    ncol = n_kt * cpk
    pk = SUBLANES // nb
    npc = ncol // pk

    if rev:
        chunk = lambda j: n_chunks - 1 - j
    else:
        chunk = lambda j: (j + lat_chunks) % n_chunks
    is_ctx = lambda j: (chunk(j) >= lat_chunks).astype(I32)

    body = functools.partial(_s5_scan_body, rev=rev, tc=tc, pitch=S5_PITCH, nb=nb, n_kt=n_kt, cpk=cpk, pk=pk)
    return pl.pallas_call(
        body,
        grid=(n_chunks,),
        in_specs=[
            pl.BlockSpec((nb, tc, d), lambda j: (0, chunk(j), 0)),
            pl.BlockSpec((nb, None, 6, d), lambda j: (0, is_ctx(j), 0, 0)),
            pl.BlockSpec(wb.shape, lambda j: (0, 0, 0)),
            pl.BlockSpec(wc.shape, lambda j: (0, 0, 0)),
            pl.BlockSpec(a_re.shape, lambda j: (0, 0, 0)),
            pl.BlockSpec(a_im.shape, lambda j: (0, 0, 0)),
        ],
        out_specs=pl.BlockSpec((nb, tc, d), lambda j: (0, chunk(j), 0)),
        out_shape=jax.ShapeDtypeStruct((nb, ltot, d), F32),
        scratch_shapes=[
            pltpu.VMEM((npc, pk * nb * S5_PITCH, LANES), F32),
            pltpu.VMEM((npc, pk * nb * S5_PITCH, LANES), F32),
            pltpu.VMEM((npc, pk * nb, LANES), F32),
            pltpu.VMEM((npc, pk * nb, LANES), F32),
            pltpu.VMEM((nb * S5_PITCH, d), F32),
        ],
        compiler_params=_cparams(("arbitrary",)),
        name="s5_scan_bwd" if rev else "s5_scan_fwd",
    )(h3, modtab, wb, wc, a_re, a_im)


def _s5_prepare(lam_re, lam_im, log_step, b_re, b_im, c_re, c_im, nb):
    g, p = lam_re.shape
    hg = b_re.shape[-1]
    lr = lam_re.astype(F32)
    li = lam_im.astype(F32)
    dt = jnp.exp(log_step.astype(F32))[:, None]
    mag = jnp.exp(lr * dt)
    ar = mag * jnp.cos(li * dt)
    ai = mag * jnp.sin(li * dt)
    den = lr * lr + li * li
    nr = ar - 1.0
    kr = (nr * lr + ai * li) / den
    ki = (ai * lr - nr * li) / den
    br = b_re.astype(F32)
    bi = b_im.astype(F32)
    bbr = kr[..., None] * br - ki[..., None] * bi
    bbi = kr[..., None] * bi + ki[..., None] * br
    gpk = MXU_K // hg
    n_kt = g // gpk
    eye = jnp.eye(gpk, dtype=F32)

    def block_diag(t):
        full = t[:, :, :, None, :] * eye[None, :, None, :, None]
        return full.reshape(n_kt, gpk * t.shape[2], gpk * t.shape[3])

    def in_blocks(bb):
        return block_diag(bb.reshape(n_kt, gpk, p, hg).transpose(0, 1, 3, 2))

    def out_blocks(cc):
        return block_diag(cc.reshape(n_kt, gpk, hg, p).transpose(0, 1, 3, 2))

    wb = jnp.concatenate([in_blocks(bbr), in_blocks(bbi)], axis=-1).astype(BF16)
    wc = jnp.concatenate([out_blocks(c_re.astype(F32)), -out_blocks(c_im.astype(F32))], axis=1).astype(BF16)
    ncol = g * p // LANES
    pk = SUBLANES // nb
    npc = ncol // pk

    def packed_rows(a):
        t = a.reshape(pk, npc, 1, LANES).transpose(1, 0, 2, 3)
        return jnp.broadcast_to(t, (npc, pk, nb, LANES)).reshape(npc, pk * nb, LANES)

    return wb, wc, packed_rows(ar), packed_rows(ai)


def _gla_proj_body(x_ref, mod_ref, w_ref, wa_ref, wa2_ref, ba2_ref,
                   q_ref, k_ref, v_ref, g_ref, laf_ref, lab_ref, *, qk_w, v_w, dk):
    u = (x_ref[...] * (1.0 + mod_ref[1:2, :]) + mod_ref[0:1, :]).astype(BF16)
    p = jnp.dot(u, w_ref[...], preferred_element_type=F32)
    q_ref[...] = p[:, :qk_w] * (dk ** -0.5)
    k_ref[...] = p[:, qk_w:2 * qk_w]
    v_ref[...] = p[:, 2 * qk_w:2 * qk_w + v_w]
    g_ref[...] = p[:, 2 * qk_w + v_w:]
    for dr, out in enumerate((laf_ref, lab_ref)):
        a_d = jnp.dot(u, wa_ref[dr], preferred_element_type=F32).astype(BF16)
        z = jnp.dot(a_d, wa2_ref[dr], preferred_element_type=F32) + ba2_ref[dr]
        out[...] = (jnp.minimum(z, 0.0) - jnp.log1p(jnp.exp(-jnp.abs(z)))) / GLA_GATE_NORM


def _gla_proj(h2d, modtab, w_main, w_a, w_a2, b_a2, *, tiles_per_batch, ctx_tile0):
    t, d = h2d.shape
    tt = TOK_TILE
    qk_w = w_a2.shape[-1]
    v_w = (w_main.shape[1] - 2 * qk_w) // 2
    dk = qk_w // GLA_HEADS
    mod_map = lambda i: (i // tiles_per_batch, ((i % tiles_per_batch) >= ctx_tile0).astype(I32), 0, 0)
    row = lambda w: pl.BlockSpec((tt, w), lambda i: (i, 0))
    full = lambda a: pl.BlockSpec(a.shape, lambda i: (0,) * a.ndim)
    body = functools.partial(_gla_proj_body, qk_w=qk_w, v_w=v_w, dk=dk)
    b_a2r = b_a2.reshape(2, 1, qk_w).astype(F32)
    return pl.pallas_call(
        body,
        grid=(t // tt,),
        in_specs=[row(d), pl.BlockSpec((None, None, 6, d), mod_map),
                  full(w_main), full(w_a), full(w_a2), full(b_a2r)],
        out_specs=[row(qk_w), row(qk_w), row(v_w), row(v_w), row(qk_w), row(qk_w)],
        out_shape=[jax.ShapeDtypeStruct((t, w), F32) for w in (qk_w, qk_w, v_w, v_w, qk_w, qk_w)],
        compiler_params=_cparams(("parallel",)),
        name="gla_proj",
    )(h2d, modtab, w_main, w_a, w_a2, b_a2r)


def _gla_rec_body(qf_ref, kf_ref, vf_ref, laf_ref, qb_ref, kb_ref, vb_ref, lab_ref, of_ref, ob_ref, *s_refs,
                  ch, dk, dv, nh, nb):
    @pl.when(pl.program_id(0) == 0)
    def _():
        for s_ref in s_refs:
            s_ref[...] = jnp.zeros_like(s_ref)

    spc = qf_ref.shape[1] // ch
    for ci in range(spc):
        _gla_chunk(qf_ref, kf_ref, vf_ref, laf_ref, of_ref, s_refs[:nh], slice(ci * ch, (ci + 1) * ch),
                   rev=False, ch=ch, dk=dk, dv=dv, nh=nh, nb=nb)
        cb = spc - 1 - ci
        _gla_chunk(qb_ref, kb_ref, vb_ref, lab_ref, ob_ref, s_refs[nh:], slice(cb * ch, (cb + 1) * ch),
                   rev=True, ch=ch, dk=dk, dv=dv, nh=nh, nb=nb)


def _gla_chunk(q_ref, k_ref, v_ref, la_ref, o_ref, s_refs, rows, *, rev, ch, dk, dv, nh, nb):
    n = nb * ch
    stack = lambda ref: jnp.concatenate([ref[bi, rows, :] for bi in range(nb)], axis=0)
    r_i = lax.broadcasted_iota(I32, (n, n), 0)
    c_i = lax.broadcasted_iota(I32, (n, n), 1)
    seen = ((r_i // ch) == (c_i // ch)) & ((c_i >= r_i) if rev else (c_i <= r_i))
    end = 0 if rev else ch - 1
    k_all = stack(k_ref)
    v_all = stack(v_ref).astype(BF16)
    b = jnp.dot(seen.astype(F32), stack(la_ref), precision=HIGHEST, preferred_element_type=F32)
    b_ends = [b[bi * ch + end:bi * ch + end + 1, :] for bi in range(nb)]
    b_end = jnp.concatenate([jnp.broadcast_to(be, (ch, nh * dk)) for be in b_ends], axis=0)
    q_d_all = (stack(q_ref) * jnp.exp(b)).astype(BF16)
    k_d_all = (k_all * jnp.exp(-b)).astype(BF16)
    k_e_all = (k_all * jnp.exp(b_end - b)).astype(BF16)
    own = (lax.broadcasted_iota(I32, (n, nb * dk), 0) // ch) == (lax.broadcasted_iota(I32, (n, nb * dk), 1) // dk)
    zero = jnp.zeros((n, nb * dk), BF16)
    for hd in range(nh):
        qs = slice(hd * dk, (hd + 1) * dk)
        vs = slice(hd * dv, (hd + 1) * dv)
        q_d, k_d, k_e = q_d_all[:, qs], k_d_all[:, qs], k_e_all[:, qs]
        v = v_all[:, vs]
        att = lax.dot_general(q_d, k_d, (((1,), (1,)), ((), ())), preferred_element_type=F32)
        att = jnp.where(seen, att, 0.0).astype(BF16)
        q_bd = jnp.where(own, jnp.concatenate([q_d] * nb, axis=1), zero)
        k_bd = jnp.where(own, jnp.concatenate([k_e] * nb, axis=1), zero)
        s_t = s_refs[hd][...]
        o = (jnp.dot(att, v, preferred_element_type=F32)
             + lax.dot_general(q_bd, s_t.astype(BF16), (((1,), (1,)), ((), ())), preferred_element_type=F32))
        for bi in range(nb):
            o_ref[bi, rows, vs] = o[bi * ch:(bi + 1) * ch]
        g_row = jnp.concatenate([jnp.exp(be[:, qs]) for be in b_ends], axis=1)
        upd = lax.dot_general(v, k_bd, (((0,), (0,)), ((), ())), preferred_element_type=F32)
        s_refs[hd][...] = s_t * g_row + upd


def _gla_rec(q, k, v, la_f, la_b, *, n_lat):
    nb, ltot, qk_w = q.shape
    v_w = v.shape[-1]
    ch = GLA_CHUNK
    blk = GLA_STEP_CHUNKS * ch
    assert n_lat % blk == 0 and (ltot - n_lat) % blk == 0
    n_chunks = ltot // blk
    lat_chunks = n_lat // blk
    fwd = lambda j: (j + lat_chunks) % n_chunks
    bwd = lambda j: n_chunks - 1 - j
    spec = lambda w, chunk: pl.BlockSpec((nb, blk, w), lambda j: (0, chunk(j), 0))
    dk, dv = qk_w // GLA_HEADS, v_w // GLA_HEADS
    body = functools.partial(_gla_rec_body, ch=ch, dk=dk, dv=dv, nh=GLA_HEADS, nb=nb)
    out = jax.ShapeDtypeStruct((nb, ltot, v_w), F32)
    return pl.pallas_call(
        body,
        grid=(n_chunks,),
        in_specs=[spec(qk_w, fwd), spec(qk_w, fwd), spec(v_w, fwd), spec(qk_w, fwd),
                  spec(qk_w, bwd), spec(qk_w, bwd), spec(v_w, bwd), spec(qk_w, bwd)],
        out_specs=[spec(v_w, fwd), spec(v_w, bwd)],
        out_shape=[out, out],
        scratch_shapes=[pltpu.VMEM((dv, nb * dk), F32) for _ in range(2 * GLA_HEADS)],
        compiler_params=_cparams(("arbitrary",)),
        name="gla_rec",
    )(q, k, v, la_f, q, k, v, la_b)


def _layer_norm_rows(v, g, b):
    mu = jnp.mean(v, axis=-1, keepdims=True)
    c = v - mu
    var = jnp.mean(c * c, axis=-1, keepdims=True)
    return c * lax.rsqrt(var + LN_EPS) * g + b


def _post_mixer_body(*refs, kind, alpha, tt, dv):
    if kind == "s5":
        (h_ref, mod_ref, ya_ref, yb_ref, dsk_ref, w_ref, lng_ref, lnb_ref, wr_ref, br_ref,
         h1_ref, xr_ref, meta_ref, gt_ref, cnt_ref, base_ref) = refs
    else:
        (h_ref, mod_ref, ya_ref, yb_ref, gate_ref, ng_ref, w_ref, lng_ref, lnb_ref, wr_ref, br_ref,
         h1_ref, xr_ref, meta_ref, gt_ref, cnt_ref, base_ref) = refs
    i = pl.program_id(0)

    @pl.when(i == 0)
    def _():
        base_ref[...] = jnp.zeros_like(base_ref)

    h = h_ref[...]
    d = h.shape[-1]
    if kind == "s5":
        u = h * (1.0 + mod_ref[1:2, :]) + mod_ref[0:1, :]
        y = dsk_ref[...] * u + ya_ref[...] + yb_ref[...]
        ge = 0.5 * y * (1.0 + jnp.tanh(math.sqrt(2.0 / math.pi) * (y + 0.044715 * (y * y * y))))
        z = jnp.dot(ge.astype(BF16), w_ref[...], preferred_element_type=F32)
        mix = z[:, :d] * _sigmoid(z[:, d:])
    else:
        o = ya_ref[...] + yb_ref[...]
        parts = []
        for hd in range(d // dv):
            oh = o[:, hd * dv:(hd + 1) * dv]
            ms = jnp.mean(oh * oh, axis=-1, keepdims=True)
            parts.append(oh * lax.rsqrt(ms + LN_EPS))
        on = jnp.concatenate(parts, axis=-1) * ng_ref[...]
        gv = gate_ref[...]
        a = on * (gv * _sigmoid(gv))
        mix = jnp.dot(a.astype(BF16), w_ref[...], preferred_element_type=F32)

    h1 = _layer_norm_rows(alpha * h + mod_ref[2:3, :] * mix, lng_ref[...], lnb_ref[...])
    h1_ref[...] = h1
    u2 = h1 * (1.0 + mod_ref[4:5, :]) + mod_ref[3:4, :]
    ncp = d // (2 * LANES)
    lo = lax.bitcast_convert_type(u2[:, :d // 2].astype(BF16).astype(F32), jnp.uint32)
    hi = lax.bitcast_convert_type(u2[:, d // 2:].astype(BF16).astype(F32), jnp.uint32)
    words = (lo >> 16) | (hi & jnp.uint32(0xFFFF0000))
    for c in range(ncp):
        xr_ref[pl.ds(c, tt, stride=ncp), :] = words[:, c * LANES:(c + 1) * LANES]

    logits = lax.dot_general(wr_ref[...], u2, (((1,), (1,)), ((), ())), precision=HIGHEST,
                             preferred_element_type=F32) + br_ref[...]
    n_e = logits.shape[0]
    erow = lax.broadcasted_iota(I32, logits.shape, 0)
    work = logits
    vals, idxs = [], []
    for _ in range(TOP_K):
        m = jnp.max(work, axis=0, keepdims=True)
        idx = jnp.min(jnp.where(work == m, erow, n_e), axis=0, keepdims=True)
        vals.append(m)
        idxs.append(idx)
        work = jnp.where(erow == idx, -jnp.inf, work)
    exps = [jnp.exp(v - vals[0]) for v in vals]
    den = exps[0]
    for e in exps[1:]:
        den = den + e

    multi = jnp.zeros(logits.shape, F32)
    for idx in idxs:
        multi = multi + (erow == idx).astype(F32)
    r_i = lax.broadcasted_iota(I32, (tt, tt), 0)
    c_i = lax.broadcasted_iota(I32, (tt, tt), 1)
    earlier = (r_i < c_i).astype(BF16)
    pos = jnp.dot(multi.astype(BF16), earlier, preferred_element_type=F32) + base_ref[:, 0:1]
    row8 = lax.broadcasted_iota(I32, (SUBLANES, tt), 0)
    meta = jnp.zeros((SUBLANES, tt), I32)
    gts = jnp.zeros((SUBLANES, tt), F32)
    for kk in range(TOP_K):
        rank_k = jnp.sum(jnp.where(erow == idxs[kk], pos, 0.0), axis=0, keepdims=True)
        meta = jnp.where(row8 == kk, idxs[kk], meta)
        meta = jnp.where(row8 == TOP_K + kk, rank_k.astype(I32), meta)
        gts = jnp.where(row8 == kk, exps[kk] / den, gts)
    meta_ref[...] = meta
    gt_ref[...] = jnp.transpose(
        jnp.concatenate([gts, jnp.zeros((ROUTE_LANES - SUBLANES, tt), F32)], axis=0))
    new_base = base_ref[:, 0:1] + jnp.sum(multi, axis=1, keepdims=True)
    base_ref[...] = jnp.broadcast_to(new_base, base_ref.shape)
    cnt_ref[...] = jnp.broadcast_to(new_base, cnt_ref.shape)


def _post_mixer(kind, h2d, modtab, ya, yb, extra, w_mix, ln_g, ln_b, w_router, b_router, *,
                alpha, tiles_per_batch, ctx_tile0):
    t, d = h2d.shape
    tt = TOK_TILE
    n_e = w_router.shape[-1]
    wr = w_router.astype(F32).T
    br = b_router.astype(F32).reshape(n_e, 1)
    mod_map = lambda i: (i // tiles_per_batch, ((i % tiles_per_batch) >= ctx_tile0).astype(I32), 0, 0)
    row = lambda w: pl.BlockSpec((tt, w), lambda i: (i, 0))
    full = lambda a: pl.BlockSpec(a.shape, lambda i: (0,) * a.ndim)
    vec = lambda a: a.reshape(1, -1).astype(F32)
    if kind == "s5":
        (d_skip,) = extra
        ins = [h2d, modtab, ya, yb, vec(d_skip), w_mix, vec(ln_g), vec(ln_b), wr, br]
        in_specs = [row(d), pl.BlockSpec((None, None, 6, d), mod_map), row(d), row(d)]
        in_specs += [full(a) for a in ins[4:]]
        dv = d
    else:
        gate, norm_g = extra
        dv = norm_g.shape[-1]
        ng = jnp.tile(norm_g.astype(F32), d // dv).reshape(1, d)
        ins = [h2d, modtab, ya, yb, gate, ng, w_mix, vec(ln_g), vec(ln_b), wr, br]
        in_specs = [row(d), pl.BlockSpec((None, None, 6, d), mod_map), row(d), row(d), row(d)]
        in_specs += [full(a) for a in ins[5:]]
    body = functools.partial(_post_mixer_body, kind=kind, alpha=alpha, tt=tt, dv=dv)
    return pl.pallas_call(
        body,
        grid=(t // tt,),
        in_specs=in_specs,
        out_specs=[
            row(d),
            pl.BlockSpec((tt * (d // (2 * LANES)), LANES), lambda i: (i, 0)),
            pl.BlockSpec((SUBLANES, tt), lambda i: (0, i)),
            row(ROUTE_LANES),
            pl.BlockSpec((n_e, ROUTE_LANES), lambda i: (0, 0)),
        ],
        out_shape=[
            jax.ShapeDtypeStruct((t, d), F32),
            jax.ShapeDtypeStruct((t * (d // (2 * LANES)), LANES), jnp.uint32),
            jax.ShapeDtypeStruct((SUBLANES, t), I32),
            jax.ShapeDtypeStruct((t, ROUTE_LANES), F32),
            jax.ShapeDtypeStruct((n_e, ROUTE_LANES), F32),
        ],
        scratch_shapes=[pltpu.VMEM((n_e, ROUTE_LANES), F32)],
        compiler_params=_cparams(("arbitrary",)),
        name="post_mixer_" + kind,
    )(*ins)


def _row_copy_wait(src_hbm, dst, sem, n_rows):
    pltpu.make_async_copy(src_hbm.at[pl.ds(0, n_rows), :], dst.at[pl.ds(0, n_rows), :], sem).wait()


def _tok_rows(t, nc):
    return pl.ds(pl.multiple_of(t * nc, nc), nc)


def _dispatch_body(dst_ref, zs_ref, x_ref, xp_hbm, zbuf, sem, zsem, *, tt, nc, n_e):
    zr = zbuf.shape[0]

    @pl.when(pl.program_id(0) == 0)
    def _():
        zbuf[...] = jnp.zeros_like(zbuf)
        for e in range(2 * n_e):
            @pl.when(zs_ref[e] >= 0)
            def _():
                pltpu.make_async_copy(zbuf, xp_hbm.at[pl.ds(pl.multiple_of(zs_ref[e] * nc, nc), zr), :], zsem).start()
        for e in range(2 * n_e):
            @pl.when(zs_ref[e] >= 0)
            def _():
                pltpu.make_async_copy(zbuf, xp_hbm.at[pl.ds(pl.multiple_of(zs_ref[e] * nc, nc), zr), :], zsem).wait()

    def issue(t, c):
        for kk in range(TOP_K):
            dst = dst_ref[t * TOP_K + kk]
            pltpu.make_async_copy(x_ref.at[_tok_rows(t, nc), :], xp_hbm.at[_tok_rows(dst, nc), :],
                                  sem).start(priority=kk % 2)
        return c

    lax.fori_loop(0, tt, issue, 0, unroll=8)
    _row_copy_wait(xp_hbm, xp_hbm, sem, tt * TOP_K * nc)


def _dispatch(dest, zstart, xr, n_rows, nc):
    t = xr.shape[0] // nc
    tt = max(k * TOK_TILE for k in (1, 2, 4) if t % (k * TOK_TILE) == 0)
    return pl.pallas_call(
        functools.partial(_dispatch_body, tt=tt, nc=nc, n_e=N_EXPERTS),
        grid=(t // tt,),
        in_specs=[pl.BlockSpec((tt * TOP_K,), lambda i: (i,), memory_space=pltpu.SMEM),
                  pl.BlockSpec(memory_space=pltpu.SMEM),
                  pl.BlockSpec((tt * nc, LANES), lambda i: (i, 0))],
        out_specs=pl.BlockSpec(memory_space=pl.ANY),
        out_shape=jax.ShapeDtypeStruct((n_rows * nc, LANES), xr.dtype),
        scratch_shapes=[pltpu.VMEM((MOE_ROWS * nc, LANES), xr.dtype),
                        pltpu.SemaphoreType.DMA, pltpu.SemaphoreType.DMA],
        compiler_params=_cparams(("arbitrary",)),
        name="moe_dispatch",
    )(dest, zstart, xr)


def _expert_body(be_ref, nu_ref, x_ref, wu_ref, bu_ref, wd_ref, bd_ref, y_ref, wub, wdb):
    i = pl.program_id(0)
    e = be_ref[i]
    prev = be_ref[jnp.maximum(i - 1, 0)]

    @pl.when((i == 0) | (e != prev))
    def _():
        wub[...] = wu_ref[...].astype(BF16)
        wdb[...] = wd_ref[...].astype(BF16)

    nc = wu_ref.shape[0] // LANES
    ncp = nc // 2
    r = y_ref.shape[0] // nc

    @pl.when(i < nu_ref[0])
    def _():
        words = [x_ref[pl.ds(c, r, stride=ncp), :] for c in range(ncp)]
        x = jnp.concatenate(
            [lax.bitcast_convert_type(w << 16, F32) for w in words]
            + [lax.bitcast_convert_type(w & jnp.uint32(0xFFFF0000), F32) for w in words], axis=-1).astype(BF16)
        ff = wdb.shape[0]
        y = bd_ref[...]
        for j in range(ff // FF_SLICE):
            lo, hi = j * FF_SLICE, (j + 1) * FF_SLICE
            h_glu = jnp.dot(x, wub[:, lo:hi], preferred_element_type=F32) + bu_ref[:, lo:hi]
            h_lin = jnp.dot(x, wub[:, ff + lo:ff + hi], preferred_element_type=F32) + bu_ref[:, ff + lo:ff + hi]
            h_glu = jnp.minimum(h_glu, SWIGLU_LIMIT)
            h_lin = jnp.clip(h_lin, -SWIGLU_LIMIT, SWIGLU_LIMIT)
            a = h_glu * _sigmoid(SWIGLU_ALPHA * h_glu) * (h_lin + 1.0)
            y = y + jnp.dot(a.astype(BF16), wdb[lo:hi, :], preferred_element_type=F32)
        for c in range(nc):
            y_ref[pl.ds(c, r, stride=nc), :] = y[:, c * LANES:(c + 1) * LANES]

    @pl.when(i >= nu_ref[0])
    def _():
        y_ref[...] = jnp.zeros_like(y_ref)


def _experts(blk_e, n_used, xp, layer, w_up, b_up, w_down, b_down):
    _, n_e, d, two_ff = w_up.shape
    nc = d // LANES
    ncp = nc // 2
    n_rows = xp.shape[0] // ncp
    r = MOE_ROWS
    ff = two_ff // 2
    grid_spec = pltpu.PrefetchScalarGridSpec(
        num_scalar_prefetch=2,
        grid=(n_rows // r,),
        in_specs=[
            pl.BlockSpec((r * ncp, LANES), lambda i, be, nu: (jnp.minimum(i, nu[0] - 1), 0)),
            pl.BlockSpec((None, None, d, two_ff), lambda i, be, nu: (layer, be[i], 0, 0)),
            pl.BlockSpec((None, None, 1, two_ff), lambda i, be, nu: (layer, be[i], 0, 0)),
            pl.BlockSpec((None, None, ff, d), lambda i, be, nu: (layer, be[i], 0, 0)),
            pl.BlockSpec((None, None, 1, d), lambda i, be, nu: (layer, be[i], 0, 0)),
        ],
        out_specs=pl.BlockSpec((r * nc, LANES), lambda i, be, nu: (i, 0)),
        scratch_shapes=[pltpu.VMEM((d, two_ff), BF16), pltpu.VMEM((ff, d), BF16)],
    )
    return pl.pallas_call(
        _expert_body,
        grid_spec=grid_spec,
        out_shape=jax.ShapeDtypeStruct((n_rows * nc, LANES), F32),
        compiler_params=_cparams(("arbitrary",)),
        name="moe_experts",
    )(blk_e, n_used, xp, w_up, b_up.reshape(-1, n_e, 1, two_ff), w_down, b_down.reshape(-1, n_e, 1, d))


def _combine_body(dst_ref, gt_ref, h_ref, mod_ref, lng_ref, lnb_ref, yp_hbm,
                  o_ref, buf, sems, *, tt, nc, n_tiles, alpha):
    s = pl.program_id(0)
    slot = s % 2

    @pl.when(s < n_tiles)
    def _():
        def issue(t, c):
            for kk in range(TOP_K):
                q = t * TOP_K + kk
                src = dst_ref[q]
                pltpu.make_async_copy(yp_hbm.at[_tok_rows(src, nc), :], buf.at[slot, kk, _tok_rows(t, nc), :],
                                      sems.at[slot]).start(priority=kk % 2)
            return c

        lax.fori_loop(0, tt, issue, 0, unroll=4)

    @pl.when(s > 0)
    def _():
        prev = 1 - slot
        for kk in range(TOP_K):
            _row_copy_wait(yp_hbm, buf.at[prev, kk], sems.at[prev], tt * nc)
        gt = gt_ref[...]
        cols = []
        for c in range(nc):
            acc = gt[:, 0:1] * buf[prev, 0, pl.ds(c, tt, stride=nc), :]
            for kk in range(1, TOP_K):
                acc = acc + gt[:, kk:kk + 1] * buf[prev, kk, pl.ds(c, tt, stride=nc), :]
            cols.append(acc)
        f = jnp.concatenate(cols, axis=-1)
        o_ref[...] = _layer_norm_rows(alpha * h_ref[...] + mod_ref[5:6, :] * f, lng_ref[...], lnb_ref[...])


def _combine(dest, gates, h1, modtab, ln_g, ln_b, yp, *, alpha, tiles_per_batch, ctx_tile0):
    t, d = h1.shape
    tt = TOK_TILE
    nc = d // LANES
    n_tiles = t // tt
    nxt = lambda s: jnp.minimum(s, n_tiles - 1)
    cur = lambda s: jnp.maximum(s - 1, 0)
    smem_blk = pl.BlockSpec((tt * TOP_K,), lambda s: (nxt(s),), memory_space=pltpu.SMEM)
    mod_map = lambda s: (cur(s) // tiles_per_batch, ((cur(s) % tiles_per_batch) >= ctx_tile0).astype(I32), 0, 0)
    vec = lambda a: a.reshape(1, -1).astype(F32)
    return pl.pallas_call(
        functools.partial(_combine_body, tt=tt, nc=nc, n_tiles=n_tiles, alpha=alpha),
        grid=(n_tiles + 1,),
        in_specs=[smem_blk,
                  pl.BlockSpec((tt, ROUTE_LANES), lambda s: (cur(s), 0)),
                  pl.BlockSpec((tt, d), lambda s: (cur(s), 0)),
                  pl.BlockSpec((None, None, 6, d), mod_map),
                  pl.BlockSpec((1, d), lambda s: (0, 0)),
                  pl.BlockSpec((1, d), lambda s: (0, 0)),
                  pl.BlockSpec(memory_space=pl.ANY)],
        out_specs=pl.BlockSpec((tt, d), lambda s: (cur(s), 0)),
        out_shape=jax.ShapeDtypeStruct((t, d), F32),
        scratch_shapes=[pltpu.VMEM((2, TOP_K, tt * nc, LANES), F32), pltpu.SemaphoreType.DMA((2,))],
        compiler_params=_cparams(("arbitrary",)),
        name="moe_combine",
    )(dest, gates, h1, modtab, vec(ln_g), vec(ln_b), yp)


def _moe_layer(h1, xr, meta, gt, counts, modtab, ln_g, ln_b, layer, w_up, b_up, w_down, b_down, *,
               alpha, tiles_per_batch, ctx_tile0):
    t, d = h1.shape
    n_e = w_up.shape[1]
    r = MOE_ROWS
    n_blocks = -(-(t * TOP_K) // r) + n_e
    cnt = counts[:, 0].astype(I32)
    padded = (cnt + r - 1) // r * r
    pends = jnp.cumsum(padded)
    pstart = pends - padded
    tail = pends[-1] + jnp.arange(n_e, dtype=I32) * r
    zstart = jnp.concatenate([jnp.where(cnt > 0, pends - r, -1),
                              jnp.where(tail < n_blocks * r, tail, -1)]).astype(I32)
    blk_row0 = jnp.arange(n_blocks, dtype=I32) * r
    blk_e = jnp.minimum(jnp.sum((pends[None, :] <= blk_row0[:, None]).astype(I32), axis=1), n_e - 1)
    n_used = (pends[-1] // r).astype(I32).reshape(1)
    ti, rk = meta[:TOP_K], meta[TOP_K:2 * TOP_K]
    first = jnp.sum(jnp.where(ti[:, :, None] == jnp.arange(n_e, dtype=I32), pstart, 0), axis=-1)
    dest = (first + rk).T.reshape(-1).astype(I32)
    xp = _dispatch(dest, zstart, xr, n_blocks * r, d // (2 * LANES))
    yp = _experts(blk_e, n_used, xp, layer, w_up, b_up, w_down, b_down)
    return _combine(dest, gt, h1, modtab, ln_g, ln_b, yp,
                    alpha=alpha, tiles_per_batch=tiles_per_batch, ctx_tile0=ctx_tile0)


def _to_cm_body(lat_ref, ctx_ref, o_ref, *, rows, k, n_lat_steps):
    w = pl.program_id(1)
    for q in range(SUBLANES // k):
        @pl.when((w < n_lat_steps) & (w % (SUBLANES // k) == q))
        def _():
            for i in range(k):
                o_ref[i * rows:(i + 1) * rows, :] = lat_ref[:, q * k + i, :]

    @pl.when(w >= n_lat_steps)
    def _():
        o_ref[...] = ctx_ref[...]


def _to_col_major(h3, n_lat):
    bn, ltot, d = h3.shape
    lc = ltot - n_lat
    rows = n_lat // GRID_W
    k = max(kk for kk in (1, 2, 4, 8) if lc % (kk * rows) == 0)
    ob = k * rows
    n_lat_steps = GRID_W // k
    per_blk = SUBLANES // k
    return pl.pallas_call(
        functools.partial(_to_cm_body, rows=rows, k=k, n_lat_steps=n_lat_steps),
        grid=(bn, n_lat_steps + lc // ob),
        in_specs=[pl.BlockSpec((None, rows, SUBLANES, d),
                               lambda b, w: (b, 0, jnp.minimum(w, n_lat_steps - 1) // per_blk, 0)),
                  pl.BlockSpec((None, ob, d), lambda b, w: (b, jnp.maximum(w, n_lat_steps), 0))],
        out_specs=pl.BlockSpec((None, ob, d), lambda b, w: (b, w, 0)),
        out_shape=jax.ShapeDtypeStruct(h3.shape, h3.dtype),
        compiler_params=_cparams(("arbitrary", "arbitrary")),
        name="to_col_major",
    )(h3.reshape(bn, ltot // GRID_W, GRID_W, d), h3)


def _to_rm_body(lat_ref, ctx_ref, o_ref, *, rows, n_ctx_rows):
    w = pl.program_id(1)
    for i in range(SUBLANES):
        o_ref[0:rows, i, :] = lat_ref[i * rows:(i + 1) * rows, :]
    for m in range(n_ctx_rows):
        o_ref[rows + m, :, :] = ctx_ref[pl.ds(pl.multiple_of(m * GRID_W + w * SUBLANES, SUBLANES), SUBLANES), :]


def _to_row_major(h3, n_lat, keep_ctx=True):
    bn, ltot, d = h3.shape
    lc = ltot - n_lat
    rows = n_lat // GRID_W
    assert lc % GRID_W == 0 and n_lat % lc == 0
    n_ctx_rows = lc // GRID_W if keep_ctx else 0
    out = pl.pallas_call(
        functools.partial(_to_rm_body, rows=rows, n_ctx_rows=n_ctx_rows),
        grid=(bn, GRID_W // SUBLANES),
        in_specs=[pl.BlockSpec((None, SUBLANES * rows, d), lambda b, w: (b, w, 0)),
                  pl.BlockSpec((None, lc, d), lambda b, w: (b, n_lat // lc, 0))],
        out_specs=pl.BlockSpec((None, rows + n_ctx_rows, SUBLANES, d), lambda b, w: (b, 0, w, 0)),
        out_shape=jax.ShapeDtypeStruct((bn, rows + n_ctx_rows, GRID_W, d), h3.dtype),
        compiler_params=_cparams(("arbitrary", "arbitrary")),
        name="to_row_major",
    )(h3, h3)
    return out.reshape(bn, (rows + n_ctx_rows) * GRID_W, d)


def kernel(x, c, ctx, c_ctx, w_ada, b_ada, ln1_g, ln1_b, ln2_g, ln2_b, s5_lam_re, s5_lam_im, s5_log_step, s5_b_re, s5_b_im, s5_c_re, s5_c_im, s5_d, s5_w_glu, gla_w_in, gla_w_a2, gla_b_a2, gla_norm_g, gla_w_out, moe_w_router, moe_b_router, moe_w_up, moe_b_up, moe_w_down, moe_b_down):
    bn, l, d = x.shape
    lc = ctx.shape[1]
    depth = w_ada.shape[0]
    ltot = l + lc
    rows = l // GRID_W
    alpha = (2 * depth) ** 0.25
    assert l % TOK_TILE == 0 and lc % TOK_TILE == 0 and l % S5_CHUNK == 0 and lc % S5_CHUNK == 0
    assert bn < SUBLANES
    tiles_per_batch = ltot // TOK_TILE
    ctx_tile0 = l // TOK_TILE
    tile_kw = dict(tiles_per_batch=tiles_per_batch, ctx_tile0=ctx_tile0)

    cc = jnp.zeros((SUBLANES, d), F32).at[:bn].set(c.astype(F32)).at[bn].set(c_ctx.astype(F32))
    mod = _ada_table(cc, w_ada.astype(F32), b_ada.astype(F32))
    mod_lat = mod[:, :bn].reshape(depth, bn, 1, 6, d)
    mod_ctx = jnp.broadcast_to(mod[:, bn].reshape(depth, 1, 1, 6, d), (depth, bn, 1, 6, d))
    modtab = jnp.concatenate([mod_lat, mod_ctx], axis=2)

    to_cm = functools.partial(_to_col_major, n_lat=l)
    to_rm = functools.partial(_to_row_major, n_lat=l)

    h3 = jnp.concatenate([x.astype(F32), ctx.astype(F32)], axis=1)
    for i in range(depth):
        j = i // 2
        mt = modtab[i]
        if i % 2 == 0:
            ys = []
            for dr, rev in ((0, False), (1, True)):
                wb, wc, a_re, a_im = _s5_prepare(s5_lam_re[j, dr], s5_lam_im[j, dr], s5_log_step[j, dr],
                                                 s5_b_re[j, dr], s5_b_im[j, dr], s5_c_re[j, dr], s5_c_im[j, dr], bn)
                ys.append(_s5_scan(h3, mt, wb, wc, a_re, a_im, rev=rev, n_lat=l).reshape(bn * ltot, d))
            h2d = h3.reshape(bn * ltot, d)
            outs = _post_mixer("s5", h2d, mt, ys[0], ys[1], (s5_d[j],), s5_w_glu[j].astype(BF16),
                               ln1_g[i], ln1_b[i], moe_w_router[i], moe_b_router[i], alpha=alpha, **tile_kw)
        else:
            h3 = to_cm(h3)
            h2d = h3.reshape(bn * ltot, d)
            w_in = gla_w_in[j]
            n_main = w_in.shape[1] - 2 * GLA_GATE_RANK
            w_a = w_in[:, n_main:].reshape(d, 2, GLA_GATE_RANK).transpose(1, 0, 2).astype(BF16)
            q, k, v, g, la_f, la_b = _gla_proj(h2d, mt, w_in[:, :n_main].astype(BF16), w_a,
                                               gla_w_a2[j].astype(BF16), gla_b_a2[j], **tile_kw)
            r3 = lambda a: a.reshape(bn, ltot, a.shape[-1])
            o_f, o_b = [o.reshape(bn * ltot, d) for o in _gla_rec(r3(q), r3(k), r3(v), r3(la_f), r3(la_b), n_lat=l)]
            outs = _post_mixer("gla", h2d, mt, o_f, o_b, (g, gla_norm_g[j]), gla_w_out[j].astype(BF16),
                               ln1_g[i], ln1_b[i], moe_w_router[i], moe_b_router[i], alpha=alpha, **tile_kw)
        h1, xr, meta, gt, counts = outs
        h2 = _moe_layer(h1, xr, meta, gt, counts, mt, ln2_g[i], ln2_b[i],
                        i, moe_w_up, moe_b_up, moe_w_down, moe_b_down, alpha=alpha, **tile_kw)
        h3 = h2.reshape(bn, ltot, d)
        if i % 2 == 1:
            h3 = to_rm(h3, keep_ctx=i < depth - 1)
    return h3[:, :l].astype(x.dtype)
```

```python
import functools
import math

import jax
import jax.numpy as jnp
from jax import lax
from jax.experimental import pallas as pl
from jax.experimental.pallas import tpu as pltpu

F32 = jnp.float32
BF16 = jnp.bfloat16
I32 = jnp.int32
HIGHEST = lax.Precision.HIGHEST

GRID_W = 64
S5_GROUP = 16
S5_STATE = 64
GLA_HEADS = 4
GLA_GATE_RANK = 16
GLA_GATE_NORM = 16.0
GLA_CHUNK = 64
GLA_STEP_CHUNKS = 2
N_EXPERTS = 32
TOP_K = 4
SWIGLU_ALPHA = 1.702
SWIGLU_LIMIT = 7.0
LN_EPS = 1e-5

LANES = 128
SUBLANES = 8
MXU_K = 256
VMEM_LIMIT = 56 * 1024 * 1024

TOK_TILE = 256
S5_CHUNK = 128
S5_PITCH = S5_CHUNK + SUBLANES // 2
MOE_ROWS = 512
FF_SLICE = 1024
ROUTE_LANES = LANES


def _sigmoid(x):
    return 1.0 / (1.0 + jnp.exp(-x))


def _cparams(sem):
    return pltpu.CompilerParams(dimension_semantics=sem, vmem_limit_bytes=VMEM_LIMIT)


def _ada_body(c_ref, w_ref, b_ref, o_ref):
    c = c_ref[...]
    cond = c * _sigmoid(c)
    o_ref[...] = jnp.dot(cond, w_ref[...], precision=HIGHEST, preferred_element_type=F32) + b_ref[...]


def _ada_table(cc, w_ada, b_ada):
    depth, d, six_d = w_ada.shape
    n_tiles = six_d // d
    return pl.pallas_call(
        _ada_body,
        grid=(depth, n_tiles),
        in_specs=[
            pl.BlockSpec((SUBLANES, d), lambda i, n: (0, 0)),
            pl.BlockSpec((None, d, d), lambda i, n: (i, 0, n)),
            pl.BlockSpec((None, 1, d), lambda i, n: (i, 0, n)),
        ],
        out_specs=pl.BlockSpec((None, SUBLANES, d), lambda i, n: (i, 0, n)),
        out_shape=jax.ShapeDtypeStruct((depth, SUBLANES, six_d), F32),
        compiler_params=_cparams(("arbitrary", "arbitrary")),
        name="ada_table",
    )(cc, w_ada, b_ada.reshape(depth, 1, six_d))


def _s5_scan_body(x_ref, mod_ref, wb_ref, wc_ref, are_ref, aim_ref, y_ref,
                  sre_ref, sim_ref, st_re, st_im, u_ref, *, rev, tc, pitch, nb, n_kt, cpk, pk):
    j = pl.program_id(0)
    ncol = n_kt * cpk
    npc = ncol // pk
    half = cpk * LANES
    gr = nb * pitch

    @pl.when(j == 0)
    def _():
        st_re[...] = jnp.zeros_like(st_re)
        st_im[...] = jnp.zeros_like(st_im)
        u_ref[...] = jnp.zeros_like(u_ref)

    for b in range(nb):
        u_ref[b * pitch:b * pitch + tc, :] = x_ref[b] * (1.0 + mod_ref[b, 1:2, :]) + mod_ref[b, 0:1, :]
    u = u_ref[...].astype(BF16)
    for kt in range(n_kt):
        r = jnp.dot(u[:, MXU_K * kt:MXU_K * (kt + 1)], wb_ref[kt], preferred_element_type=F32)
        for c in range(cpk):
            col = kt * cpk + c
            grows = slice((col // npc) * gr, (col // npc + 1) * gr)
            sre_ref[col % npc, grows, :] = r[:, LANES * c:LANES * (c + 1)]
            sim_ref[col % npc, grows, :] = r[:, half + LANES * c:half + LANES * (c + 1)]

    grp = 8
    for cg in range(npc // grp):
        cols = list(range(cg * grp, (cg + 1) * grp))
        ar = [are_ref[c] for c in cols]
        ai = [aim_ref[c] for c in cols]
        init = tuple(st_re[c] for c in cols) + tuple(st_im[c] for c in cols)

        def step(t, carry, cols=cols, ar=ar, ai=ai):
            tt = (tc - 1 - t) if rev else t
            out_re, out_im = [], []
            for k, c in enumerate(cols):
                rows = pl.ds(tt, pk * nb, stride=pitch)
                pr, pi = carry[k], carry[grp + k]
                nr = ar[k] * pr - ai[k] * pi + sre_ref[c, rows, :]
                ni = ar[k] * pi + ai[k] * pr + sim_ref[c, rows, :]
                sre_ref[c, rows, :] = nr
                sim_ref[c, rows, :] = ni
                out_re.append(nr)
                out_im.append(ni)
            return tuple(out_re) + tuple(out_im)

        fin = lax.fori_loop(0, tc, step, init)
        for k, c in enumerate(cols):
            st_re[c] = fin[k]
            st_im[c] = fin[grp + k]

    def stacked(ref, kt):
        def piece(c):
            col = kt * cpk + c
            return ref[col % npc, (col // npc) * gr:(col // npc + 1) * gr, :]

        return jnp.concatenate([piece(c) for c in range(cpk)], axis=-1).astype(BF16)

    for kt in range(n_kt):
        y = (jnp.dot(stacked(sre_ref, kt), wc_ref[kt, :half, :], preferred_element_type=F32)
             + jnp.dot(stacked(sim_ref, kt), wc_ref[kt, half:, :], preferred_element_type=F32))
        for b in range(nb):
            y_ref[b, :, MXU_K * kt:MXU_K * (kt + 1)] = y[b * pitch:b * pitch + tc]


def _s5_scan(h3, modtab, wb, wc, a_re, a_im, *, rev, n_lat):
    nb, ltot, d = h3.shape
    tc = S5_CHUNK
    n_chunks = ltot // tc
    lat_chunks = n_lat // tc
    n_kt, _, two_half = wb.shape
    cpk = two_half // (2 * LANES)
    ncol = n_kt * cpk
    pk = SUBLANES // nb
    npc = ncol // pk

    if rev:
        chunk = lambda j: n_chunks - 1 - j
    else:
        chunk = lambda j: (j + lat_chunks) % n_chunks
    is_ctx = lambda j: (chunk(j) >= lat_chunks).astype(I32)

    body = functools.partial(_s5_scan_body, rev=rev, tc=tc, pitch=S5_PITCH, nb=nb, n_kt=n_kt, cpk=cpk, pk=pk)
    return pl.pallas_call(
        body,
        grid=(n_chunks,),
        in_specs=[
            pl.BlockSpec((nb, tc, d), lambda j: (0, chunk(j), 0)),
            pl.BlockSpec((nb, None, 6, d), lambda j: (0, is_ctx(j), 0, 0)),
            pl.BlockSpec(wb.shape, lambda j: (0, 0, 0)),
            pl.BlockSpec(wc.shape, lambda j: (0, 0, 0)),
            pl.BlockSpec(a_re.shape, lambda j: (0, 0, 0)),
            pl.BlockSpec(a_im.shape, lambda j: (0, 0, 0)),
        ],
        out_specs=pl.BlockSpec((nb, tc, d), lambda j: (0, chunk(j), 0)),
        out_shape=jax.ShapeDtypeStruct((nb, ltot, d), F32),
        scratch_shapes=[
            pltpu.VMEM((npc, pk * nb * S5_PITCH, LANES), F32),
            pltpu.VMEM((npc, pk * nb * S5_PITCH, LANES), F32),
            pltpu.VMEM((npc, pk * nb, LANES), F32),
            pltpu.VMEM((npc, pk * nb, LANES), F32),
            pltpu.VMEM((nb * S5_PITCH, d), F32),
        ],
        compiler_params=_cparams(("arbitrary",)),
        name="s5_scan_bwd" if rev else "s5_scan_fwd",
    )(h3, modtab, wb, wc, a_re, a_im)


def _s5_prepare(lam_re, lam_im, log_step, b_re, b_im, c_re, c_im, nb):
    g, p = lam_re.shape
    hg = b_re.shape[-1]
    lr = lam_re.astype(F32)
    li = lam_im.astype(F32)
    dt = jnp.exp(log_step.astype(F32))[:, None]
    mag = jnp.exp(lr * dt)
    ar = mag * jnp.cos(li * dt)
    ai = mag * jnp.sin(li * dt)
    den = lr * lr + li * li
    nr = ar - 1.0
    kr = (nr * lr + ai * li) / den
    ki = (ai * lr - nr * li) / den
    br = b_re.astype(F32)
    bi = b_im.astype(F32)
    bbr = kr[..., None] * br - ki[..., None] * bi
    bbi = kr[..., None] * bi + ki[..., None] * br
    gpk = MXU_K // hg
    n_kt = g // gpk
    eye = jnp.eye(gpk, dtype=F32)

    def block_diag(t):
        full = t[:, :, :, None, :] * eye[None, :, None, :, None]
        return full.reshape(n_kt, gpk * t.shape[2], gpk * t.shape[3])

    def in_blocks(bb):
        return block_diag(bb.reshape(n_kt, gpk, p, hg).transpose(0, 1, 3, 2))

    def out_blocks(cc):
        return block_diag(cc.reshape(n_kt, gpk, hg, p).transpose(0, 1, 3, 2))

    wb = jnp.concatenate([in_blocks(bbr), in_blocks(bbi)], axis=-1).astype(BF16)
    wc = jnp.concatenate([out_blocks(c_re.astype(F32)), -out_blocks(c_im.astype(F32))], axis=1).astype(BF16)
    ncol = g * p // LANES
    pk = SUBLANES // nb
    npc = ncol // pk

    def packed_rows(a):
        t = a.reshape(pk, npc, 1, LANES).transpose(1, 0, 2, 3)
        return jnp.broadcast_to(t, (npc, pk, nb, LANES)).reshape(npc, pk * nb, LANES)

    return wb, wc, packed_rows(ar), packed_rows(ai)


def _gla_proj_body(x_ref, mod_ref, w_ref, wa_ref, wa2_ref, ba2_ref,
                   q_ref, k_ref, v_ref, g_ref, laf_ref, lab_ref, *, qk_w, v_w, dk):
    u = (x_ref[...] * (1.0 + mod_ref[1:2, :]) + mod_ref[0:1, :]).astype(BF16)
    p = jnp.dot(u, w_ref[...], preferred_element_type=F32)
    q_ref[...] = p[:, :qk_w] * (dk ** -0.5)
    k_ref[...] = p[:, qk_w:2 * qk_w]
    v_ref[...] = p[:, 2 * qk_w:2 * qk_w + v_w]
    g_ref[...] = p[:, 2 * qk_w + v_w:]
    for dr, out in enumerate((laf_ref, lab_ref)):
        a_d = jnp.dot(u, wa_ref[dr], preferred_element_type=F32).astype(BF16)
        z = jnp.dot(a_d, wa2_ref[dr], preferred_element_type=F32) + ba2_ref[dr]
        out[...] = (jnp.minimum(z, 0.0) - jnp.log1p(jnp.exp(-jnp.abs(z)))) / GLA_GATE_NORM


def _gla_proj(h2d, modtab, w_main, w_a, w_a2, b_a2, *, tiles_per_batch, ctx_tile0):
    t, d = h2d.shape
    tt = TOK_TILE
    qk_w = w_a2.shape[-1]
    v_w = (w_main.shape[1] - 2 * qk_w) // 2
    dk = qk_w // GLA_HEADS
    mod_map = lambda i: (i // tiles_per_batch, ((i % tiles_per_batch) >= ctx_tile0).astype(I32), 0, 0)
    row = lambda w: pl.BlockSpec((tt, w), lambda i: (i, 0))
    full = lambda a: pl.BlockSpec(a.shape, lambda i: (0,) * a.ndim)
    body = functools.partial(_gla_proj_body, qk_w=qk_w, v_w=v_w, dk=dk)
    b_a2r = b_a2.reshape(2, 1, qk_w).astype(F32)
    return pl.pallas_call(
        body,
        grid=(t // tt,),
        in_specs=[row(d), pl.BlockSpec((None, None, 6, d), mod_map),
                  full(w_main), full(w_a), full(w_a2), full(b_a2r)],
        out_specs=[row(qk_w), row(qk_w), row(v_w), row(v_w), row(qk_w), row(qk_w)],
        out_shape=[jax.ShapeDtypeStruct((t, w), F32) for w in (qk_w, qk_w, v_w, v_w, qk_w, qk_w)],
        compiler_params=_cparams(("parallel",)),
        name="gla_proj",
    )(h2d, modtab, w_main, w_a, w_a2, b_a2r)


def _gla_rec_body(qf_ref, kf_ref, vf_ref, laf_ref, qb_ref, kb_ref, vb_ref, lab_ref, of_ref, ob_ref, *s_refs,
                  ch, dk, dv, nh, nb):
    @pl.when(pl.program_id(0) == 0)
    def _():
        for s_ref in s_refs:
            s_ref[...] = jnp.zeros_like(s_ref)

    spc = qf_ref.shape[1] // ch
    for ci in range(spc):
        _gla_chunk(qf_ref, kf_ref, vf_ref, laf_ref, of_ref, s_refs[:nh], slice(ci * ch, (ci + 1) * ch),
                   rev=False, ch=ch, dk=dk, dv=dv, nh=nh, nb=nb)
        cb = spc - 1 - ci
        _gla_chunk(qb_ref, kb_ref, vb_ref, lab_ref, ob_ref, s_refs[nh:], slice(cb * ch, (cb + 1) * ch),
                   rev=True, ch=ch, dk=dk, dv=dv, nh=nh, nb=nb)


def _gla_chunk(q_ref, k_ref, v_ref, la_ref, o_ref, s_refs, rows, *, rev, ch, dk, dv, nh, nb):
    n = nb * ch
    stack = lambda ref: jnp.concatenate([ref[bi, rows, :] for bi in range(nb)], axis=0)
    r_i = lax.broadcasted_iota(I32, (n, n), 0)
    c_i = lax.broadcasted_iota(I32, (n, n), 1)
    seen = ((r_i // ch) == (c_i // ch)) & ((c_i >= r_i) if rev else (c_i <= r_i))
    end = 0 if rev else ch - 1
    k_all = stack(k_ref)
    v_all = stack(v_ref).astype(BF16)
    b = jnp.dot(seen.astype(F32), stack(la_ref), precision=HIGHEST, preferred_element_type=F32)
    b_ends = [b[bi * ch + end:bi * ch + end + 1, :] for bi in range(nb)]
    b_end = jnp.concatenate([jnp.broadcast_to(be, (ch, nh * dk)) for be in b_ends], axis=0)
    q_d_all = (stack(q_ref) * jnp.exp(b)).astype(BF16)
    k_d_all = (k_all * jnp.exp(-b)).astype(BF16)
    k_e_all = (k_all * jnp.exp(b_end - b)).astype(BF16)
    own = (lax.broadcasted_iota(I32, (n, nb * dk), 0) // ch) == (lax.broadcasted_iota(I32, (n, nb * dk), 1) // dk)
    zero = jnp.zeros((n, nb * dk), BF16)
    for hd in range(nh):
        qs = slice(hd * dk, (hd + 1) * dk)
        vs = slice(hd * dv, (hd + 1) * dv)
        q_d, k_d, k_e = q_d_all[:, qs], k_d_all[:, qs], k_e_all[:, qs]
        v = v_all[:, vs]
        att = lax.dot_general(q_d, k_d, (((1,), (1,)), ((), ())), preferred_element_type=F32)
        att = jnp.where(seen, att, 0.0).astype(BF16)
        q_bd = jnp.where(own, jnp.concatenate([q_d] * nb, axis=1), zero)
        k_bd = jnp.where(own, jnp.concatenate([k_e] * nb, axis=1), zero)
        s_t = s_refs[hd][...]
        o = (jnp.dot(att, v, preferred_element_type=F32)
             + lax.dot_general(q_bd, s_t.astype(BF16), (((1,), (1,)), ((), ())), preferred_element_type=F32))
        for bi in range(nb):
            o_ref[bi, rows, vs] = o[bi * ch:(bi + 1) * ch]
        g_row = jnp.concatenate([jnp.exp(be[:, qs]) for be in b_ends], axis=1)
        upd = lax.dot_general(v, k_bd, (((0,), (0,)), ((), ())), preferred_element_type=F32)
        s_refs[hd][...] = s_t * g_row + upd


def _gla_rec(q, k, v, la_f, la_b, *, n_lat):
    nb, ltot, qk_w = q.shape
    v_w = v.shape[-1]
    ch = GLA_CHUNK
    blk = GLA_STEP_CHUNKS * ch
    assert n_lat % blk == 0 and (ltot - n_lat) % blk == 0
    n_chunks = ltot // blk
    lat_chunks = n_lat // blk
    fwd = lambda j: (j + lat_chunks) % n_chunks
    bwd = lambda j: n_chunks - 1 - j
    spec = lambda w, chunk: pl.BlockSpec((nb, blk, w), lambda j: (0, chunk(j), 0))
    dk, dv = qk_w // GLA_HEADS, v_w // GLA_HEADS
    body = functools.partial(_gla_rec_body, ch=ch, dk=dk, dv=dv, nh=GLA_HEADS, nb=nb)
    out = jax.ShapeDtypeStruct((nb, ltot, v_w), F32)
    return pl.pallas_call(
        body,
        grid=(n_chunks,),
        in_specs=[spec(qk_w, fwd), spec(qk_w, fwd), spec(v_w, fwd), spec(qk_w, fwd),
                  spec(qk_w, bwd), spec(qk_w, bwd), spec(v_w, bwd), spec(qk_w, bwd)],
        out_specs=[spec(v_w, fwd), spec(v_w, bwd)],
        out_shape=[out, out],
        scratch_shapes=[pltpu.VMEM((dv, nb * dk), F32) for _ in range(2 * GLA_HEADS)],
        compiler_params=_cparams(("arbitrary",)),
        name="gla_rec",
    )(q, k, v, la_f, q, k, v, la_b)


def _layer_norm_rows(v, g, b):
    mu = jnp.mean(v, axis=-1, keepdims=True)
    c = v - mu
    var = jnp.mean(c * c, axis=-1, keepdims=True)
    return c * lax.rsqrt(var + LN_EPS) * g + b


def _post_mixer_body(*refs, kind, alpha, tt, dv):
    if kind == "s5":
        (h_ref, mod_ref, ya_ref, yb_ref, dsk_ref, w_ref, lng_ref, lnb_ref, wr_ref, br_ref,
         h1_ref, xr_ref, meta_ref, gt_ref, cnt_ref, base_ref) = refs
    else:
        (h_ref, mod_ref, ya_ref, yb_ref, gate_ref, ng_ref, w_ref, lng_ref, lnb_ref, wr_ref, br_ref,
         h1_ref, xr_ref, meta_ref, gt_ref, cnt_ref, base_ref) = refs
    i = pl.program_id(0)

    @pl.when(i == 0)
    def _():
        base_ref[...] = jnp.zeros_like(base_ref)

    h = h_ref[...]
    d = h.shape[-1]
    if kind == "s5":
        u = h * (1.0 + mod_ref[1:2, :]) + mod_ref[0:1, :]
        y = dsk_ref[...] * u + ya_ref[...] + yb_ref[...]
        ge = 0.5 * y * (1.0 + jnp.tanh(math.sqrt(2.0 / math.pi) * (y + 0.044715 * (y * y * y))))
        z = jnp.dot(ge.astype(BF16), w_ref[...], preferred_element_type=F32)
        mix = z[:, :d] * _sigmoid(z[:, d:])
    else:
        o = ya_ref[...] + yb_ref[...]
        parts = []
        for hd in range(d // dv):
            oh = o[:, hd * dv:(hd + 1) * dv]
            ms = jnp.mean(oh * oh, axis=-1, keepdims=True)
            parts.append(oh * lax.rsqrt(ms + LN_EPS))
        on = jnp.concatenate(parts, axis=-1) * ng_ref[...]
        gv = gate_ref[...]
        a = on * (gv * _sigmoid(gv))
        mix = jnp.dot(a.astype(BF16), w_ref[...], preferred_element_type=F32)

    h1 = _layer_norm_rows(alpha * h + mod_ref[2:3, :] * mix, lng_ref[...], lnb_ref[...])
    h1_ref[...] = h1
    u2 = h1 * (1.0 + mod_ref[4:5, :]) + mod_ref[3:4, :]
    ncp = d // (2 * LANES)
    lo = lax.bitcast_convert_type(u2[:, :d // 2].astype(BF16).astype(F32), jnp.uint32)
    hi = lax.bitcast_convert_type(u2[:, d // 2:].astype(BF16).astype(F32), jnp.uint32)
    words = (lo >> 16) | (hi & jnp.uint32(0xFFFF0000))
    for c in range(ncp):
        xr_ref[pl.ds(c, tt, stride=ncp), :] = words[:, c * LANES:(c + 1) * LANES]

    logits = lax.dot_general(wr_ref[...], u2, (((1,), (1,)), ((), ())), precision=HIGHEST,
                             preferred_element_type=F32) + br_ref[...]
    n_e = logits.shape[0]
    erow = lax.broadcasted_iota(I32, logits.shape, 0)
    work = logits
    vals, idxs = [], []
    for _ in range(TOP_K):
        m = jnp.max(work, axis=0, keepdims=True)
        idx = jnp.min(jnp.where(work == m, erow, n_e), axis=0, keepdims=True)
        vals.append(m)
        idxs.append(idx)
        work = jnp.where(erow == idx, -jnp.inf, work)
    exps = [jnp.exp(v - vals[0]) for v in vals]
    den = exps[0]
    for e in exps[1:]:
        den = den + e

    multi = jnp.zeros(logits.shape, F32)
    for idx in idxs:
        multi = multi + (erow == idx).astype(F32)
    r_i = lax.broadcasted_iota(I32, (tt, tt), 0)
    c_i = lax.broadcasted_iota(I32, (tt, tt), 1)
    earlier = (r_i < c_i).astype(BF16)
    pos = jnp.dot(multi.astype(BF16), earlier, preferred_element_type=F32) + base_ref[:, 0:1]
    row8 = lax.broadcasted_iota(I32, (SUBLANES, tt), 0)
    meta = jnp.zeros((SUBLANES, tt), I32)
    gts = jnp.zeros((SUBLANES, tt), F32)
    for kk in range(TOP_K):
        rank_k = jnp.sum(jnp.where(erow == idxs[kk], pos, 0.0), axis=0, keepdims=True)
        meta = jnp.where(row8 == kk, idxs[kk], meta)
        meta = jnp.where(row8 == TOP_K + kk, rank_k.astype(I32), meta)
        gts = jnp.where(row8 == kk, exps[kk] / den, gts)
    meta_ref[...] = meta
    gt_ref[...] = jnp.transpose(
        jnp.concatenate([gts, jnp.zeros((ROUTE_LANES - SUBLANES, tt), F32)], axis=0))
    new_base = base_ref[:, 0:1] + jnp.sum(multi, axis=1, keepdims=True)
    base_ref[...] = jnp.broadcast_to(new_base, base_ref.shape)
    cnt_ref[...] = jnp.broadcast_to(new_base, cnt_ref.shape)


def _post_mixer(kind, h2d, modtab, ya, yb, extra, w_mix, ln_g, ln_b, w_router, b_router, *,
                alpha, tiles_per_batch, ctx_tile0):
    t, d = h2d.shape
    tt = TOK_TILE
    n_e = w_router.shape[-1]
    wr = w_router.astype(F32).T
    br = b_router.astype(F32).reshape(n_e, 1)
    mod_map = lambda i: (i // tiles_per_batch, ((i % tiles_per_batch) >= ctx_tile0).astype(I32), 0, 0)
    row = lambda w: pl.BlockSpec((tt, w), lambda i: (i, 0))
    full = lambda a: pl.BlockSpec(a.shape, lambda i: (0,) * a.ndim)
    vec = lambda a: a.reshape(1, -1).astype(F32)
    if kind == "s5":
        (d_skip,) = extra
        ins = [h2d, modtab, ya, yb, vec(d_skip), w_mix, vec(ln_g), vec(ln_b), wr, br]
        in_specs = [row(d), pl.BlockSpec((None, None, 6, d), mod_map), row(d), row(d)]
        in_specs += [full(a) for a in ins[4:]]
        dv = d
    else:
        gate, norm_g = extra
        dv = norm_g.shape[-1]
        ng = jnp.tile(norm_g.astype(F32), d // dv).reshape(1, d)
        ins = [h2d, modtab, ya, yb, gate, ng, w_mix, vec(ln_g), vec(ln_b), wr, br]
        in_specs = [row(d), pl.BlockSpec((None, None, 6, d), mod_map), row(d), row(d), row(d)]
        in_specs += [full(a) for a in ins[5:]]
    body = functools.partial(_post_mixer_body, kind=kind, alpha=alpha, tt=tt, dv=dv)
    return pl.pallas_call(
        body,
        grid=(t // tt,),
        in_specs=in_specs,
        out_specs=[
            row(d),
            pl.BlockSpec((tt * (d // (2 * LANES)), LANES), lambda i: (i, 0)),
            pl.BlockSpec((SUBLANES, tt), lambda i: (0, i)),
            row(ROUTE_LANES),
            pl.BlockSpec((n_e, ROUTE_LANES), lambda i: (0, 0)),
        ],
        out_shape=[
            jax.ShapeDtypeStruct((t, d), F32),
            jax.ShapeDtypeStruct((t * (d // (2 * LANES)), LANES), jnp.uint32),
            jax.ShapeDtypeStruct((SUBLANES, t), I32),
            jax.ShapeDtypeStruct((t, ROUTE_LANES), F32),
            jax.ShapeDtypeStruct((n_e, ROUTE_LANES), F32),
        ],
        scratch_shapes=[pltpu.VMEM((n_e, ROUTE_LANES), F32)],
        compiler_params=_cparams(("arbitrary",)),
        name="post_mixer_" + kind,
    )(*ins)


def _row_copy_wait(src_hbm, dst, sem, n_rows):
    pltpu.make_async_copy(src_hbm.at[pl.ds(0, n_rows), :], dst.at[pl.ds(0, n_rows), :], sem).wait()


def _tok_rows(t, nc):
    return pl.ds(pl.multiple_of(t * nc, nc), nc)


def _dispatch_body(dst_ref, zs_ref, x_ref, xp_hbm, zbuf, sem, zsem, *, tt, nc, n_e):
    zr = zbuf.shape[0]

    @pl.when(pl.program_id(0) == 0)
    def _():
        zbuf[...] = jnp.zeros_like(zbuf)
        for e in range(2 * n_e):
            @pl.when(zs_ref[e] >= 0)
            def _():
                pltpu.make_async_copy(zbuf, xp_hbm.at[pl.ds(pl.multiple_of(zs_ref[e] * nc, nc), zr), :], zsem).start()
        for e in range(2 * n_e):
            @pl.when(zs_ref[e] >= 0)
            def _():
                pltpu.make_async_copy(zbuf, xp_hbm.at[pl.ds(pl.multiple_of(zs_ref[e] * nc, nc), zr), :], zsem).wait()

    def issue(t, c):
        for kk in range(TOP_K):
            dst = dst_ref[t * TOP_K + kk]
            pltpu.make_async_copy(x_ref.at[_tok_rows(t, nc), :], xp_hbm.at[_tok_rows(dst, nc), :],
                                  sem).start(priority=kk % 2)
        return c

    lax.fori_loop(0, tt, issue, 0, unroll=8)
    _row_copy_wait(xp_hbm, xp_hbm, sem, tt * TOP_K * nc)


def _dispatch(dest, zstart, xr, n_rows, nc):
    t = xr.shape[0] // nc
    tt = max(k * TOK_TILE for k in (1, 2, 4) if t % (k * TOK_TILE) == 0)
    return pl.pallas_call(
        functools.partial(_dispatch_body, tt=tt, nc=nc, n_e=N_EXPERTS),
        grid=(t // tt,),
        in_specs=[pl.BlockSpec((tt * TOP_K,), lambda i: (i,), memory_space=pltpu.SMEM),
                  pl.BlockSpec(memory_space=pltpu.SMEM),
                  pl.BlockSpec((tt * nc, LANES), lambda i: (i, 0))],
        out_specs=pl.BlockSpec(memory_space=pl.ANY),
        out_shape=jax.ShapeDtypeStruct((n_rows * nc, LANES), xr.dtype),
        scratch_shapes=[pltpu.VMEM((MOE_ROWS * nc, LANES), xr.dtype),
                        pltpu.SemaphoreType.DMA, pltpu.SemaphoreType.DMA],
        compiler_params=_cparams(("arbitrary",)),
        name="moe_dispatch",
    )(dest, zstart, xr)


def _expert_body(be_ref, nu_ref, x_ref, wu_ref, bu_ref, wd_ref, bd_ref, y_ref, wub, wdb):
    i = pl.program_id(0)
    e = be_ref[i]
    prev = be_ref[jnp.maximum(i - 1, 0)]

    @pl.when((i == 0) | (e != prev))
    def _():
        wub[...] = wu_ref[...].astype(BF16)
        wdb[...] = wd_ref[...].astype(BF16)

    nc = wu_ref.shape[0] // LANES
    ncp = nc // 2
    r = y_ref.shape[0] // nc

    @pl.when(i < nu_ref[0])
    def _():
        words = [x_ref[pl.ds(c, r, stride=ncp), :] for c in range(ncp)]
        x = jnp.concatenate(
            [lax.bitcast_convert_type(w << 16, F32) for w in words]
            + [lax.bitcast_convert_type(w & jnp.uint32(0xFFFF0000), F32) for w in words], axis=-1).astype(BF16)
        ff = wdb.shape[0]
        y = bd_ref[...]
        for j in range(ff // FF_SLICE):
            lo, hi = j * FF_SLICE, (j + 1) * FF_SLICE
            h_glu = jnp.dot(x, wub[:, lo:hi], preferred_element_type=F32) + bu_ref[:, lo:hi]
            h_lin = jnp.dot(x, wub[:, ff + lo:ff + hi], preferred_element_type=F32) + bu_ref[:, ff + lo:ff + hi]
            h_glu = jnp.minimum(h_glu, SWIGLU_LIMIT)
            h_lin = jnp.clip(h_lin, -SWIGLU_LIMIT, SWIGLU_LIMIT)
            a = h_glu * _sigmoid(SWIGLU_ALPHA * h_glu) * (h_lin + 1.0)
            y = y + jnp.dot(a.astype(BF16), wdb[lo:hi, :], preferred_element_type=F32)
        for c in range(nc):
            y_ref[pl.ds(c, r, stride=nc), :] = y[:, c * LANES:(c + 1) * LANES]

    @pl.when(i >= nu_ref[0])
    def _():
        y_ref[...] = jnp.zeros_like(y_ref)


def _experts(blk_e, n_used, xp, layer, w_up, b_up, w_down, b_down):
    _, n_e, d, two_ff = w_up.shape
    nc = d // LANES
    ncp = nc // 2
    n_rows = xp.shape[0] // ncp
    r = MOE_ROWS
    ff = two_ff // 2
    grid_spec = pltpu.PrefetchScalarGridSpec(
        num_scalar_prefetch=2,
        grid=(n_rows // r,),
        in_specs=[
            pl.BlockSpec((r * ncp, LANES), lambda i, be, nu: (jnp.minimum(i, nu[0] - 1), 0)),
            pl.BlockSpec((None, None, d, two_ff), lambda i, be, nu: (layer, be[i], 0, 0)),
            pl.BlockSpec((None, None, 1, two_ff), lambda i, be, nu: (layer, be[i], 0, 0)),
            pl.BlockSpec((None, None, ff, d), lambda i, be, nu: (layer, be[i], 0, 0)),
            pl.BlockSpec((None, None, 1, d), lambda i, be, nu: (layer, be[i], 0, 0)),
        ],
        out_specs=pl.BlockSpec((r * nc, LANES), lambda i, be, nu: (i, 0)),
        scratch_shapes=[pltpu.VMEM((d, two_ff), BF16), pltpu.VMEM((ff, d), BF16)],
    )
    return pl.pallas_call(
        _expert_body,
        grid_spec=grid_spec,
        out_shape=jax.ShapeDtypeStruct((n_rows * nc, LANES), F32),
        compiler_params=_cparams(("arbitrary",)),
        name="moe_experts",
    )(blk_e, n_used, xp, w_up, b_up.reshape(-1, n_e, 1, two_ff), w_down, b_down.reshape(-1, n_e, 1, d))


def _combine_body(dst_ref, gt_ref, h_ref, mod_ref, lng_ref, lnb_ref, yp_hbm,
                  o_ref, buf, sems, *, tt, nc, n_tiles, alpha):
    s = pl.program_id(0)
    slot = s % 2

    @pl.when(s < n_tiles)
    def _():
        def issue(t, c):
            for kk in range(TOP_K):
                q = t * TOP_K + kk
                src = dst_ref[q]
                pltpu.make_async_copy(yp_hbm.at[_tok_rows(src, nc), :], buf.at[slot, kk, _tok_rows(t, nc), :],
                                      sems.at[slot]).start(priority=kk % 2)
            return c

        lax.fori_loop(0, tt, issue, 0, unroll=8)

    @pl.when(s > 0)
    def _():
        prev = 1 - slot
        for kk in range(TOP_K):
            _row_copy_wait(yp_hbm, buf.at[prev, kk], sems.at[prev], tt * nc)
        gt = gt_ref[...]
        cols = []
        for c in range(nc):
            acc = gt[:, 0:1] * buf[prev, 0, pl.ds(c, tt, stride=nc), :]
            for kk in range(1, TOP_K):
                acc = acc + gt[:, kk:kk + 1] * buf[prev, kk, pl.ds(c, tt, stride=nc), :]
            cols.append(acc)
        f = jnp.concatenate(cols, axis=-1)
        o_ref[...] = _layer_norm_rows(alpha * h_ref[...] + mod_ref[5:6, :] * f, lng_ref[...], lnb_ref[...])


def _combine(dest, gates, h1, modtab, ln_g, ln_b, yp, *, alpha, tiles_per_batch, ctx_tile0):
    t, d = h1.shape
    tt = TOK_TILE
    nc = d // LANES
    n_tiles = t // tt
    nxt = lambda s: jnp.minimum(s, n_tiles - 1)
    cur = lambda s: jnp.maximum(s - 1, 0)
    smem_blk = pl.BlockSpec((tt * TOP_K,), lambda s: (nxt(s),), memory_space=pltpu.SMEM)
    mod_map = lambda s: (cur(s) // tiles_per_batch, ((cur(s) % tiles_per_batch) >= ctx_tile0).astype(I32), 0, 0)
    vec = lambda a: a.reshape(1, -1).astype(F32)
    return pl.pallas_call(
        functools.partial(_combine_body, tt=tt, nc=nc, n_tiles=n_tiles, alpha=alpha),
        grid=(n_tiles + 1,),
        in_specs=[smem_blk,
                  pl.BlockSpec((tt, ROUTE_LANES), lambda s: (cur(s), 0)),
                  pl.BlockSpec((tt, d), lambda s: (cur(s), 0)),
                  pl.BlockSpec((None, None, 6, d), mod_map),
                  pl.BlockSpec((1, d), lambda s: (0, 0)),
                  pl.BlockSpec((1, d), lambda s: (0, 0)),
                  pl.BlockSpec(memory_space=pl.ANY)],
        out_specs=pl.BlockSpec((tt, d), lambda s: (cur(s), 0)),
        out_shape=jax.ShapeDtypeStruct((t, d), F32),
        scratch_shapes=[pltpu.VMEM((2, TOP_K, tt * nc, LANES), F32), pltpu.SemaphoreType.DMA((2,))],
        compiler_params=_cparams(("arbitrary",)),
        name="moe_combine",
    )(dest, gates, h1, modtab, vec(ln_g), vec(ln_b), yp)


def _moe_layer(h1, xr, meta, gt, counts, modtab, ln_g, ln_b, layer, w_up, b_up, w_down, b_down, *,
               alpha, tiles_per_batch, ctx_tile0):
    t, d = h1.shape
    n_e = w_up.shape[1]
    r = MOE_ROWS
    n_blocks = -(-(t * TOP_K) // r) + n_e
    cnt = counts[:, 0].astype(I32)
    padded = (cnt + r - 1) // r * r
    pends = jnp.cumsum(padded)
    pstart = pends - padded
    tail = pends[-1] + jnp.arange(n_e, dtype=I32) * r
    zstart = jnp.concatenate([jnp.where(cnt > 0, pends - r, -1),
                              jnp.where(tail < n_blocks * r, tail, -1)]).astype(I32)
    blk_row0 = jnp.arange(n_blocks, dtype=I32) * r
    blk_e = jnp.minimum(jnp.sum((pends[None, :] <= blk_row0[:, None]).astype(I32), axis=1), n_e - 1)
    n_used = (pends[-1] // r).astype(I32).reshape(1)
    ti, rk = meta[:TOP_K], meta[TOP_K:2 * TOP_K]
    first = jnp.sum(jnp.where(ti[:, :, None] == jnp.arange(n_e, dtype=I32), pstart, 0), axis=-1)
    dest = (first + rk).T.reshape(-1).astype(I32)
    xp = _dispatch(dest, zstart, xr, n_blocks * r, d // (2 * LANES))
    yp = _experts(blk_e, n_used, xp, layer, w_up, b_up, w_down, b_down)
    return _combine(dest, gt, h1, modtab, ln_g, ln_b, yp,
                    alpha=alpha, tiles_per_batch=tiles_per_batch, ctx_tile0=ctx_tile0)


def _to_cm_body(lat_ref, ctx_ref, o_ref, *, rows, k, n_lat_steps):
    w = pl.program_id(1)
    for q in range(SUBLANES // k):
        @pl.when((w < n_lat_steps) & (w % (SUBLANES // k) == q))
        def _():
            for i in range(k):
                o_ref[i * rows:(i + 1) * rows, :] = lat_ref[:, q * k + i, :]

    @pl.when(w >= n_lat_steps)
    def _():
        o_ref[...] = ctx_ref[...]


def _to_col_major(h3, n_lat):
    bn, ltot, d = h3.shape
    lc = ltot - n_lat
    rows = n_lat // GRID_W
    k = max(kk for kk in (1, 2, 4, 8) if lc % (kk * rows) == 0)
    ob = k * rows
    n_lat_steps = GRID_W // k
    per_blk = SUBLANES // k
    return pl.pallas_call(
        functools.partial(_to_cm_body, rows=rows, k=k, n_lat_steps=n_lat_steps),
        grid=(bn, n_lat_steps + lc // ob),
        in_specs=[pl.BlockSpec((None, rows, SUBLANES, d),
                               lambda b, w: (b, 0, jnp.minimum(w, n_lat_steps - 1) // per_blk, 0)),
                  pl.BlockSpec((None, ob, d), lambda b, w: (b, jnp.maximum(w, n_lat_steps), 0))],
        out_specs=pl.BlockSpec((None, ob, d), lambda b, w: (b, w, 0)),
        out_shape=jax.ShapeDtypeStruct(h3.shape, h3.dtype),
        compiler_params=_cparams(("arbitrary", "arbitrary")),
        name="to_col_major",
    )(h3.reshape(bn, ltot // GRID_W, GRID_W, d), h3)


def _to_rm_body(lat_ref, ctx_ref, o_ref, *, rows, n_ctx_rows):
    w = pl.program_id(1)
    for i in range(SUBLANES):
        o_ref[0:rows, i, :] = lat_ref[i * rows:(i + 1) * rows, :]
    for m in range(n_ctx_rows):
        o_ref[rows + m, :, :] = ctx_ref[pl.ds(pl.multiple_of(m * GRID_W + w * SUBLANES, SUBLANES), SUBLANES), :]


def _to_row_major(h3, n_lat, keep_ctx=True):
    bn, ltot, d = h3.shape
    lc = ltot - n_lat
    rows = n_lat // GRID_W
    assert lc % GRID_W == 0 and n_lat % lc == 0
    n_ctx_rows = lc // GRID_W if keep_ctx else 0
    out = pl.pallas_call(
        functools.partial(_to_rm_body, rows=rows, n_ctx_rows=n_ctx_rows),
        grid=(bn, GRID_W // SUBLANES),
        in_specs=[pl.BlockSpec((None, SUBLANES * rows, d), lambda b, w: (b, w, 0)),
                  pl.BlockSpec((None, lc, d), lambda b, w: (b, n_lat // lc, 0))],
        out_specs=pl.BlockSpec((None, rows + n_ctx_rows, SUBLANES, d), lambda b, w: (b, 0, w, 0)),
        out_shape=jax.ShapeDtypeStruct((bn, rows + n_ctx_rows, GRID_W, d), h3.dtype),
        compiler_params=_cparams(("arbitrary", "arbitrary")),
        name="to_row_major",
    )(h3, h3)
    return out.reshape(bn, (rows + n_ctx_rows) * GRID_W, d)


def kernel(x, c, ctx, c_ctx, w_ada, b_ada, ln1_g, ln1_b, ln2_g, ln2_b, s5_lam_re, s5_lam_im, s5_log_step, s5_b_re, s5_b_im, s5_c_re, s5_c_im, s5_d, s5_w_glu, gla_w_in, gla_w_a2, gla_b_a2, gla_norm_g, gla_w_out, moe_w_router, moe_b_router, moe_w_up, moe_b_up, moe_w_down, moe_b_down):
    bn, l, d = x.shape
    lc = ctx.shape[1]
    depth = w_ada.shape[0]
    ltot = l + lc
    rows = l // GRID_W
    alpha = (2 * depth) ** 0.25
    assert l % TOK_TILE == 0 and lc % TOK_TILE == 0 and l % S5_CHUNK == 0 and lc % S5_CHUNK == 0
    assert bn < SUBLANES
    tiles_per_batch = ltot // TOK_TILE
    ctx_tile0 = l // TOK_TILE
    tile_kw = dict(tiles_per_batch=tiles_per_batch, ctx_tile0=ctx_tile0)

    cc = jnp.zeros((SUBLANES, d), F32).at[:bn].set(c.astype(F32)).at[bn].set(c_ctx.astype(F32))
    mod = _ada_table(cc, w_ada.astype(F32), b_ada.astype(F32))
    mod_lat = mod[:, :bn].reshape(depth, bn, 1, 6, d)
    mod_ctx = jnp.broadcast_to(mod[:, bn].reshape(depth, 1, 1, 6, d), (depth, bn, 1, 6, d))
    modtab = jnp.concatenate([mod_lat, mod_ctx], axis=2)

    to_cm = functools.partial(_to_col_major, n_lat=l)
    to_rm = functools.partial(_to_row_major, n_lat=l)

    h3 = jnp.concatenate([x.astype(F32), ctx.astype(F32)], axis=1)
    for i in range(depth):
        j = i // 2
        mt = modtab[i]
        if i % 2 == 0:
            ys = []
            for dr, rev in ((0, False), (1, True)):
                wb, wc, a_re, a_im = _s5_prepare(s5_lam_re[j, dr], s5_lam_im[j, dr], s5_log_step[j, dr],
                                                 s5_b_re[j, dr], s5_b_im[j, dr], s5_c_re[j, dr], s5_c_im[j, dr], bn)
                ys.append(_s5_scan(h3, mt, wb, wc, a_re, a_im, rev=rev, n_lat=l).reshape(bn * ltot, d))
            h2d = h3.reshape(bn * ltot, d)
            outs = _post_mixer("s5", h2d, mt, ys[0], ys[1], (s5_d[j],), s5_w_glu[j].astype(BF16),
                               ln1_g[i], ln1_b[i], moe_w_router[i], moe_b_router[i], alpha=alpha, **tile_kw)
        else:
            h3 = to_cm(h3)
            h2d = h3.reshape(bn * ltot, d)
            w_in = gla_w_in[j]
            n_main = w_in.shape[1] - 2 * GLA_GATE_RANK
            w_a = w_in[:, n_main:].reshape(d, 2, GLA_GATE_RANK).transpose(1, 0, 2).astype(BF16)
            q, k, v, g, la_f, la_b = _gla_proj(h2d, mt, w_in[:, :n_main].astype(BF16), w_a,
                                               gla_w_a2[j].astype(BF16), gla_b_a2[j], **tile_kw)
            r3 = lambda a: a.reshape(bn, ltot, a.shape[-1])
            o_f, o_b = [o.reshape(bn * ltot, d) for o in _gla_rec(r3(q), r3(k), r3(v), r3(la_f), r3(la_b), n_lat=l)]
            outs = _post_mixer("gla", h2d, mt, o_f, o_b, (g, gla_norm_g[j]), gla_w_out[j].astype(BF16),
                               ln1_g[i], ln1_b[i], moe_w_router[i], moe_b_router[i], alpha=alpha, **tile_kw)
        h1, xr, meta, gt, counts = outs
        h2 = _moe_layer(h1, xr, meta, gt, counts, mt, ln2_g[i], ln2_b[i],
                        i, moe_w_up, moe_b_up, moe_w_down, moe_b_down, alpha=alpha, **tile_kw)
        h3 = h2.reshape(bn, ltot, d)
        if i % 2 == 1:
            h3 = to_rm(h3, keep_ctx=i < depth - 1)
    return h3[:, :l].astype(x.dtype)
```

```python
import functools
import math

import jax
import jax.numpy as jnp
from jax import lax
from jax.experimental import pallas as pl
from jax.experimental.pallas import tpu as pltpu

F32 = jnp.float32
BF16 = jnp.bfloat16
I32 = jnp.int32
HIGHEST = lax.Precision.HIGHEST

GRID_W = 64
S5_GROUP = 16
S5_STATE = 64
GLA_HEADS = 4
GLA_GATE_RANK = 16
GLA_GATE_NORM = 16.0
GLA_CHUNK = 64
GLA_STEP_CHUNKS = 2
N_EXPERTS = 32
TOP_K = 4
SWIGLU_ALPHA = 1.702
SWIGLU_LIMIT = 7.0
LN_EPS = 1e-5

LANES = 128
SUBLANES = 8
MXU_K = 256
VMEM_LIMIT = 56 * 1024 * 1024

TOK_TILE = 256
S5_CHUNK = 128
S5_PITCH = S5_CHUNK + SUBLANES // 2
MOE_ROWS = 512
FF_SLICE = 1024
ROUTE_LANES = LANES


def _sigmoid(x):
    return 1.0 / (1.0 + jnp.exp(-x))


def _cparams(sem):
    return pltpu.CompilerParams(dimension_semantics=sem, vmem_limit_bytes=VMEM_LIMIT)


def _ada_body(c_ref, w_ref, b_ref, o_ref):
    c = c_ref[...]
    cond = c * _sigmoid(c)
    o_ref[...] = jnp.dot(cond, w_ref[...], precision=HIGHEST, preferred_element_type=F32) + b_ref[...]


def _ada_table(cc, w_ada, b_ada):
    depth, d, six_d = w_ada.shape
    n_tiles = six_d // d
    return pl.pallas_call(
        _ada_body,
        grid=(depth, n_tiles),
        in_specs=[
            pl.BlockSpec((SUBLANES, d), lambda i, n: (0, 0)),
            pl.BlockSpec((None, d, d), lambda i, n: (i, 0, n)),
            pl.BlockSpec((None, 1, d), lambda i, n: (i, 0, n)),
        ],
        out_specs=pl.BlockSpec((None, SUBLANES, d), lambda i, n: (i, 0, n)),
        out_shape=jax.ShapeDtypeStruct((depth, SUBLANES, six_d), F32),
        compiler_params=_cparams(("arbitrary", "arbitrary")),
        name="ada_table",
    )(cc, w_ada, b_ada.reshape(depth, 1, six_d))


def _s5_scan_body(x_ref, mod_ref, wb_ref, wc_ref, are_ref, aim_ref, y_ref,
                  sre_ref, sim_ref, st_re, st_im, u_ref, *, rev, tc, pitch, nb, n_kt, cpk, pk):
    j = pl.program_id(0)
    ncol = n_kt * cpk
    npc = ncol // pk
    half = cpk * LANES
    gr = nb * pitch

    @pl.when(j == 0)
    def _():
        st_re[...] = jnp.zeros_like(st_re)
        st_im[...] = jnp.zeros_like(st_im)
        u_ref[...] = jnp.zeros_like(u_ref)

    for b in range(nb):
        u_ref[b * pitch:b * pitch + tc, :] = x_ref[b] * (1.0 + mod_ref[b, 1:2, :]) + mod_ref[b, 0:1, :]
    u = u_ref[...].astype(BF16)
    for kt in range(n_kt):
        r = jnp.dot(u[:, MXU_K * kt:MXU_K * (kt + 1)], wb_ref[kt], preferred_element_type=F32)
        for c in range(cpk):
            col = kt * cpk + c
            grows = slice((col // npc) * gr, (col // npc + 1) * gr)
            sre_ref[col % npc, grows, :] = r[:, LANES * c:LANES * (c + 1)]
            sim_ref[col % npc, grows, :] = r[:, half + LANES * c:half + LANES * (c + 1)]

    grp = 8
    for cg in range(npc // grp):
        cols = list(range(cg * grp, (cg + 1) * grp))
        ar = [are_ref[c] for c in cols]
        ai = [aim_ref[c] for c in cols]
        init = tuple(st_re[c] for c in cols) + tuple(st_im[c] for c in cols)

        def step(t, carry, cols=cols, ar=ar, ai=ai):
            tt = (tc - 1 - t) if rev else t
            out_re, out_im = [], []
            for k, c in enumerate(cols):
                rows = pl.ds(tt, pk * nb, stride=pitch)
                pr, pi = carry[k], carry[grp + k]
                nr = ar[k] * pr - ai[k] * pi + sre_ref[c, rows, :]
                ni = ar[k] * pi + ai[k] * pr + sim_ref[c, rows, :]
                sre_ref[c, rows, :] = nr
                sim_ref[c, rows, :] = ni
                out_re.append(nr)
                out_im.append(ni)
            return tuple(out_re) + tuple(out_im)

        fin = lax.fori_loop(0, tc, step, init)
        for k, c in enumerate(cols):
            st_re[c] = fin[k]
            st_im[c] = fin[grp + k]

    def stacked(ref, kt):
        def piece(c):
            col = kt * cpk + c
            return ref[col % npc, (col // npc) * gr:(col // npc + 1) * gr, :]

        return jnp.concatenate([piece(c) for c in range(cpk)], axis=-1).astype(BF16)

    for kt in range(n_kt):
        y = (jnp.dot(stacked(sre_ref, kt), wc_ref[kt, :half, :], preferred_element_type=F32)
             + jnp.dot(stacked(sim_ref, kt), wc_ref[kt, half:, :], preferred_element_type=F32))
        for b in range(nb):
            y_ref[b, :, MXU_K * kt:MXU_K * (kt + 1)] = y[b * pitch:b * pitch + tc]


def _s5_scan(h3, modtab, wb, wc, a_re, a_im, *, rev, n_lat):
    nb, ltot, d = h3.shape
    tc = S5_CHUNK
    n_chunks = ltot // tc
    lat_chunks = n_lat // tc
    n_kt, _, two_half = wb.shape
    cpk = two_half // (2 * LANES)
    ncol = n_kt * cpk
    pk = SUBLANES // nb
    npc = ncol // pk

    if rev:
        chunk = lambda j: n_chunks - 1 - j
    else:
        chunk = lambda j: (j + lat_chunks) % n_chunks
    is_ctx = lambda j: (chunk(j) >= lat_chunks).astype(I32)

    body = functools.partial(_s5_scan_body, rev=rev, tc=tc, pitch=S5_PITCH, nb=nb, n_kt=n_kt, cpk=cpk, pk=pk)
    return pl.pallas_call(
        body,
        grid=(n_chunks,),
        in_specs=[
            pl.BlockSpec((nb, tc, d), lambda j: (0, chunk(j), 0)),
            pl.BlockSpec((nb, None, 6, d), lambda j: (0, is_ctx(j), 0, 0)),
            pl.BlockSpec(wb.shape, lambda j: (0, 0, 0)),
            pl.BlockSpec(wc.shape, lambda j: (0, 0, 0)),
            pl.BlockSpec(a_re.shape, lambda j: (0, 0, 0)),
            pl.BlockSpec(a_im.shape, lambda j: (0, 0, 0)),
        ],
        out_specs=pl.BlockSpec((nb, tc, d), lambda j: (0, chunk(j), 0)),
        out_shape=jax.ShapeDtypeStruct((nb, ltot, d), F32),
        scratch_shapes=[
            pltpu.VMEM((npc, pk * nb * S5_PITCH, LANES), F32),
            pltpu.VMEM((npc, pk * nb * S5_PITCH, LANES), F32),
            pltpu.VMEM((npc, pk * nb, LANES), F32),
            pltpu.VMEM((npc, pk * nb, LANES), F32),
            pltpu.VMEM((nb * S5_PITCH, d), F32),
        ],
        compiler_params=_cparams(("arbitrary",)),
        name="s5_scan_bwd" if rev else "s5_scan_fwd",
    )(h3, modtab, wb, wc, a_re, a_im)


def _s5_prepare(lam_re, lam_im, log_step, b_re, b_im, c_re, c_im, nb):
    g, p = lam_re.shape
    hg = b_re.shape[-1]
    lr = lam_re.astype(F32)
    li = lam_im.astype(F32)
    dt = jnp.exp(log_step.astype(F32))[:, None]
    mag = jnp.exp(lr * dt)
    ar = mag * jnp.cos(li * dt)
    ai = mag * jnp.sin(li * dt)
    den = lr * lr + li * li
    nr = ar - 1.0
    kr = (nr * lr + ai * li) / den
    ki = (ai * lr - nr * li) / den
    br = b_re.astype(F32)
    bi = b_im.astype(F32)
    bbr = kr[..., None] * br - ki[..., None] * bi
    bbi = kr[..., None] * bi + ki[..., None] * br
    gpk = MXU_K // hg
    n_kt = g // gpk
    eye = jnp.eye(gpk, dtype=F32)

    def block_diag(t):
        full = t[:, :, :, None, :] * eye[None, :, None, :, None]
        return full.reshape(n_kt, gpk * t.shape[2], gpk * t.shape[3])

    def in_blocks(bb):
        return block_diag(bb.reshape(n_kt, gpk, p, hg).transpose(0, 1, 3, 2))

    def out_blocks(cc):
        return block_diag(cc.reshape(n_kt, gpk, hg, p).transpose(0, 1, 3, 2))

    wb = jnp.concatenate([in_blocks(bbr), in_blocks(bbi)], axis=-1).astype(BF16)
    wc = jnp.concatenate([out_blocks(c_re.astype(F32)), -out_blocks(c_im.astype(F32))], axis=1).astype(BF16)
    ncol = g * p // LANES
    pk = SUBLANES // nb
    npc = ncol // pk

    def packed_rows(a):
        t = a.reshape(pk, npc, 1, LANES).transpose(1, 0, 2, 3)
        return jnp.broadcast_to(t, (npc, pk, nb, LANES)).reshape(npc, pk * nb, LANES)

    return wb, wc, packed_rows(ar), packed_rows(ai)


def _gla_proj_body(x_ref, mod_ref, w_ref, wa_ref, wa2_ref, ba2_ref,
                   q_ref, k_ref, v_ref, g_ref, laf_ref, lab_ref, *, qk_w, v_w, dk):
    u = (x_ref[...] * (1.0 + mod_ref[1:2, :]) + mod_ref[0:1, :]).astype(BF16)
    p = jnp.dot(u, w_ref[...], preferred_element_type=F32)
    q_ref[...] = p[:, :qk_w] * (dk ** -0.5)
    k_ref[...] = p[:, qk_w:2 * qk_w]
    v_ref[...] = p[:, 2 * qk_w:2 * qk_w + v_w]
    g_ref[...] = p[:, 2 * qk_w + v_w:]
    for dr, out in enumerate((laf_ref, lab_ref)):
        a_d = jnp.dot(u, wa_ref[dr], preferred_element_type=F32).astype(BF16)
        z = jnp.dot(a_d, wa2_ref[dr], preferred_element_type=F32) + ba2_ref[dr]
        out[...] = (jnp.minimum(z, 0.0) - jnp.log1p(jnp.exp(-jnp.abs(z)))) / GLA_GATE_NORM


def _gla_proj(h2d, modtab, w_main, w_a, w_a2, b_a2, *, tiles_per_batch, ctx_tile0):
    t, d = h2d.shape
    tt = TOK_TILE
    qk_w = w_a2.shape[-1]
    v_w = (w_main.shape[1] - 2 * qk_w) // 2
    dk = qk_w // GLA_HEADS
    mod_map = lambda i: (i // tiles_per_batch, ((i % tiles_per_batch) >= ctx_tile0).astype(I32), 0, 0)
    row = lambda w: pl.BlockSpec((tt, w), lambda i: (i, 0))
    full = lambda a: pl.BlockSpec(a.shape, lambda i: (0,) * a.ndim)
    body = functools.partial(_gla_proj_body, qk_w=qk_w, v_w=v_w, dk=dk)
    b_a2r = b_a2.reshape(2, 1, qk_w).astype(F32)
    return pl.pallas_call(
        body,
        grid=(t // tt,),
        in_specs=[row(d), pl.BlockSpec((None, None, 6, d), mod_map),
                  full(w_main), full(w_a), full(w_a2), full(b_a2r)],
        out_specs=[row(qk_w), row(qk_w), row(v_w), row(v_w), row(qk_w), row(qk_w)],
        out_shape=[jax.ShapeDtypeStruct((t, w), F32) for w in (qk_w, qk_w, v_w, v_w, qk_w, qk_w)],
        compiler_params=_cparams(("parallel",)),
        name="gla_proj",
    )(h2d, modtab, w_main, w_a, w_a2, b_a2r)


def _gla_rec_body(qf_ref, kf_ref, vf_ref, laf_ref, qb_ref, kb_ref, vb_ref, lab_ref, of_ref, ob_ref, *s_refs,
                  ch, dk, dv, nh, nb):
    @pl.when(pl.program_id(0) == 0)
    def _():
        for s_ref in s_refs:
            s_ref[...] = jnp.zeros_like(s_ref)

    spc = qf_ref.shape[1] // ch
    for ci in range(spc):
        _gla_chunk(qf_ref, kf_ref, vf_ref, laf_ref, of_ref, s_refs[:nh], slice(ci * ch, (ci + 1) * ch),
                   rev=False, ch=ch, dk=dk, dv=dv, nh=nh, nb=nb)
        cb = spc - 1 - ci
        _gla_chunk(qb_ref, kb_ref, vb_ref, lab_ref, ob_ref, s_refs[nh:], slice(cb * ch, (cb + 1) * ch),
                   rev=True, ch=ch, dk=dk, dv=dv, nh=nh, nb=nb)


def _gla_chunk(q_ref, k_ref, v_ref, la_ref, o_ref, s_refs, rows, *, rev, ch, dk, dv, nh, nb):
    n = nb * ch
    stack = lambda ref: jnp.concatenate([ref[bi, rows, :] for bi in range(nb)], axis=0)
    r_i = lax.broadcasted_iota(I32, (n, n), 0)
    c_i = lax.broadcasted_iota(I32, (n, n), 1)
    seen = ((r_i // ch) == (c_i // ch)) & ((c_i >= r_i) if rev else (c_i <= r_i))
    end = 0 if rev else ch - 1
    k_all = stack(k_ref)
    v_all = stack(v_ref).astype(BF16)
    b = jnp.dot(seen.astype(F32), stack(la_ref), precision=HIGHEST, preferred_element_type=F32)
    b_ends = [b[bi * ch + end:bi * ch + end + 1, :] for bi in range(nb)]
    b_end = jnp.concatenate([jnp.broadcast_to(be, (ch, nh * dk)) for be in b_ends], axis=0)
    q_d_all = (stack(q_ref) * jnp.exp(b)).astype(BF16)
    k_d_all = (k_all * jnp.exp(-b)).astype(BF16)
    k_e_all = (k_all * jnp.exp(b_end - b)).astype(BF16)
    own = (lax.broadcasted_iota(I32, (n, nb * dk), 0) // ch) == (lax.broadcasted_iota(I32, (n, nb * dk), 1) // dk)
    zero = jnp.zeros((n, nb * dk), BF16)
    for hd in range(nh):
        qs = slice(hd * dk, (hd + 1) * dk)
        vs = slice(hd * dv, (hd + 1) * dv)
        q_d, k_d, k_e = q_d_all[:, qs], k_d_all[:, qs], k_e_all[:, qs]
        v = v_all[:, vs]
        att = lax.dot_general(q_d, k_d, (((1,), (1,)), ((), ())), preferred_element_type=F32)
        att = jnp.where(seen, att, 0.0).astype(BF16)
        q_bd = jnp.where(own, jnp.concatenate([q_d] * nb, axis=1), zero)
        k_bd = jnp.where(own, jnp.concatenate([k_e] * nb, axis=1), zero)
        s_t = s_refs[hd][...]
        o = (jnp.dot(att, v, preferred_element_type=F32)
             + lax.dot_general(q_bd, s_t.astype(BF16), (((1,), (1,)), ((), ())), preferred_element_type=F32))
        for bi in range(nb):
            o_ref[bi, rows, vs] = o[bi * ch:(bi + 1) * ch]
        g_row = jnp.concatenate([jnp.exp(be[:, qs]) for be in b_ends], axis=1)
        upd = lax.dot_general(v, k_bd, (((0,), (0,)), ((), ())), preferred_element_type=F32)
        s_refs[hd][...] = s_t * g_row + upd


def _gla_rec(q, k, v, la_f, la_b, *, n_lat):
    nb, ltot, qk_w = q.shape
    v_w = v.shape[-1]
    ch = GLA_CHUNK
    blk = GLA_STEP_CHUNKS * ch
    assert n_lat % blk == 0 and (ltot - n_lat) % blk == 0
    n_chunks = ltot // blk
    lat_chunks = n_lat // blk
    fwd = lambda j: (j + lat_chunks) % n_chunks
    bwd = lambda j: n_chunks - 1 - j
    spec = lambda w, chunk: pl.BlockSpec((nb, blk, w), lambda j: (0, chunk(j), 0))
    dk, dv = qk_w // GLA_HEADS, v_w // GLA_HEADS
    body = functools.partial(_gla_rec_body, ch=ch, dk=dk, dv=dv, nh=GLA_HEADS, nb=nb)
    out = jax.ShapeDtypeStruct((nb, ltot, v_w), F32)
    return pl.pallas_call(
        body,
        grid=(n_chunks,),
        in_specs=[spec(qk_w, fwd), spec(qk_w, fwd), spec(v_w, fwd), spec(qk_w, fwd),
                  spec(qk_w, bwd), spec(qk_w, bwd), spec(v_w, bwd), spec(qk_w, bwd)],
        out_specs=[spec(v_w, fwd), spec(v_w, bwd)],
        out_shape=[out, out],
        scratch_shapes=[pltpu.VMEM((dv, nb * dk), F32) for _ in range(2 * GLA_HEADS)],
        compiler_params=_cparams(("arbitrary",)),
        name="gla_rec",
    )(q, k, v, la_f, q, k, v, la_b)


def _layer_norm_rows(v, g, b):
    mu = jnp.mean(v, axis=-1, keepdims=True)
    c = v - mu
    var = jnp.mean(c * c, axis=-1, keepdims=True)
    return c * lax.rsqrt(var + LN_EPS) * g + b


def _post_mixer_body(*refs, kind, alpha, tt, dv):
    if kind == "s5":
        (h_ref, mod_ref, ya_ref, yb_ref, dsk_ref, w_ref, lng_ref, lnb_ref, wr_ref, br_ref,
         h1_ref, xr_ref, meta_ref, gt_ref, cnt_ref, base_ref) = refs
    else:
        (h_ref, mod_ref, ya_ref, yb_ref, gate_ref, ng_ref, w_ref, lng_ref, lnb_ref, wr_ref, br_ref,
         h1_ref, xr_ref, meta_ref, gt_ref, cnt_ref, base_ref) = refs
    i = pl.program_id(0)

    @pl.when(i == 0)
    def _():
        base_ref[...] = jnp.zeros_like(base_ref)

    h = h_ref[...]
    d = h.shape[-1]
    if kind == "s5":
        u = h * (1.0 + mod_ref[1:2, :]) + mod_ref[0:1, :]
        y = dsk_ref[...] * u + ya_ref[...] + yb_ref[...]
        ge = 0.5 * y * (1.0 + jnp.tanh(math.sqrt(2.0 / math.pi) * (y + 0.044715 * (y * y * y))))
        z = jnp.dot(ge.astype(BF16), w_ref[...], preferred_element_type=F32)
        mix = z[:, :d] * _sigmoid(z[:, d:])
    else:
        o = ya_ref[...] + yb_ref[...]
        parts = []
        for hd in range(d // dv):
            oh = o[:, hd * dv:(hd + 1) * dv]
            ms = jnp.mean(oh * oh, axis=-1, keepdims=True)
            parts.append(oh * lax.rsqrt(ms + LN_EPS))
        on = jnp.concatenate(parts, axis=-1) * ng_ref[...]
        gv = gate_ref[...]
        a = on * (gv * _sigmoid(gv))
        mix = jnp.dot(a.astype(BF16), w_ref[...], preferred_element_type=F32)

    h1 = _layer_norm_rows(alpha * h + mod_ref[2:3, :] * mix, lng_ref[...], lnb_ref[...])
    h1_ref[...] = h1
    u2 = h1 * (1.0 + mod_ref[4:5, :]) + mod_ref[3:4, :]
    ncp = d // (2 * LANES)
    lo = lax.bitcast_convert_type(u2[:, :d // 2].astype(BF16).astype(F32), jnp.uint32)
    hi = lax.bitcast_convert_type(u2[:, d // 2:].astype(BF16).astype(F32), jnp.uint32)
    words = (lo >> 16) | (hi & jnp.uint32(0xFFFF0000))
    for c in range(ncp):
        xr_ref[pl.ds(c, tt, stride=ncp), :] = words[:, c * LANES:(c + 1) * LANES]

    logits = lax.dot_general(wr_ref[...], u2, (((1,), (1,)), ((), ())), precision=HIGHEST,
                             preferred_element_type=F32) + br_ref[...]
    n_e = logits.shape[0]
    erow = lax.broadcasted_iota(I32, logits.shape, 0)
    work = logits
    vals, idxs = [], []
    for _ in range(TOP_K):
        m = jnp.max(work, axis=0, keepdims=True)
        idx = jnp.min(jnp.where(work == m, erow, n_e), axis=0, keepdims=True)
        vals.append(m)
        idxs.append(idx)
        work = jnp.where(erow == idx, -jnp.inf, work)
    exps = [jnp.exp(v - vals[0]) for v in vals]
    den = exps[0]
    for e in exps[1:]:
        den = den + e

    multi = jnp.zeros(logits.shape, F32)
    for idx in idxs:
        multi = multi + (erow == idx).astype(F32)
    r_i = lax.broadcasted_iota(I32, (tt, tt), 0)
    c_i = lax.broadcasted_iota(I32, (tt, tt), 1)
    earlier = (r_i < c_i).astype(BF16)
    pos = jnp.dot(multi.astype(BF16), earlier, preferred_element_type=F32) + base_ref[:, 0:1]
    row8 = lax.broadcasted_iota(I32, (SUBLANES, tt), 0)
    meta = jnp.zeros((SUBLANES, tt), I32)
    gts = jnp.zeros((SUBLANES, tt), F32)
    for kk in range(TOP_K):
        rank_k = jnp.sum(jnp.where(erow == idxs[kk], pos, 0.0), axis=0, keepdims=True)
        meta = jnp.where(row8 == kk, idxs[kk], meta)
        meta = jnp.where(row8 == TOP_K + kk, rank_k.astype(I32), meta)
        gts = jnp.where(row8 == kk, exps[kk] / den, gts)
    meta_ref[...] = meta
    gt_ref[...] = jnp.transpose(
        jnp.concatenate([gts, jnp.zeros((ROUTE_LANES - SUBLANES, tt), F32)], axis=0))
    new_base = base_ref[:, 0:1] + jnp.sum(multi, axis=1, keepdims=True)
    base_ref[...] = jnp.broadcast_to(new_base, base_ref.shape)
    cnt_ref[...] = jnp.broadcast_to(new_base, cnt_ref.shape)


def _post_mixer(kind, h2d, modtab, ya, yb, extra, w_mix, ln_g, ln_b, w_router, b_router, *,
                alpha, tiles_per_batch, ctx_tile0):
    t, d = h2d.shape
    tt = TOK_TILE
    n_e = w_router.shape[-1]
    wr = w_router.astype(F32).T
    br = b_router.astype(F32).reshape(n_e, 1)
    mod_map = lambda i: (i // tiles_per_batch, ((i % tiles_per_batch) >= ctx_tile0).astype(I32), 0, 0)
    row = lambda w: pl.BlockSpec((tt, w), lambda i: (i, 0))
    full = lambda a: pl.BlockSpec(a.shape, lambda i: (0,) * a.ndim)
    vec = lambda a: a.reshape(1, -1).astype(F32)
    if kind == "s5":
        (d_skip,) = extra
        ins = [h2d, modtab, ya, yb, vec(d_skip), w_mix, vec(ln_g), vec(ln_b), wr, br]
        in_specs = [row(d), pl.BlockSpec((None, None, 6, d), mod_map), row(d), row(d)]
        in_specs += [full(a) for a in ins[4:]]
        dv = d
    else:
        gate, norm_g = extra
        dv = norm_g.shape[-1]
        ng = jnp.tile(norm_g.astype(F32), d // dv).reshape(1, d)
        ins = [h2d, modtab, ya, yb, gate, ng, w_mix, vec(ln_g), vec(ln_b), wr, br]
        in_specs = [row(d), pl.BlockSpec((None, None, 6, d), mod_map), row(d), row(d), row(d)]
        in_specs += [full(a) for a in ins[5:]]
    body = functools.partial(_post_mixer_body, kind=kind, alpha=alpha, tt=tt, dv=dv)
    return pl.pallas_call(
        body,
        grid=(t // tt,),
        in_specs=in_specs,
        out_specs=[
            row(d),
            pl.BlockSpec((tt * (d // (2 * LANES)), LANES), lambda i: (i, 0)),
            pl.BlockSpec((SUBLANES, tt), lambda i: (0, i)),
            row(ROUTE_LANES),
            pl.BlockSpec((n_e, ROUTE_LANES), lambda i: (0, 0)),
        ],
        out_shape=[
            jax.ShapeDtypeStruct((t, d), F32),
            jax.ShapeDtypeStruct((t * (d // (2 * LANES)), LANES), jnp.uint32),
            jax.ShapeDtypeStruct((SUBLANES, t), I32),
            jax.ShapeDtypeStruct((t, ROUTE_LANES), F32),
            jax.ShapeDtypeStruct((n_e, ROUTE_LANES), F32),
        ],
        scratch_shapes=[pltpu.VMEM((n_e, ROUTE_LANES), F32)],
        compiler_params=_cparams(("arbitrary",)),
        name="post_mixer_" + kind,
    )(*ins)


def _row_copy_wait(src_hbm, dst, sem, n_rows):
    pltpu.make_async_copy(src_hbm.at[pl.ds(0, n_rows), :], dst.at[pl.ds(0, n_rows), :], sem).wait()


def _tok_rows(t, nc):
    return pl.ds(pl.multiple_of(t * nc, nc), nc)


def _dispatch_body(dst_ref, zs_ref, x_hbm, xp_hbm, zbuf, xbuf, fsems, rsems, zsem, *, tt, nc, n_e, n_tiles):
    zr = zbuf.shape[0]
    s = pl.program_id(0)
    slot = s % 3

    def fetch(tile, slot_):
        return pltpu.make_async_copy(x_hbm.at[pl.ds(pl.multiple_of(tile * (tt * nc), tt * nc), tt * nc), :],
                                     xbuf.at[slot_], fsems.at[slot_])

    @pl.when(s == 0)
    def _():
        fetch(0, 0).start()
        if n_tiles > 1:
            fetch(1, 1).start()

    @pl.when(pl.program_id(0) == 0)
    def _():
        zbuf[...] = jnp.zeros_like(zbuf)
        for e in range(2 * n_e):
            @pl.when(zs_ref[e] >= 0)
            def _():
                pltpu.make_async_copy(zbuf, xp_hbm.at[pl.ds(pl.multiple_of(zs_ref[e] * nc, nc), zr), :], zsem).start()
        for e in range(2 * n_e):
            @pl.when(zs_ref[e] >= 0)
            def _():
                pltpu.make_async_copy(zbuf, xp_hbm.at[pl.ds(pl.multiple_of(zs_ref[e] * nc, nc), zr), :], zsem).wait()

    fetch(s, slot).wait()
    rsem = rsems.at[s % 2]

    def issue(t, c):
        for kk in range(TOP_K):
            dst = dst_ref[t * TOP_K + kk]
            pltpu.make_async_copy(xbuf.at[slot, _tok_rows(t, nc), :], xp_hbm.at[_tok_rows(dst, nc), :],
                                  rsem).start(priority=kk % 2)
        return c

    lax.fori_loop(0, tt, issue, 0, unroll=8)

    @pl.when(s > 0)
    def _():
        _row_copy_wait(xp_hbm, xp_hbm, rsems.at[(s - 1) % 2], tt * TOP_K * nc)

    @pl.when(s + 2 < n_tiles)
    def _():
        fetch(s + 2, (s + 2) % 3).start()

    @pl.when(s == n_tiles - 1)
    def _():
        _row_copy_wait(xp_hbm, xp_hbm, rsem, tt * TOP_K * nc)


def _dispatch(dest, zstart, xr, n_rows, nc):
    t = xr.shape[0] // nc
    tt = max(k * TOK_TILE for k in (1, 2, 4) if t % (k * TOK_TILE) == 0)
    n_tiles = t // tt
    return pl.pallas_call(
        functools.partial(_dispatch_body, tt=tt, nc=nc, n_e=N_EXPERTS, n_tiles=n_tiles),
        grid=(n_tiles,),
        in_specs=[pl.BlockSpec((tt * TOP_K,), lambda i: (i,), memory_space=pltpu.SMEM),
                  pl.BlockSpec(memory_space=pltpu.SMEM),
                  pl.BlockSpec(memory_space=pl.ANY)],
        out_specs=pl.BlockSpec(memory_space=pl.ANY),
        out_shape=jax.ShapeDtypeStruct((n_rows * nc, LANES), xr.dtype),
        scratch_shapes=[pltpu.VMEM((MOE_ROWS * nc, LANES), xr.dtype),
                        pltpu.VMEM((3, tt * nc, LANES), xr.dtype),
                        pltpu.SemaphoreType.DMA((3,)), pltpu.SemaphoreType.DMA((2,)),
                        pltpu.SemaphoreType.DMA],
        compiler_params=_cparams(("arbitrary",)),
        name="moe_dispatch",
    )(dest, zstart, xr)


def _expert_body(be_ref, nu_ref, x_ref, wu_ref, bu_ref, wd_ref, bd_ref, y_ref, wub, wdb):
    i = pl.program_id(0)
    e = be_ref[i]
    prev = be_ref[jnp.maximum(i - 1, 0)]

    @pl.when((i == 0) | (e != prev))
    def _():
        wub[...] = wu_ref[...].astype(BF16)
        wdb[...] = wd_ref[...].astype(BF16)

    nc = wu_ref.shape[0] // LANES
    ncp = nc // 2
    r = y_ref.shape[0] // nc

    @pl.when(i < nu_ref[0])
    def _():
        words = [x_ref[pl.ds(c, r, stride=ncp), :] for c in range(ncp)]
        x = jnp.concatenate(
            [lax.bitcast_convert_type(w << 16, F32) for w in words]
            + [lax.bitcast_convert_type(w & jnp.uint32(0xFFFF0000), F32) for w in words], axis=-1).astype(BF16)
        ff = wdb.shape[0]
        y = bd_ref[...]
        for j in range(ff // FF_SLICE):
            lo, hi = j * FF_SLICE, (j + 1) * FF_SLICE
            h_glu = jnp.dot(x, wub[:, lo:hi], preferred_element_type=F32) + bu_ref[:, lo:hi]
            h_lin = jnp.dot(x, wub[:, ff + lo:ff + hi], preferred_element_type=F32) + bu_ref[:, ff + lo:ff + hi]
            h_glu = jnp.minimum(h_glu, SWIGLU_LIMIT)
            h_lin = jnp.clip(h_lin, -SWIGLU_LIMIT, SWIGLU_LIMIT)
            a = h_glu * _sigmoid(SWIGLU_ALPHA * h_glu) * (h_lin + 1.0)
            y = y + jnp.dot(a.astype(BF16), wdb[lo:hi, :], preferred_element_type=F32)
        for c in range(nc):
            y_ref[pl.ds(c, r, stride=nc), :] = y[:, c * LANES:(c + 1) * LANES]

    @pl.when(i >= nu_ref[0])
    def _():
        y_ref[...] = jnp.zeros_like(y_ref)


def _experts(blk_e, n_used, xp, layer, w_up, b_up, w_down, b_down):
    _, n_e, d, two_ff = w_up.shape
    nc = d // LANES
    ncp = nc // 2
    n_rows = xp.shape[0] // ncp
    r = MOE_ROWS
    ff = two_ff // 2
    grid_spec = pltpu.PrefetchScalarGridSpec(
        num_scalar_prefetch=2,
        grid=(n_rows // r,),
        in_specs=[
            pl.BlockSpec((r * ncp, LANES), lambda i, be, nu: (jnp.minimum(i, nu[0] - 1), 0)),
            pl.BlockSpec((None, None, d, two_ff), lambda i, be, nu: (layer, be[i], 0, 0)),
            pl.BlockSpec((None, None, 1, two_ff), lambda i, be, nu: (layer, be[i], 0, 0)),
            pl.BlockSpec((None, None, ff, d), lambda i, be, nu: (layer, be[i], 0, 0)),
            pl.BlockSpec((None, None, 1, d), lambda i, be, nu: (layer, be[i], 0, 0)),
        ],
        out_specs=pl.BlockSpec((r * nc, LANES), lambda i, be, nu: (i, 0)),
        scratch_shapes=[pltpu.VMEM((d, two_ff), BF16), pltpu.VMEM((ff, d), BF16)],
    )
    return pl.pallas_call(
        _expert_body,
        grid_spec=grid_spec,
        out_shape=jax.ShapeDtypeStruct((n_rows * nc, LANES), F32),
        compiler_params=_cparams(("arbitrary",)),
        name="moe_experts",
    )(blk_e, n_used, xp, w_up, b_up.reshape(-1, n_e, 1, two_ff), w_down, b_down.reshape(-1, n_e, 1, d))


def _combine_body(dst_ref, gt_ref, h_ref, mod_ref, lng_ref, lnb_ref, yp_hbm,
                  o_ref, buf, sems, *, tt, nc, n_tiles, alpha):
    s = pl.program_id(0)
    slot = s % 2

    @pl.when(s < n_tiles)
    def _():
        def issue(t, c):
            for kk in range(TOP_K):
                q = t * TOP_K + kk
                src = dst_ref[q]
                pltpu.make_async_copy(yp_hbm.at[_tok_rows(src, nc), :], buf.at[slot, kk, _tok_rows(t, nc), :],
                                      sems.at[slot]).start(priority=kk % 2)
            return c

        lax.fori_loop(0, tt, issue, 0, unroll=8)

    @pl.when(s > 0)
    def _():
        prev = 1 - slot
        for kk in range(TOP_K):
            _row_copy_wait(yp_hbm, buf.at[prev, kk], sems.at[prev], tt * nc)
        gt = gt_ref[...]
        cols = []
        for c in range(nc):
            acc = gt[:, 0:1] * buf[prev, 0, pl.ds(c, tt, stride=nc), :]
            for kk in range(1, TOP_K):
                acc = acc + gt[:, kk:kk + 1] * buf[prev, kk, pl.ds(c, tt, stride=nc), :]
            cols.append(acc)
        f = jnp.concatenate(cols, axis=-1)
        o_ref[...] = _layer_norm_rows(alpha * h_ref[...] + mod_ref[5:6, :] * f, lng_ref[...], lnb_ref[...])


def _combine(dest, gates, h1, modtab, ln_g, ln_b, yp, *, alpha, tiles_per_batch, ctx_tile0):
    t, d = h1.shape
    tt = TOK_TILE
    nc = d // LANES
    n_tiles = t // tt
    nxt = lambda s: jnp.minimum(s, n_tiles - 1)
    cur = lambda s: jnp.maximum(s - 1, 0)
    smem_blk = pl.BlockSpec((tt * TOP_K,), lambda s: (nxt(s),), memory_space=pltpu.SMEM)
    mod_map = lambda s: (cur(s) // tiles_per_batch, ((cur(s) % tiles_per_batch) >= ctx_tile0).astype(I32), 0, 0)
    vec = lambda a: a.reshape(1, -1).astype(F32)
    return pl.pallas_call(
        functools.partial(_combine_body, tt=tt, nc=nc, n_tiles=n_tiles, alpha=alpha),
        grid=(n_tiles + 1,),
        in_specs=[smem_blk,
                  pl.BlockSpec((tt, ROUTE_LANES), lambda s: (cur(s), 0)),
                  pl.BlockSpec((tt, d), lambda s: (cur(s), 0)),
                  pl.BlockSpec((None, None, 6, d), mod_map),
                  pl.BlockSpec((1, d), lambda s: (0, 0)),
                  pl.BlockSpec((1, d), lambda s: (0, 0)),
                  pl.BlockSpec(memory_space=pl.ANY)],
        out_specs=pl.BlockSpec((tt, d), lambda s: (cur(s), 0)),
        out_shape=jax.ShapeDtypeStruct((t, d), F32),
        scratch_shapes=[pltpu.VMEM((2, TOP_K, tt * nc, LANES), F32), pltpu.SemaphoreType.DMA((2,))],
        compiler_params=_cparams(("arbitrary",)),
        name="moe_combine",
    )(dest, gates, h1, modtab, vec(ln_g), vec(ln_b), yp)


def _moe_layer(h1, xr, meta, gt, counts, modtab, ln_g, ln_b, layer, w_up, b_up, w_down, b_down, *,
               alpha, tiles_per_batch, ctx_tile0):
    t, d = h1.shape
    n_e = w_up.shape[1]
    r = MOE_ROWS
    n_blocks = -(-(t * TOP_K) // r) + n_e
    cnt = counts[:, 0].astype(I32)
    padded = (cnt + r - 1) // r * r
    pends = jnp.cumsum(padded)
    pstart = pends - padded
    tail = pends[-1] + jnp.arange(n_e, dtype=I32) * r
    zstart = jnp.concatenate([jnp.where(cnt > 0, pends - r, -1),
                              jnp.where(tail < n_blocks * r, tail, -1)]).astype(I32)
    blk_row0 = jnp.arange(n_blocks, dtype=I32) * r
    blk_e = jnp.minimum(jnp.sum((pends[None, :] <= blk_row0[:, None]).astype(I32), axis=1), n_e - 1)
    n_used = (pends[-1] // r).astype(I32).reshape(1)
    ti, rk = meta[:TOP_K], meta[TOP_K:2 * TOP_K]
    first = jnp.sum(jnp.where(ti[:, :, None] == jnp.arange(n_e, dtype=I32), pstart, 0), axis=-1)
    dest = (first + rk).T.reshape(-1).astype(I32)
    xp = _dispatch(dest, zstart, xr, n_blocks * r, d // (2 * LANES))
    yp = _experts(blk_e, n_used, xp, layer, w_up, b_up, w_down, b_down)
    return _combine(dest, gt, h1, modtab, ln_g, ln_b, yp,
                    alpha=alpha, tiles_per_batch=tiles_per_batch, ctx_tile0=ctx_tile0)


def _to_cm_body(lat_ref, ctx_ref, o_ref, *, rows, k, n_lat_steps):
    w = pl.program_id(1)
    for q in range(SUBLANES // k):
        @pl.when((w < n_lat_steps) & (w % (SUBLANES // k) == q))
        def _():
            for i in range(k):
                o_ref[i * rows:(i + 1) * rows, :] = lat_ref[:, q * k + i, :]

    @pl.when(w >= n_lat_steps)
    def _():
        o_ref[...] = ctx_ref[...]


def _to_col_major(h3, n_lat):
    bn, ltot, d = h3.shape
    lc = ltot - n_lat
    rows = n_lat // GRID_W
    k = max(kk for kk in (1, 2, 4, 8) if lc % (kk * rows) == 0)
    ob = k * rows
    n_lat_steps = GRID_W // k
    per_blk = SUBLANES // k
    return pl.pallas_call(
        functools.partial(_to_cm_body, rows=rows, k=k, n_lat_steps=n_lat_steps),
        grid=(bn, n_lat_steps + lc // ob),
        in_specs=[pl.BlockSpec((None, rows, SUBLANES, d),
                               lambda b, w: (b, 0, jnp.minimum(w, n_lat_steps - 1) // per_blk, 0)),
                  pl.BlockSpec((None, ob, d), lambda b, w: (b, jnp.maximum(w, n_lat_steps), 0))],
        out_specs=pl.BlockSpec((None, ob, d), lambda b, w: (b, w, 0)),
        out_shape=jax.ShapeDtypeStruct(h3.shape, h3.dtype),
        compiler_params=_cparams(("arbitrary", "arbitrary")),
        name="to_col_major",
    )(h3.reshape(bn, ltot // GRID_W, GRID_W, d), h3)


def _to_rm_body(lat_ref, ctx_ref, o_ref, *, rows, n_ctx_rows):
    w = pl.program_id(1)
    for i in range(SUBLANES):
        o_ref[0:rows, i, :] = lat_ref[i * rows:(i + 1) * rows, :]
    for m in range(n_ctx_rows):
        o_ref[rows + m, :, :] = ctx_ref[pl.ds(pl.multiple_of(m * GRID_W + w * SUBLANES, SUBLANES), SUBLANES), :]


def _to_row_major(h3, n_lat, keep_ctx=True):
    bn, ltot, d = h3.shape
    lc = ltot - n_lat
    rows = n_lat // GRID_W
    assert lc % GRID_W == 0 and n_lat % lc == 0
    n_ctx_rows = lc // GRID_W if keep_ctx else 0
    out = pl.pallas_call(
        functools.partial(_to_rm_body, rows=rows, n_ctx_rows=n_ctx_rows),
        grid=(bn, GRID_W // SUBLANES),
        in_specs=[pl.BlockSpec((None, SUBLANES * rows, d), lambda b, w: (b, w, 0)),
                  pl.BlockSpec((None, lc, d), lambda b, w: (b, n_lat // lc, 0))],
        out_specs=pl.BlockSpec((None, rows + n_ctx_rows, SUBLANES, d), lambda b, w: (b, 0, w, 0)),
        out_shape=jax.ShapeDtypeStruct((bn, rows + n_ctx_rows, GRID_W, d), h3.dtype),
        compiler_params=_cparams(("arbitrary", "arbitrary")),
        name="to_row_major",
    )(h3, h3)
    return out.reshape(bn, (rows + n_ctx_rows) * GRID_W, d)


def kernel(x, c, ctx, c_ctx, w_ada, b_ada, ln1_g, ln1_b, ln2_g, ln2_b, s5_lam_re, s5_lam_im, s5_log_step, s5_b_re, s5_b_im, s5_c_re, s5_c_im, s5_d, s5_w_glu, gla_w_in, gla_w_a2, gla_b_a2, gla_norm_g, gla_w_out, moe_w_router, moe_b_router, moe_w_up, moe_b_up, moe_w_down, moe_b_down):
    bn, l, d = x.shape
    lc = ctx.shape[1]
    depth = w_ada.shape[0]
    ltot = l + lc
    rows = l // GRID_W
    alpha = (2 * depth) ** 0.25
    assert l % TOK_TILE == 0 and lc % TOK_TILE == 0 and l % S5_CHUNK == 0 and lc % S5_CHUNK == 0
    assert bn < SUBLANES
    tiles_per_batch = ltot // TOK_TILE
    ctx_tile0 = l // TOK_TILE
    tile_kw = dict(tiles_per_batch=tiles_per_batch, ctx_tile0=ctx_tile0)

    cc = jnp.zeros((SUBLANES, d), F32).at[:bn].set(c.astype(F32)).at[bn].set(c_ctx.astype(F32))
    mod = _ada_table(cc, w_ada.astype(F32), b_ada.astype(F32))
    mod_lat = mod[:, :bn].reshape(depth, bn, 1, 6, d)
    mod_ctx = jnp.broadcast_to(mod[:, bn].reshape(depth, 1, 1, 6, d), (depth, bn, 1, 6, d))
    modtab = jnp.concatenate([mod_lat, mod_ctx], axis=2)

    to_cm = functools.partial(_to_col_major, n_lat=l)
    to_rm = functools.partial(_to_row_major, n_lat=l)

    h3 = jnp.concatenate([x.astype(F32), ctx.astype(F32)], axis=1)
    for i in range(depth):
        j = i // 2
        mt = modtab[i]
        if i % 2 == 0:
            ys = []
            for dr, rev in ((0, False), (1, True)):
                wb, wc, a_re, a_im = _s5_prepare(s5_lam_re[j, dr], s5_lam_im[j, dr], s5_log_step[j, dr],
                                                 s5_b_re[j, dr], s5_b_im[j, dr], s5_c_re[j, dr], s5_c_im[j, dr], bn)
                ys.append(_s5_scan(h3, mt, wb, wc, a_re, a_im, rev=rev, n_lat=l).reshape(bn * ltot, d))
            h2d = h3.reshape(bn * ltot, d)
            outs = _post_mixer("s5", h2d, mt, ys[0], ys[1], (s5_d[j],), s5_w_glu[j].astype(BF16),
                               ln1_g[i], ln1_b[i], moe_w_router[i], moe_b_router[i], alpha=alpha, **tile_kw)
        else:
            h3 = to_cm(h3)
            h2d = h3.reshape(bn * ltot, d)
            w_in = gla_w_in[j]
            n_main = w_in.shape[1] - 2 * GLA_GATE_RANK
            w_a = w_in[:, n_main:].reshape(d, 2, GLA_GATE_RANK).transpose(1, 0, 2).astype(BF16)
            q, k, v, g, la_f, la_b = _gla_proj(h2d, mt, w_in[:, :n_main].astype(BF16), w_a,
                                               gla_w_a2[j].astype(BF16), gla_b_a2[j], **tile_kw)
            r3 = lambda a: a.reshape(bn, ltot, a.shape[-1])
            o_f, o_b = [o.reshape(bn * ltot, d) for o in _gla_rec(r3(q), r3(k), r3(v), r3(la_f), r3(la_b), n_lat=l)]
            outs = _post_mixer("gla", h2d, mt, o_f, o_b, (g, gla_norm_g[j]), gla_w_out[j].astype(BF16),
                               ln1_g[i], ln1_b[i], moe_w_router[i], moe_b_router[i], alpha=alpha, **tile_kw)
        h1, xr, meta, gt, counts = outs
        h2 = _moe_layer(h1, xr, meta, gt, counts, mt, ln2_g[i], ln2_b[i],
                        i, moe_w_up, moe_b_up, moe_w_down, moe_b_down, alpha=alpha, **tile_kw)
        h3 = h2.reshape(bn, ltot, d)
        if i % 2 == 1:
            h3 = to_rm(h3, keep_ctx=i < depth - 1)
    return h3[:, :l].astype(x.dtype)
```
